```python
import jax, jax.numpy as jnp
from jax import lax
import numpy as np

D_MODEL = 1024
BATCH = 2
SEQ = 8192
DEPTH = 2

GRID_W = 64
EPS = 1e-6

A_HEADS = 8
A_HEAD_DIM = D_MODEL // 16
A_WIDTH = A_HEADS * A_HEAD_DIM
CONV_WIDTH = 3
B_GROUPS = 4
B_GROUP_DIM = D_MODEL // 8
B_WIDTH = B_GROUPS * B_GROUP_DIM
EVEN_IN = 4 * A_WIDTH + 2 * B_WIDTH
EVEN_MIX = A_WIDTH + B_WIDTH

HEAD_DIM = 128
N_HEADS = D_MODEL // HEAD_DIM
N_KV_HEADS = N_HEADS // 4
Q_WIDTH = N_HEADS * HEAD_DIM
KV_WIDTH = N_KV_HEADS * HEAD_DIM
ODD_IN = 2 * Q_WIDTH + 2 * KV_WIDTH
ROPE_THETA = 10000.0
Q_BLOCK = 128

N_EVEN = (DEPTH + 1) // 2
N_ODD = DEPTH // 2

kernel_name = "hybrid_shortconv_fourier_axial_gqa_encoder"


def rms_norm(x, g):
    xf = x.astype(jnp.float32)
    y = xf * lax.rsqrt(jnp.mean(xf * xf, axis=-1, keepdims=True) + EPS)
    return (y * g.astype(jnp.float32)).astype(x.dtype)


def centred_depthwise_conv(u, w):
    s = u.shape[1]
    pad = CONV_WIDTH // 2
    up = jnp.pad(u, ((0, 0), (pad, CONV_WIDTH - 1 - pad), (0, 0)))
    out = up[:, 0:s] * w[0]
    for tap in range(1, CONV_WIDTH):
        out = out + up[:, tap:tap + s] * w[tap]
    return out


def short_conv_fourier_mixer(h, w_in, conv_w, w_out):
    bsz, s, _ = h.shape
    proj = h @ w_in
    a_x, a_b, a_c, a_z, b_u, b_z = jnp.split(
        proj,
        [A_WIDTH, 2 * A_WIDTH, 3 * A_WIDTH, 4 * A_WIDTH, 4 * A_WIDTH + B_WIDTH],
        axis=-1)
    y_a = a_b * centred_depthwise_conv(a_c * a_x, conv_w) * jax.nn.silu(a_z)
    u = b_u.reshape(bsz, s, B_GROUPS, B_GROUP_DIM).astype(jnp.float32)
    f = jnp.fft.fft2(u, axes=(1, 3), norm="ortho").real
    y_b = f.reshape(bsz, s, B_WIDTH).astype(h.dtype) * jax.nn.silu(b_z)
    return jnp.concatenate([y_a, y_b], axis=-1) @ w_out


def axial_rope_tables(s):
    rows = s // GRID_W
    row = jnp.repeat(jnp.arange(rows), GRID_W).astype(jnp.float32)
    col = jnp.tile(jnp.arange(GRID_W), rows).astype(jnp.float32)
    n_pair = HEAD_DIM // 4
    inv = ROPE_THETA ** (-jnp.arange(n_pair, dtype=jnp.float32) / n_pair)
    ang = jnp.concatenate([row[:, None] * inv, col[:, None] * inv], axis=-1)
    return jnp.cos(ang), jnp.sin(ang)


def apply_rope(x, cos, sin):
    xf = x.astype(jnp.float32).reshape(x.shape[:-1] + (HEAD_DIM // 2, 2))
    x1, x2 = xf[..., 0], xf[..., 1]
    c = cos[None, :, None, :]
    sn = sin[None, :, None, :]
    out = jnp.stack([x1 * c - x2 * sn, x1 * sn + x2 * c], axis=-1)
    return out.reshape(x.shape).astype(x.dtype)


def gqa_axial_mixer(h, w_in, q_gain, k_gain, w_out):
    bsz, s, _ = h.shape
    proj = h @ w_in
    q, k, v, z = jnp.split(
        proj, [Q_WIDTH, Q_WIDTH + KV_WIDTH, Q_WIDTH + 2 * KV_WIDTH], axis=-1)
    q = q.reshape(bsz, s, N_HEADS, HEAD_DIM)
    k = k.reshape(bsz, s, N_KV_HEADS, HEAD_DIM)
    v = v.reshape(bsz, s, N_KV_HEADS, HEAD_DIM)
    cos, sin = axial_rope_tables(s)
    q = apply_rope(rms_norm(q, q_gain), cos, sin)
    k = apply_rope(rms_norm(k, k_gain), cos, sin)
    groups = N_HEADS // N_KV_HEADS
    n_blk = s // Q_BLOCK
    qb = q.reshape(bsz, n_blk, Q_BLOCK, N_KV_HEADS, groups, HEAD_DIM).transpose(1, 0, 3, 4, 2, 5)
    kt = k.transpose(0, 2, 1, 3)
    vt = v.transpose(0, 2, 1, 3)
    scale = HEAD_DIM ** -0.5

    def attend_block(q_blk):
        sc = jnp.einsum('bkgqd,bksd->bkgqs', q_blk, kt,
                        preferred_element_type=jnp.float32) * scale
        p = jax.nn.softmax(sc, axis=-1)
        return jnp.einsum('bkgqs,bksd->bkgqd', p.astype(vt.dtype), vt)

    o = lax.map(attend_block, qb)
    o = o.transpose(1, 0, 4, 2, 3, 5).reshape(bsz, s, Q_WIDTH)
    return (o * jax.nn.silu(z)) @ w_out


def setup_inputs(seed: int = 0) -> dict:
    key = jax.random.key(seed)
    ks = jax.random.split(key, 12)
    f32 = jnp.float32
    x = jax.random.normal(ks[0], (BATCH, SEQ, D_MODEL), f32)
    norm_even = 1.0 + 0.02 * jax.random.normal(ks[1], (N_EVEN, D_MODEL), f32)
    w_in_even = jax.random.normal(ks[2], (N_EVEN, D_MODEL, EVEN_IN), f32) * D_MODEL ** -0.5
    conv_w = jax.random.normal(ks[3], (N_EVEN, CONV_WIDTH, A_WIDTH), f32) * CONV_WIDTH ** -0.5
    w_out_even = jax.random.normal(ks[4], (N_EVEN, EVEN_MIX, D_MODEL), f32) * EVEN_MIX ** -0.5
    norm_odd = 1.0 + 0.02 * jax.random.normal(ks[5], (N_ODD, D_MODEL), f32)
    w_in_odd = jax.random.normal(ks[6], (N_ODD, D_MODEL, ODD_IN), f32) * D_MODEL ** -0.5
    q_gain = 1.0 + 0.02 * jax.random.normal(ks[7], (N_ODD, HEAD_DIM), f32)
    k_gain = 1.0 + 0.02 * jax.random.normal(ks[8], (N_ODD, HEAD_DIM), f32)
    w_out_odd = jax.random.normal(ks[9], (N_ODD, Q_WIDTH, D_MODEL), f32) * Q_WIDTH ** -0.5
    final_norm = 1.0 + 0.02 * jax.random.normal(ks[10], (D_MODEL,), f32)
    return {"x": x, "norm_even": norm_even, "w_in_even": w_in_even, "conv_w": conv_w,
            "w_out_even": w_out_even, "norm_odd": norm_odd, "w_in_odd": w_in_odd,
            "q_gain": q_gain, "k_gain": k_gain, "w_out_odd": w_out_odd,
            "final_norm": final_norm}


def reference(x, norm_even, w_in_even, conv_w, w_out_even, norm_odd, w_in_odd,
              q_gain, k_gain, w_out_odd, final_norm):
    for layer in range(DEPTH):
        i = layer // 2
        if layer % 2 == 0:
            x = x + short_conv_fourier_mixer(rms_norm(x, norm_even[i]), w_in_even[i],
                                             conv_w[i], w_out_even[i])
        else:
            x = x + gqa_axial_mixer(rms_norm(x, norm_odd[i]), w_in_odd[i],
                                    q_gain[i], k_gain[i], w_out_odd[i])
    return rms_norm(x, final_norm)
```

```python
import functools
import math

import numpy as np
import jax
import jax.numpy as jnp
from jax import lax
from jax.experimental import pallas as pl
from jax.experimental.pallas import tpu as pltpu

F32 = jnp.float32
BF16 = jnp.bfloat16

EPS = 1e-6
GRID_W = 64
ROPE_THETA = 10000.0
CONV_WIDTH = 3
HEAD_DIM = 128
N_HEADS = 8
N_KV_HEADS = 2
KV_GROUP = N_HEADS // N_KV_HEADS
A_WIDTH = 512
B_WIDTH = 512
B_GROUPS = 4
B_GROUP_DIM = 128
FFT_FAST = 128

VMEM_LIMIT_BYTES = 56 * 1024 * 1024


def _silu(z):
    return z / (1.0 + jnp.exp(-z))


def _rms_norm(x, g):
    return x * lax.rsqrt(jnp.mean(x * x, axis=-1, keepdims=True) + EPS) * g


def _params(n_grid_dims):
    return pltpu.CompilerParams(
        dimension_semantics=("arbitrary",) * n_grid_dims,
        vmem_limit_bytes=VMEM_LIMIT_BYTES)


def _even_in_kernel(x_ref, g_ref, w_ref, cx_ref, ga_ref, bu_ref, gb_ref):
    h = _rms_norm(x_ref[...], g_ref[...]).astype(BF16)
    p = jnp.dot(h, w_ref[...], preferred_element_type=F32)
    a_x = p[:, 0 * A_WIDTH:1 * A_WIDTH]
    a_b = p[:, 1 * A_WIDTH:2 * A_WIDTH]
    a_c = p[:, 2 * A_WIDTH:3 * A_WIDTH]
    a_z = p[:, 3 * A_WIDTH:4 * A_WIDTH]
    b_u = p[:, 4 * A_WIDTH:4 * A_WIDTH + B_WIDTH]
    b_z = p[:, 4 * A_WIDTH + B_WIDTH:]
    cx_ref[...] = (a_c * a_x).astype(cx_ref.dtype)
    ga_ref[...] = (a_b * _silu(a_z)).astype(ga_ref.dtype)
    bu_ref[...] = b_u.astype(bu_ref.dtype)
    gb_ref[...] = _silu(b_z).astype(gb_ref.dtype)


def _even_in(x2, g, w, tm):
    n, d = x2.shape
    e_in = w.shape[1]
    out = jax.ShapeDtypeStruct((n, A_WIDTH), BF16)
    row = lambda i: (i, 0)
    fixed = lambda i: (0, 0)
    return pl.pallas_call(
        _even_in_kernel,
        grid=(n // tm,),
        in_specs=[pl.BlockSpec((tm, d), row),
                  pl.BlockSpec((1, d), fixed),
                  pl.BlockSpec((d, e_in), fixed)],
        out_specs=[pl.BlockSpec((tm, A_WIDTH), row)] * 4,
        out_shape=[out] * 4,
        compiler_params=_params(1),
        name="even_in",
    )(x2, g, w)


def _dft_tables(seq):
    slow = seq // FFT_FAST
    a = np.arange(slow)
    th = 2.0 * np.pi * np.outer(a, a) / slow
    w1 = np.concatenate([np.cos(th), -np.sin(th)], axis=0)
    b = np.arange(FFT_FAST)
    k = a[:, None, None] + slow * b[None, :, None]
    th2 = 2.0 * np.pi * (k * b[None, None, :]) / seq
    cs, sn = np.cos(th2), np.sin(th2)
    t2 = np.concatenate([np.concatenate([cs, sn], axis=2),
                         np.concatenate([-sn, cs], axis=2)], axis=1)
    ch = np.arange(B_GROUP_DIM)
    thc = 2.0 * np.pi * np.outer(ch, ch) / B_GROUP_DIM
    wc = np.concatenate([np.cos(thc), np.sin(thc)], axis=0)
    return tuple(jnp.asarray(t, dtype=F32) for t in (w1, t2, wc))


def _dft1_kernel(w_ref, u_ref, y_ref):
    slow = u_ref.shape[1]
    y = jnp.dot(w_ref[...].astype(BF16), u_ref[0], preferred_element_type=F32)
    y_ref[0, 0] = y[:slow].astype(y_ref.dtype)
    y_ref[0, 1] = y[slow:].astype(y_ref.dtype)


def _dft1(w1, u3, lanes):
    bsz, slow, width = u3.shape
    return pl.pallas_call(
        _dft1_kernel,
        grid=(bsz, width // lanes),
        in_specs=[pl.BlockSpec((2 * slow, slow), lambda b, j: (0, 0)),
                  pl.BlockSpec((1, slow, lanes), lambda b, j: (b, 0, j))],
        out_specs=pl.BlockSpec((1, 2, slow, lanes), lambda b, j: (b, 0, 0, j)),
        out_shape=jax.ShapeDtypeStruct((bsz, 2, slow, width), BF16),
        compiler_params=_params(2),
        name="dft_stage1",
    )(w1, u3)


def _dft2_kernel(y_ref, t_ref, wc_ref, f_ref, *, cc, scale):
    gd = B_GROUP_DIM
    for ci in range(cc):
        rows = slice(ci * FFT_FAST, (ci + 1) * FFT_FAST)
        ys = jnp.concatenate([y_ref[0, 0, rows, :], y_ref[0, 1, rows, :]], axis=0)
        p = jnp.dot(t_ref[ci].astype(BF16), ys, preferred_element_type=F32).astype(BF16)
        lhs = jnp.concatenate(
            [jnp.concatenate([p[:FFT_FAST, g * gd:(g + 1) * gd], p[FFT_FAST:, g * gd:(g + 1) * gd]], axis=1)
             for g in range(B_GROUPS)], axis=0)
        f = jnp.dot(lhs, wc_ref[...].astype(BF16), preferred_element_type=F32) * scale
        for g in range(B_GROUPS):
            f_ref[0, :, ci * B_WIDTH + g * gd:ci * B_WIDTH + (g + 1) * gd] = (
                f[g * FFT_FAST:(g + 1) * FFT_FAST].astype(f_ref.dtype))


def _dft2(y4, t2, wc, cc, scale):
    bsz, _, seq, width = y4.shape
    slow = seq // FFT_FAST
    return pl.pallas_call(
        functools.partial(_dft2_kernel, cc=cc, scale=scale),
        grid=(slow // cc, bsz),
        in_specs=[pl.BlockSpec((1, 2, cc * FFT_FAST, width), lambda j, b: (b, 0, j, 0)),
                  pl.BlockSpec((cc, 2 * FFT_FAST, 2 * FFT_FAST), lambda j, b: (j, 0, 0)),
                  pl.BlockSpec(wc.shape, lambda j, b: (0, 0))],
        out_specs=pl.BlockSpec((1, FFT_FAST, cc * width), lambda j, b: (b, 0, j)),
        out_shape=jax.ShapeDtypeStruct((bsz, FFT_FAST, slow * width), BF16),
        compiler_params=_params(2),
        name="dft_stage2",
    )(y4, t2, wc)


def _rope_tables(seq):
    t = np.arange(seq)
    row = (t // GRID_W).astype(np.float64)
    col = (t % GRID_W).astype(np.float64)
    n_pair = HEAD_DIM // 4
    inv = ROPE_THETA ** (-np.arange(n_pair, dtype=np.float64) / n_pair)
    ang = np.concatenate([row[:, None] * inv, col[:, None] * inv], axis=-1)
    cos = np.repeat(np.cos(ang), 2, axis=1)
    sin = np.repeat(np.sin(ang), 2, axis=1)
    sign = np.tile(np.array([-1.0, 1.0]), HEAD_DIM // 2)
    return jnp.asarray(cos, dtype=F32), jnp.asarray(sin * sign, dtype=F32)


def _norm_rope(xh, gain, cos, sin_signed, even_lane):
    xn = _rms_norm(xh, gain)
    partner = jnp.where(even_lane, pltpu.roll(xn, HEAD_DIM - 1, axis=1), pltpu.roll(xn, 1, axis=1))
    return xn * cos + partner * sin_signed


def _mid_kernel(x_ref, cx_ref, cxp_ref, cxn_ref, ga_ref, f_ref, gb_ref, cw_ref, wo_ref,
                g_ref, wi_ref, qg_ref, kg_ref, cos_ref, sin_ref,
                x1_ref, q_ref, k_ref, v_ref, sz_ref, *, q_scale):
    i = pl.program_id(1)
    n_i = pl.num_programs(1)
    tm = x_ref.shape[0]
    cx = cx_ref[...].astype(F32)
    prev_row = jnp.where(i > 0, cxp_ref[7:8, :].astype(F32), 0.0)
    next_row = jnp.where(i < n_i - 1, cxn_ref[0:1, :].astype(F32), 0.0)
    r = lax.broadcasted_iota(jnp.int32, cx.shape, 0)
    up = jnp.where(r == 0, prev_row, pltpu.roll(cx, 1, axis=0))
    dn = jnp.where(r == tm - 1, next_row, pltpu.roll(cx, tm - 1, axis=0))
    conv = up * cw_ref[0:1, :] + cx * cw_ref[1:2, :] + dn * cw_ref[2:3, :]
    y_a = ga_ref[...].astype(F32) * conv
    y_b = f_ref[...].astype(F32) * gb_ref[...].astype(F32)
    y = jnp.concatenate([y_a, y_b], axis=1).astype(BF16)
    x1 = x_ref[...] + jnp.dot(y, wo_ref[...], preferred_element_type=F32)
    x1_ref[...] = x1
    h = _rms_norm(x1, g_ref[...]).astype(BF16)
    p = jnp.dot(h, wi_ref[...], preferred_element_type=F32)
    qw = N_HEADS * HEAD_DIM
    kw = N_KV_HEADS * HEAD_DIM
    cos = cos_ref[...]
    sin = sin_ref[...]
    even_lane = (lax.broadcasted_iota(jnp.int32, cos.shape, 1) & 1) == 0
    for hh in range(N_HEADS):
        qh = _norm_rope(p[:, hh * HEAD_DIM:(hh + 1) * HEAD_DIM], qg_ref[...], cos, sin, even_lane)
        q_ref[0, hh] = (qh * q_scale).astype(q_ref.dtype)
    for hh in range(N_KV_HEADS):
        kh = _norm_rope(p[:, qw + hh * HEAD_DIM:qw + (hh + 1) * HEAD_DIM], kg_ref[...], cos, sin, even_lane)
        k_ref[0, hh] = kh.astype(k_ref.dtype)
        v_ref[0, hh] = p[:, qw + kw + hh * HEAD_DIM:qw + kw + (hh + 1) * HEAD_DIM].astype(v_ref.dtype)
    sz_ref[...] = _silu(p[:, qw + 2 * kw:]).astype(sz_ref.dtype)


def _mid(x2, cx, ga, f2, gb, conv_w, w_out, g_odd, w_in, q_gain, k_gain, cos, sin, bsz, seq, tm, q_scale):
    n, d = x2.shape
    nt = seq // tm
    halo = 8
    tpb = tm // halo
    last_blk = n // halo - 1
    row = lambda b, i: (b * nt + i, 0)
    fixed = lambda b, i: (0, 0)
    prev = lambda b, i: (jnp.maximum((b * nt + i) * tpb - 1, 0), 0)
    nxt = lambda b, i: (jnp.minimum((b * nt + i + 1) * tpb, last_blk), 0)
    pos = lambda b, i: (i, 0)
    head = lambda b, i: (b, 0, i, 0)
    half = pl.BlockSpec((tm, A_WIDTH), row)
    return pl.pallas_call(
        functools.partial(_mid_kernel, q_scale=q_scale),
        grid=(bsz, nt),
        in_specs=[pl.BlockSpec((tm, d), row),
                  half,
                  pl.BlockSpec((halo, A_WIDTH), prev),
                  pl.BlockSpec((halo, A_WIDTH), nxt),
                  half, half, half,
                  pl.BlockSpec(conv_w.shape, fixed),
                  pl.BlockSpec(w_out.shape, fixed),
                  pl.BlockSpec((1, d), fixed),
                  pl.BlockSpec(w_in.shape, fixed),
                  pl.BlockSpec((1, HEAD_DIM), fixed),
                  pl.BlockSpec((1, HEAD_DIM), fixed),
                  pl.BlockSpec((tm, HEAD_DIM), pos),
                  pl.BlockSpec((tm, HEAD_DIM), pos)],
        out_specs=[pl.BlockSpec((tm, d), row),
                   pl.BlockSpec((1, N_HEADS, tm, HEAD_DIM), head),
                   pl.BlockSpec((1, N_KV_HEADS, tm, HEAD_DIM), head),
                   pl.BlockSpec((1, N_KV_HEADS, tm, HEAD_DIM), head),
                   pl.BlockSpec((tm, N_HEADS * HEAD_DIM), row)],
        out_shape=[jax.ShapeDtypeStruct((n, d), F32),
                   jax.ShapeDtypeStruct((bsz, N_HEADS, seq, HEAD_DIM), BF16),
                   jax.ShapeDtypeStruct((bsz, N_KV_HEADS, seq, HEAD_DIM), BF16),
                   jax.ShapeDtypeStruct((bsz, N_KV_HEADS, seq, HEAD_DIM), BF16),
                   jax.ShapeDtypeStruct((n, N_HEADS * HEAD_DIM), BF16)],
        compiler_params=_params(2),
        name="even_out_odd_in",
    )(x2, cx, cx, cx, ga, f2, gb, conv_w, w_out, g_odd, w_in, q_gain, k_gain, cos, sin)


def _attn_kernel(q_ref, k_ref, v_ref, o_ref, *, tk):
    _, grp, tq, hd = q_ref.shape
    seq = k_ref.shape[2]
    m_rows = grp * tq
    q = q_ref[0].reshape(m_rows, hd)

    def body(j, carry):
        m, l, acc = carry
        start = pl.multiple_of(j * tk, tk)
        ks = k_ref[0, 0, pl.ds(start, tk), :]
        vs = v_ref[0, 0, pl.ds(start, tk), :]
        s = lax.dot_general(q, ks, (((1,), (1,)), ((), ())), preferred_element_type=F32)
        m_new = jnp.maximum(m, jnp.max(s, axis=1, keepdims=True))
        alpha = jnp.exp2(m - m_new)
        p = jnp.exp2(s - m_new)
        l = alpha * l + jnp.sum(p, axis=1, keepdims=True)
        acc = alpha * acc + jnp.dot(p.astype(BF16), vs, preferred_element_type=F32)
        return m_new, l, acc

    init = (jnp.full((m_rows, 1), -jnp.inf, F32), jnp.zeros((m_rows, 1), F32), jnp.zeros((m_rows, hd), F32))
    _, l, acc = lax.fori_loop(0, seq // tk, body, init)
    o = acc / l
    for g in range(grp):
        o_ref[0, :, g * hd:(g + 1) * hd] = o[g * tq:(g + 1) * tq].astype(o_ref.dtype)


def _attention(q, k, v, tq, tk):
    bsz, nh, seq, hd = q.shape
    nkv = k.shape[1]
    grp = nh // nkv
    return pl.pallas_call(
        functools.partial(_attn_kernel, tk=tk),
        grid=(bsz, nkv, seq // tq),
        in_specs=[pl.BlockSpec((1, grp, tq, hd), lambda b, h, i: (b, h, i, 0)),
                  pl.BlockSpec((1, 1, seq, hd), lambda b, h, i: (b, h, 0, 0)),
                  pl.BlockSpec((1, 1, seq, hd), lambda b, h, i: (b, h, 0, 0))],
        out_specs=pl.BlockSpec((1, tq, grp * hd), lambda b, h, i: (b, i, h)),
        out_shape=jax.ShapeDtypeStruct((bsz, seq, nh * hd), BF16),
        compiler_params=_params(3),
        name="gqa_attention",
    )(q, k, v)


def _odd_out_kernel(x1_ref, o_ref, sz_ref, w_ref, g_ref, out_ref):
    y = (o_ref[...].astype(F32) * sz_ref[...].astype(F32)).astype(BF16)
    x2 = x1_ref[...] + jnp.dot(y, w_ref[...], preferred_element_type=F32)
    out_ref[...] = _rms_norm(x2, g_ref[...])


def _odd_out(x1, o2, sz, w, g, tm):
    n, d = x1.shape
    row = lambda i: (i, 0)
    fixed = lambda i: (0, 0)
    return pl.pallas_call(
        _odd_out_kernel,
        grid=(n // tm,),
        in_specs=[pl.BlockSpec((tm, d), row),
                  pl.BlockSpec((tm, o2.shape[1]), row),
                  pl.BlockSpec((tm, sz.shape[1]), row),
                  pl.BlockSpec(w.shape, fixed),
                  pl.BlockSpec((1, d), fixed)],
        out_specs=pl.BlockSpec((tm, d), row),
        out_shape=jax.ShapeDtypeStruct((n, d), F32),
        compiler_params=_params(1),
        name="odd_out",
    )(x1, o2, sz, w, g)


def kernel(x, norm_even, w_in_even, conv_w, w_out_even, norm_odd, w_in_odd, q_gain, k_gain, w_out_odd, final_norm):
    bsz, seq, d = x.shape
    assert norm_even.shape[0] == 1 and norm_odd.shape[0] == 1, "one even and one odd layer"
    assert seq % FFT_FAST == 0 and seq % GRID_W == 0
    n = bsz * seq
    slow = seq // FFT_FAST
    tm = min(512, seq)
    x2 = x.reshape(n, d)

    cx, ga, bu, gb = _even_in(x2, norm_even[0][None, :], w_in_even[0].astype(BF16), tm)
    w1, t2, wc = _dft_tables(seq)
    y = _dft1(w1, bu.reshape(bsz, slow, FFT_FAST * B_WIDTH), lanes=min(8192, FFT_FAST * B_WIDTH))
    scale = 1.0 / math.sqrt(seq * B_GROUP_DIM)
    f = _dft2(y.reshape(bsz, 2, seq, B_WIDTH), t2, wc, cc=min(8, slow), scale=scale)
    f2 = f.reshape(n, B_WIDTH)

    cos, sin = _rope_tables(seq)
    q_scale = HEAD_DIM ** -0.5 * math.log2(math.e)
    x1, q, k, v, sz = _mid(x2, cx, ga, f2, gb, conv_w[0], w_out_even[0].astype(BF16),
                           norm_odd[0][None, :], w_in_odd[0].astype(BF16),
                           q_gain[0][None, :], k_gain[0][None, :], cos, sin, bsz, seq, tm, q_scale)

    o = _attention(q, k, v, tq=min(256, seq), tk=min(512, seq))
    out = _odd_out(x1, o.reshape(n, N_HEADS * HEAD_DIM), sz, w_out_odd[0].astype(BF16), final_norm[None, :], tm)
    return out.reshape(bsz, seq, d)
```

```python
import functools
import math

import numpy as np
import jax
import jax.numpy as jnp
from jax import lax
from jax.experimental import pallas as pl
from jax.experimental.pallas import tpu as pltpu

F32 = jnp.float32
BF16 = jnp.bfloat16

EPS = 1e-6
GRID_W = 64
ROPE_THETA = 10000.0
CONV_WIDTH = 3
HEAD_DIM = 128
N_HEADS = 8
N_KV_HEADS = 2
KV_GROUP = N_HEADS // N_KV_HEADS
A_WIDTH = 512
B_WIDTH = 512
B_GROUPS = 4
B_GROUP_DIM = 128
FFT_FAST = 128
V_PAD_ROWS = 16
SCORE_SPAN_LIMIT = 60.0
SHIFT_MARGIN = 1.01
ATTN_UNROLL = 4

VMEM_LIMIT_BYTES = 56 * 1024 * 1024


def _silu(z):
    return z / (1.0 + jnp.exp(-z))


def _rms_norm(x, g):
    return x * lax.rsqrt(jnp.mean(x * x, axis=-1, keepdims=True) + EPS) * g


def _params(n_grid_dims):
    return pltpu.CompilerParams(
        dimension_semantics=("arbitrary",) * n_grid_dims,
        vmem_limit_bytes=VMEM_LIMIT_BYTES)


def _even_in_kernel(x_ref, g_ref, w_ref, cx_ref, ga_ref, bu_ref, gb_ref):
    h = _rms_norm(x_ref[...], g_ref[...]).astype(BF16)
    p = jnp.dot(h, w_ref[...], preferred_element_type=F32)
    a_x = p[:, 0 * A_WIDTH:1 * A_WIDTH]
    a_b = p[:, 1 * A_WIDTH:2 * A_WIDTH]
    a_c = p[:, 2 * A_WIDTH:3 * A_WIDTH]
    a_z = p[:, 3 * A_WIDTH:4 * A_WIDTH]
    b_u = p[:, 4 * A_WIDTH:4 * A_WIDTH + B_WIDTH]
    b_z = p[:, 4 * A_WIDTH + B_WIDTH:]
    cx_ref[...] = (a_c * a_x).astype(cx_ref.dtype)
    ga_ref[...] = (a_b * _silu(a_z)).astype(ga_ref.dtype)
    bu_ref[...] = b_u.astype(bu_ref.dtype)
    gb_ref[...] = _silu(b_z).astype(gb_ref.dtype)


def _even_in(x2, g, w, tm):
    n, d = x2.shape
    e_in = w.shape[1]
    out = jax.ShapeDtypeStruct((n, A_WIDTH), BF16)
    row = lambda i: (i, 0)
    fixed = lambda i: (0, 0)
    return pl.pallas_call(
        _even_in_kernel,
        grid=(n // tm,),
        in_specs=[pl.BlockSpec((tm, d), row),
                  pl.BlockSpec((1, d), fixed),
                  pl.BlockSpec((d, e_in), fixed)],
        out_specs=[pl.BlockSpec((tm, A_WIDTH), row)] * 4,
        out_shape=[out] * 4,
        compiler_params=_params(1),
        name="even_in",
    )(x2, g, w)


def _dft_tables(seq):
    slow = seq // FFT_FAST
    a = np.arange(slow)
    th = 2.0 * np.pi * np.outer(a, a) / slow
    w1 = np.concatenate([np.cos(th), -np.sin(th)], axis=0)
    b = np.arange(FFT_FAST)
    k = a[:, None, None] + slow * b[None, :, None]
    th2 = 2.0 * np.pi * (k * b[None, None, :]) / seq
    cs, sn = np.cos(th2), np.sin(th2)
    t2 = np.concatenate([np.concatenate([cs, sn], axis=2),
                         np.concatenate([-sn, cs], axis=2)], axis=1)
    ch = np.arange(B_GROUP_DIM)
    thc = 2.0 * np.pi * np.outer(ch, ch) / B_GROUP_DIM
    wc = np.concatenate([np.cos(thc), np.sin(thc)], axis=0)
    return tuple(jnp.asarray(t, dtype=F32) for t in (w1, t2, wc))


def _dft1_kernel(w_ref, u_ref, y_ref):
    slow = u_ref.shape[1]
    y = jnp.dot(w_ref[...].astype(BF16), u_ref[0], preferred_element_type=F32)
    y_ref[0, 0] = y[:slow].astype(y_ref.dtype)
    y_ref[0, 1] = y[slow:].astype(y_ref.dtype)


def _dft1(w1, u3, lanes):
    bsz, slow, width = u3.shape
    return pl.pallas_call(
        _dft1_kernel,
        grid=(bsz, width // lanes),
        in_specs=[pl.BlockSpec((2 * slow, slow), lambda b, j: (0, 0)),
                  pl.BlockSpec((1, slow, lanes), lambda b, j: (b, 0, j))],
        out_specs=pl.BlockSpec((1, 2, slow, lanes), lambda b, j: (b, 0, 0, j)),
        out_shape=jax.ShapeDtypeStruct((bsz, 2, slow, width), BF16),
        compiler_params=_params(2),
        name="dft_stage1",
    )(w1, u3)


def _dft2_kernel(y_ref, t_ref, wc_ref, f_ref, *, cc, scale):
    gd = B_GROUP_DIM
    for ci in range(cc):
        rows = slice(ci * FFT_FAST, (ci + 1) * FFT_FAST)
        ys = jnp.concatenate([y_ref[0, 0, rows, :], y_ref[0, 1, rows, :]], axis=0)
        p = jnp.dot(t_ref[ci].astype(BF16), ys, preferred_element_type=F32).astype(BF16)
        lhs = jnp.concatenate(
            [jnp.concatenate([p[:FFT_FAST, g * gd:(g + 1) * gd], p[FFT_FAST:, g * gd:(g + 1) * gd]], axis=1)
             for g in range(B_GROUPS)], axis=0)
        f = jnp.dot(lhs, wc_ref[...].astype(BF16), preferred_element_type=F32) * scale
        for g in range(B_GROUPS):
            f_ref[0, :, ci * B_WIDTH + g * gd:ci * B_WIDTH + (g + 1) * gd] = (
                f[g * FFT_FAST:(g + 1) * FFT_FAST].astype(f_ref.dtype))


def _dft2(y4, t2, wc, cc, scale):
    bsz, _, seq, width = y4.shape
    slow = seq // FFT_FAST
    return pl.pallas_call(
        functools.partial(_dft2_kernel, cc=cc, scale=scale),
        grid=(slow // cc, bsz),
        in_specs=[pl.BlockSpec((1, 2, cc * FFT_FAST, width), lambda j, b: (b, 0, j, 0)),
                  pl.BlockSpec((cc, 2 * FFT_FAST, 2 * FFT_FAST), lambda j, b: (j, 0, 0)),
                  pl.BlockSpec(wc.shape, lambda j, b: (0, 0))],
        out_specs=pl.BlockSpec((1, FFT_FAST, cc * width), lambda j, b: (b, 0, j)),
        out_shape=jax.ShapeDtypeStruct((bsz, FFT_FAST, slow * width), BF16),
        compiler_params=_params(2),
        name="dft_stage2",
    )(y4, t2, wc)


def _rope_tables(seq):
    t = np.arange(seq)
    row = (t // GRID_W).astype(np.float64)
    col = (t % GRID_W).astype(np.float64)
    n_pair = HEAD_DIM // 4
    inv = ROPE_THETA ** (-np.arange(n_pair, dtype=np.float64) / n_pair)
    ang = np.concatenate([row[:, None] * inv, col[:, None] * inv], axis=-1)
    cos = np.repeat(np.cos(ang), 2, axis=1)
    sin = np.repeat(np.sin(ang), 2, axis=1)
    sign = np.tile(np.array([-1.0, 1.0]), HEAD_DIM // 2)
    return jnp.asarray(cos, dtype=F32), jnp.asarray(sin * sign, dtype=F32)


def _norm_rope(xh, gain, cos, sin_signed, even_lane):
    xn = _rms_norm(xh, gain)
    partner = jnp.where(even_lane, pltpu.roll(xn, HEAD_DIM - 1, axis=1), pltpu.roll(xn, 1, axis=1))
    return xn * cos + partner * sin_signed


def _mid_kernel(x_ref, cx_ref, cxp_ref, cxn_ref, ga_ref, f_ref, gb_ref, cw_ref, wo_ref,
                g_ref, wi_ref, qg_ref, kg_ref, cos_ref, sin_ref,
                x1_ref, q_ref, qn_ref, k_ref, v_ref, sz_ref, *, q_scale):
    i = pl.program_id(1)
    n_i = pl.num_programs(1)
    tm = x_ref.shape[0]
    cx = cx_ref[...].astype(F32)
    prev_row = jnp.where(i > 0, cxp_ref[7:8, :].astype(F32), 0.0)
    next_row = jnp.where(i < n_i - 1, cxn_ref[0:1, :].astype(F32), 0.0)
    r = lax.broadcasted_iota(jnp.int32, cx.shape, 0)
    up = jnp.where(r == 0, prev_row, pltpu.roll(cx, 1, axis=0))
    dn = jnp.where(r == tm - 1, next_row, pltpu.roll(cx, tm - 1, axis=0))
    conv = up * cw_ref[0:1, :] + cx * cw_ref[1:2, :] + dn * cw_ref[2:3, :]
    y_a = ga_ref[...].astype(F32) * conv
    y_b = f_ref[...].astype(F32) * gb_ref[...].astype(F32)
    y = jnp.concatenate([y_a, y_b], axis=1).astype(BF16)
    x1 = x_ref[...] + jnp.dot(y, wo_ref[...], preferred_element_type=F32)
    x1_ref[...] = x1
    h = _rms_norm(x1, g_ref[...]).astype(BF16)
    p = jnp.dot(h, wi_ref[...], preferred_element_type=F32)
    qw = N_HEADS * HEAD_DIM
    kw = N_KV_HEADS * HEAD_DIM
    cos = cos_ref[...]
    sin = sin_ref[...]
    even_lane = (lax.broadcasted_iota(jnp.int32, cos.shape, 1) & 1) == 0
    for hh in range(N_HEADS):
        qh = _norm_rope(p[:, hh * HEAD_DIM:(hh + 1) * HEAD_DIM], qg_ref[...], cos, sin, even_lane)
        qt = (qh * q_scale).T.astype(q_ref.dtype)
        q_ref[0, hh] = qt
        qf = qt.astype(F32)
        qn_ref[0, hh] = jnp.sqrt(jnp.sum(qf * qf, axis=0, keepdims=True))
    ones_pad = (lax.broadcasted_iota(jnp.int32, (V_PAD_ROWS, tm), 0) == 0).astype(v_ref.dtype)
    for hh in range(N_KV_HEADS):
        kh = _norm_rope(p[:, qw + hh * HEAD_DIM:qw + (hh + 1) * HEAD_DIM], kg_ref[...], cos, sin, even_lane)
        k_ref[0, hh] = kh.astype(k_ref.dtype)
        vt = p[:, qw + kw + hh * HEAD_DIM:qw + kw + (hh + 1) * HEAD_DIM].T.astype(v_ref.dtype)
        v_ref[0, hh] = jnp.concatenate([vt, ones_pad], axis=0)
    sz_ref[...] = _silu(p[:, qw + 2 * kw:]).astype(sz_ref.dtype)


def _mid(x2, cx, ga, f2, gb, conv_w, w_out, g_odd, w_in, q_gain, k_gain, cos, sin, bsz, seq, tm, q_scale):
    n, d = x2.shape
    nt = seq // tm
    halo = 8
    tpb = tm // halo
    last_blk = n // halo - 1
    row = lambda b, i: (b * nt + i, 0)
    fixed = lambda b, i: (0, 0)
    prev = lambda b, i: (jnp.maximum((b * nt + i) * tpb - 1, 0), 0)
    nxt = lambda b, i: (jnp.minimum((b * nt + i + 1) * tpb, last_blk), 0)
    pos = lambda b, i: (i, 0)
    head = lambda b, i: (b, 0, i, 0)
    head_t = lambda b, i: (b, 0, 0, i)
    half = pl.BlockSpec((tm, A_WIDTH), row)
    return pl.pallas_call(
        functools.partial(_mid_kernel, q_scale=q_scale),
        grid=(bsz, nt),
        in_specs=[pl.BlockSpec((tm, d), row),
                  half,
                  pl.BlockSpec((halo, A_WIDTH), prev),
                  pl.BlockSpec((halo, A_WIDTH), nxt),
                  half, half, half,
                  pl.BlockSpec(conv_w.shape, fixed),
                  pl.BlockSpec(w_out.shape, fixed),
                  pl.BlockSpec((1, d), fixed),
                  pl.BlockSpec(w_in.shape, fixed),
                  pl.BlockSpec((1, HEAD_DIM), fixed),
                  pl.BlockSpec((1, HEAD_DIM), fixed),
                  pl.BlockSpec((tm, HEAD_DIM), pos),
                  pl.BlockSpec((tm, HEAD_DIM), pos)],
        out_specs=[pl.BlockSpec((tm, d), row),
                   pl.BlockSpec((1, N_HEADS, HEAD_DIM, tm), head_t),
                   pl.BlockSpec((1, N_HEADS, 1, tm), head_t),
                   pl.BlockSpec((1, N_KV_HEADS, tm, HEAD_DIM), head),
                   pl.BlockSpec((1, N_KV_HEADS, HEAD_DIM + V_PAD_ROWS, tm), head_t),
                   pl.BlockSpec((tm, N_HEADS * HEAD_DIM), row)],
        out_shape=[jax.ShapeDtypeStruct((n, d), F32),
                   jax.ShapeDtypeStruct((bsz, N_HEADS, HEAD_DIM, seq), BF16),
                   jax.ShapeDtypeStruct((bsz, N_HEADS, 1, seq), F32),
                   jax.ShapeDtypeStruct((bsz, N_KV_HEADS, seq, HEAD_DIM), BF16),
                   jax.ShapeDtypeStruct((bsz, N_KV_HEADS, HEAD_DIM + V_PAD_ROWS, seq), BF16),
                   jax.ShapeDtypeStruct((n, N_HEADS * HEAD_DIM), BF16)],
        compiler_params=_params(2),
        name="even_out_odd_in",
    )(x2, cx, cx, cx, ga, f2, gb, conv_w, w_out, g_odd, w_in, q_gain, k_gain, cos, sin)


def _attn_kernel(q_ref, qn_ref, k_ref, v_ref, o_ref, kmax_ref, acc_ref, *, tk):
    _, grp, hd, tq = q_ref.shape
    seq = k_ref.shape[2]
    n_chunks = seq // tk
    v_rows = v_ref.shape[2]

    @pl.when(pl.program_id(2) == 0)
    def _key_norm_bound():
        def body(j, best):
            kc = k_ref[0, 0, pl.ds(pl.multiple_of(j * tk, tk), tk), :].astype(F32)
            return jnp.maximum(best, jnp.sum(kc * kc, axis=1, keepdims=True))
        best = lax.fori_loop(0, n_chunks, body, jnp.zeros((tk, 1), F32))
        kmax_ref[0] = jnp.sqrt(jnp.max(best))

    shift = [qn_ref[0, g] * (kmax_ref[0] * SHIFT_MARGIN) for g in range(grp)]
    span = functools.reduce(jnp.maximum, [jnp.max(c) for c in shift])

    def chunk(j):
        start = pl.multiple_of(j * tk, tk)
        return k_ref[0, 0, pl.ds(start, tk), :], v_ref[0, 0, :, pl.ds(start, tk)]

    @pl.when(span <= SCORE_SPAN_LIMIT)
    def _bounded_shift():
        q_all = jnp.concatenate([q_ref[0, g] for g in range(grp)], axis=1)
        shift_all = jnp.concatenate(shift, axis=1)

        def body(j, acc):
            kc, vc = chunk(j)
            s = jnp.dot(kc, q_all, preferred_element_type=F32)
            p = jnp.exp2(s - shift_all).astype(BF16)
            return acc + jnp.dot(vc, p, preferred_element_type=F32)
        acc = lax.fori_loop(0, n_chunks, body, jnp.zeros((v_rows, grp * tq), F32), unroll=ATTN_UNROLL)
        for g in range(grp):
            acc_ref[g] = acc[:, g * tq:(g + 1) * tq]

    @pl.when(span > SCORE_SPAN_LIMIT)
    def _online_max():
        for g in range(grp):
            def body(j, carry):
                m, acc = carry
                kc, vc = chunk(j)
                s = jnp.dot(kc, q_ref[0, g], preferred_element_type=F32)
                m_new = jnp.maximum(m, jnp.max(s, axis=0, keepdims=True))
                p = jnp.exp2(s - m_new).astype(BF16)
                acc = jnp.exp2(m - m_new) * acc + jnp.dot(vc, p, preferred_element_type=F32)
                return m_new, acc
            init = (jnp.full((1, tq), -jnp.inf, F32), jnp.zeros((v_rows, tq), F32))
            _, acc = lax.fori_loop(0, n_chunks, body, init)
            acc_ref[g] = acc

    for g in range(grp):
        acc = acc_ref[g]
        o_t = acc[:hd] / acc[hd:hd + 1]
        o_ref[0, :, g * hd:(g + 1) * hd] = o_t.T.astype(o_ref.dtype)


def _attention(qt, qn, k, vt, tq, tk):
    bsz, nh, hd, seq = qt.shape
    nkv = k.shape[1]
    grp = nh // nkv
    v_rows = vt.shape[2]
    return pl.pallas_call(
        functools.partial(_attn_kernel, tk=tk),
        grid=(bsz, nkv, seq // tq),
        in_specs=[pl.BlockSpec((1, grp, hd, tq), lambda b, h, i: (b, h, 0, i)),
                  pl.BlockSpec((1, grp, 1, tq), lambda b, h, i: (b, h, 0, i)),
                  pl.BlockSpec((1, 1, seq, hd), lambda b, h, i: (b, h, 0, 0)),
                  pl.BlockSpec((1, 1, v_rows, seq), lambda b, h, i: (b, h, 0, 0))],
        out_specs=pl.BlockSpec((1, tq, grp * hd), lambda b, h, i: (b, i, h)),
        out_shape=jax.ShapeDtypeStruct((bsz, seq, nh * hd), BF16),
        scratch_shapes=[pltpu.SMEM((1,), F32), pltpu.VMEM((grp, v_rows, tq), F32)],
        compiler_params=_params(3),
        name="gqa_attention",
    )(qt, qn, k, vt)


def _odd_out_kernel(x1_ref, o_ref, sz_ref, w_ref, g_ref, out_ref):
    y = (o_ref[...].astype(F32) * sz_ref[...].astype(F32)).astype(BF16)
    x2 = x1_ref[...] + jnp.dot(y, w_ref[...], preferred_element_type=F32)
    out_ref[...] = _rms_norm(x2, g_ref[...])


def _odd_out(x1, o2, sz, w, g, tm):
    n, d = x1.shape
    row = lambda i: (i, 0)
    fixed = lambda i: (0, 0)
    return pl.pallas_call(
        _odd_out_kernel,
        grid=(n // tm,),
        in_specs=[pl.BlockSpec((tm, d), row),
                  pl.BlockSpec((tm, o2.shape[1]), row),
                  pl.BlockSpec((tm, sz.shape[1]), row),
                  pl.BlockSpec(w.shape, fixed),
                  pl.BlockSpec((1, d), fixed)],
        out_specs=pl.BlockSpec((tm, d), row),
        out_shape=jax.ShapeDtypeStruct((n, d), F32),
        compiler_params=_params(1),
        name="odd_out",
    )(x1, o2, sz, w, g)


def kernel(x, norm_even, w_in_even, conv_w, w_out_even, norm_odd, w_in_odd, q_gain, k_gain, w_out_odd, final_norm):
    bsz, seq, d = x.shape
    assert norm_even.shape[0] == 1 and norm_odd.shape[0] == 1, "one even and one odd layer"
    assert seq % FFT_FAST == 0 and seq % GRID_W == 0
    n = bsz * seq
    slow = seq // FFT_FAST
    tm = min(512, seq)
    x2 = x.reshape(n, d)

    cx, ga, bu, gb = _even_in(x2, norm_even[0][None, :], w_in_even[0].astype(BF16), tm)
    w1, t2, wc = _dft_tables(seq)
    y = _dft1(w1, bu.reshape(bsz, slow, FFT_FAST * B_WIDTH), lanes=min(8192, FFT_FAST * B_WIDTH))
    scale = 1.0 / math.sqrt(seq * B_GROUP_DIM)
    f = _dft2(y.reshape(bsz, 2, seq, B_WIDTH), t2, wc, cc=min(8, slow), scale=scale)
    f2 = f.reshape(n, B_WIDTH)

    cos, sin = _rope_tables(seq)
    q_scale = HEAD_DIM ** -0.5 * math.log2(math.e)
    x1, qt, qn, k, vt, sz = _mid(x2, cx, ga, f2, gb, conv_w[0], w_out_even[0].astype(BF16),
                                 norm_odd[0][None, :], w_in_odd[0].astype(BF16),
                                 q_gain[0][None, :], k_gain[0][None, :], cos, sin, bsz, seq, tm, q_scale)

    o = _attention(qt, qn, k, vt, tq=min(256, seq), tk=min(512, seq))
    out = _odd_out(x1, o.reshape(n, N_HEADS * HEAD_DIM), sz, w_out_odd[0].astype(BF16), final_norm[None, :], tm)
    return out.reshape(bsz, seq, d)
```

```python
import functools
import math

import numpy as np
import jax
import jax.numpy as jnp
from jax import lax
from jax.experimental import pallas as pl
from jax.experimental.pallas import tpu as pltpu

F32 = jnp.float32
BF16 = jnp.bfloat16

EPS = 1e-6
GRID_W = 64
ROPE_THETA = 10000.0
CONV_WIDTH = 3
HEAD_DIM = 128
N_HEADS = 8
N_KV_HEADS = 2
KV_GROUP = N_HEADS // N_KV_HEADS
A_WIDTH = 512
B_WIDTH = 512
B_GROUPS = 4
B_GROUP_DIM = 128
FFT_FAST = 128
V_PAD_ROWS = 16
SCORE_SPAN_LIMIT = 60.0
SHIFT_MARGIN = 1.01
MID_SUBTILES = 2
ATTN_UNROLL = 16

VMEM_LIMIT_BYTES = 56 * 1024 * 1024


def _silu(z):
    return z / (1.0 + jnp.exp(-z))


def _rms_norm(x, g):
    return x * lax.rsqrt(jnp.mean(x * x, axis=-1, keepdims=True) + EPS) * g


def _params(n_grid_dims):
    return pltpu.CompilerParams(
        dimension_semantics=("arbitrary",) * n_grid_dims,
        vmem_limit_bytes=VMEM_LIMIT_BYTES)


def _even_in_kernel(x_ref, g_ref, w_ref, cx_ref, ga_ref, bu_ref, gb_ref):
    h = _rms_norm(x_ref[...], g_ref[...]).astype(BF16)
    p = jnp.dot(h, w_ref[...], preferred_element_type=F32)
    a_x = p[:, 0 * A_WIDTH:1 * A_WIDTH]
    a_b = p[:, 1 * A_WIDTH:2 * A_WIDTH]
    a_c = p[:, 2 * A_WIDTH:3 * A_WIDTH]
    a_z = p[:, 3 * A_WIDTH:4 * A_WIDTH]
    b_u = p[:, 4 * A_WIDTH:4 * A_WIDTH + B_WIDTH]
    b_z = p[:, 4 * A_WIDTH + B_WIDTH:]
    cx_ref[...] = (a_c * a_x).astype(cx_ref.dtype)
    ga_ref[...] = (a_b * _silu(a_z)).astype(ga_ref.dtype)
    bu_ref[...] = b_u.astype(bu_ref.dtype)
    gb_ref[...] = _silu(b_z).astype(gb_ref.dtype)


def _even_in(x2, g, w, tm):
    n, d = x2.shape
    e_in = w.shape[1]
    out = jax.ShapeDtypeStruct((n, A_WIDTH), BF16)
    row = lambda i: (i, 0)
    fixed = lambda i: (0, 0)
    return pl.pallas_call(
        _even_in_kernel,
        grid=(n // tm,),
        in_specs=[pl.BlockSpec((tm, d), row),
                  pl.BlockSpec((1, d), fixed),
                  pl.BlockSpec((d, e_in), fixed)],
        out_specs=[pl.BlockSpec((tm, A_WIDTH), row)] * 4,
        out_shape=[out] * 4,
        compiler_params=_params(1),
        name="even_in",
    )(x2, g, w)


def _dft_tables(seq):
    slow = seq // FFT_FAST
    a = np.arange(slow)
    th = 2.0 * np.pi * np.outer(a, a) / slow
    w1 = np.concatenate([np.cos(th), -np.sin(th)], axis=0)
    b = np.arange(FFT_FAST)
    k = a[:, None, None] + slow * b[None, :, None]
    th2 = 2.0 * np.pi * (k * b[None, None, :]) / seq
    cs, sn = np.cos(th2), np.sin(th2)
    t2 = np.concatenate([np.concatenate([cs, sn], axis=2),
                         np.concatenate([-sn, cs], axis=2)], axis=1)
    ch = np.arange(B_GROUP_DIM)
    thc = 2.0 * np.pi * np.outer(ch, ch) / B_GROUP_DIM
    wc = np.concatenate([np.cos(thc), np.sin(thc)], axis=0)
    return tuple(jnp.asarray(t, dtype=F32) for t in (w1, t2, wc))


def _dft1_kernel(w_ref, u_ref, y_ref):
    slow = u_ref.shape[1]
    y = jnp.dot(w_ref[...].astype(BF16), u_ref[0], preferred_element_type=F32)
    y_ref[0, 0] = y[:slow].astype(y_ref.dtype)
    y_ref[0, 1] = y[slow:].astype(y_ref.dtype)


def _dft1(w1, u3, lanes):
    bsz, slow, width = u3.shape
    return pl.pallas_call(
        _dft1_kernel,
        grid=(bsz, width // lanes),
        in_specs=[pl.BlockSpec((2 * slow, slow), lambda b, j: (0, 0)),
                  pl.BlockSpec((1, slow, lanes), lambda b, j: (b, 0, j))],
        out_specs=pl.BlockSpec((1, 2, slow, lanes), lambda b, j: (b, 0, 0, j)),
        out_shape=jax.ShapeDtypeStruct((bsz, 2, slow, width), BF16),
        compiler_params=_params(2),
        name="dft_stage1",
    )(w1, u3)


def _dft2_kernel(y_ref, t_ref, wc_ref, f_ref, *, cc, scale):
    gd = B_GROUP_DIM
    for ci in range(cc):
        rows = slice(ci * FFT_FAST, (ci + 1) * FFT_FAST)
        ys = jnp.concatenate([y_ref[0, 0, rows, :], y_ref[0, 1, rows, :]], axis=0)
        p = jnp.dot(t_ref[ci].astype(BF16), ys, preferred_element_type=F32).astype(BF16)
        lhs = jnp.concatenate(
            [jnp.concatenate([p[:FFT_FAST, g * gd:(g + 1) * gd], p[FFT_FAST:, g * gd:(g + 1) * gd]], axis=1)
             for g in range(B_GROUPS)], axis=0)
        f = jnp.dot(lhs, wc_ref[...].astype(BF16), preferred_element_type=F32) * scale
        for g in range(B_GROUPS):
            f_ref[0, :, ci * B_WIDTH + g * gd:ci * B_WIDTH + (g + 1) * gd] = (
                f[g * FFT_FAST:(g + 1) * FFT_FAST].astype(f_ref.dtype))


def _dft2(y4, t2, wc, cc, scale):
    bsz, _, seq, width = y4.shape
    slow = seq // FFT_FAST
    return pl.pallas_call(
        functools.partial(_dft2_kernel, cc=cc, scale=scale),
        grid=(slow // cc, bsz),
        in_specs=[pl.BlockSpec((1, 2, cc * FFT_FAST, width), lambda j, b: (b, 0, j, 0)),
                  pl.BlockSpec((cc, 2 * FFT_FAST, 2 * FFT_FAST), lambda j, b: (j, 0, 0)),
                  pl.BlockSpec(wc.shape, lambda j, b: (0, 0))],
        out_specs=pl.BlockSpec((1, FFT_FAST, cc * width), lambda j, b: (b, 0, j)),
        out_shape=jax.ShapeDtypeStruct((bsz, FFT_FAST, slow * width), BF16),
        compiler_params=_params(2),
        name="dft_stage2",
    )(y4, t2, wc)


def _rope_tables(seq):
    t = np.arange(seq)
    row = (t // GRID_W).astype(np.float64)
    col = (t % GRID_W).astype(np.float64)
    n_pair = HEAD_DIM // 4
    inv = ROPE_THETA ** (-np.arange(n_pair, dtype=np.float64) / n_pair)
    ang = np.concatenate([row[:, None] * inv, col[:, None] * inv], axis=-1)
    cos = np.repeat(np.cos(ang), 2, axis=1)
    sin = np.repeat(np.sin(ang), 2, axis=1)
    sign = np.tile(np.array([-1.0, 1.0]), HEAD_DIM // 2)
    return jnp.asarray(cos, dtype=F32), jnp.asarray(sin * sign, dtype=F32)


def _norm_rope(xh, gain, cos, sin_signed, even_lane):
    xn = _rms_norm(xh, gain)
    partner = jnp.where(even_lane, pltpu.roll(xn, HEAD_DIM - 1, axis=1), pltpu.roll(xn, 1, axis=1))
    return xn * cos + partner * sin_signed


def _mid_kernel(x_ref, cx_ref, cxp_ref, cxn_ref, ga_ref, f_ref, gb_ref, cw_ref, wo_ref,
                g_ref, wi_ref, qg_ref, kg_ref, cos_ref, sin_ref,
                x1_ref, q_ref, qn_ref, k_ref, v_ref, sz_ref, *, q_scale):
    i = pl.program_id(1)
    n_i = pl.num_programs(1)
    tm = x_ref.shape[0]
    cx = cx_ref[...].astype(F32)
    prev_row = jnp.where(i > 0, cxp_ref[7:8, :].astype(F32), 0.0)
    next_row = jnp.where(i < n_i - 1, cxn_ref[0:1, :].astype(F32), 0.0)
    r = lax.broadcasted_iota(jnp.int32, cx.shape, 0)
    up = jnp.where(r == 0, prev_row, pltpu.roll(cx, 1, axis=0))
    dn = jnp.where(r == tm - 1, next_row, pltpu.roll(cx, tm - 1, axis=0))
    conv = up * cw_ref[0:1, :] + cx * cw_ref[1:2, :] + dn * cw_ref[2:3, :]
    y_a = ga_ref[...].astype(F32) * conv
    y_b = f_ref[...].astype(F32) * gb_ref[...].astype(F32)
    y = jnp.concatenate([y_a, y_b], axis=1).astype(BF16)
    qw = N_HEADS * HEAD_DIM
    kw = N_KV_HEADS * HEAD_DIM
    ts = tm // MID_SUBTILES
    even_lane = (lax.broadcasted_iota(jnp.int32, (ts, HEAD_DIM), 1) & 1) == 0
    ones_pad = (lax.broadcasted_iota(jnp.int32, (V_PAD_ROWS, ts), 0) == 0).astype(v_ref.dtype)
    for sub in range(MID_SUBTILES):
        rows = slice(sub * ts, (sub + 1) * ts)
        x1 = x_ref[rows, :] + jnp.dot(y[rows], wo_ref[...], preferred_element_type=F32)
        x1_ref[rows, :] = x1
        h = _rms_norm(x1, g_ref[...]).astype(BF16)
        p = jnp.dot(h, wi_ref[...], preferred_element_type=F32)
        cos = cos_ref[rows, :]
        sin = sin_ref[rows, :]
        for hh in range(N_HEADS):
            qh = _norm_rope(p[:, hh * HEAD_DIM:(hh + 1) * HEAD_DIM], qg_ref[...], cos, sin, even_lane)
            qt = (qh * q_scale).T.astype(q_ref.dtype)
            q_ref[0, hh, :, rows] = qt
            qf = qt.astype(F32)
            qn_ref[0, hh, :, rows] = jnp.sqrt(jnp.sum(qf * qf, axis=0, keepdims=True))
        for hh in range(N_KV_HEADS):
            kh = _norm_rope(p[:, qw + hh * HEAD_DIM:qw + (hh + 1) * HEAD_DIM], kg_ref[...], cos, sin, even_lane)
            k_ref[0, hh, rows, :] = kh.astype(k_ref.dtype)
            vt = p[:, qw + kw + hh * HEAD_DIM:qw + kw + (hh + 1) * HEAD_DIM].T.astype(v_ref.dtype)
            v_ref[0, hh, :, rows] = jnp.concatenate([vt, ones_pad], axis=0)
        sz_ref[rows, :] = _silu(p[:, qw + 2 * kw:]).astype(sz_ref.dtype)


def _mid(x2, cx, ga, f2, gb, conv_w, w_out, g_odd, w_in, q_gain, k_gain, cos, sin, bsz, seq, tm, q_scale):
    n, d = x2.shape
    nt = seq // tm
    halo = 8
    tpb = tm // halo
    last_blk = n // halo - 1
    row = lambda b, i: (b * nt + i, 0)
    fixed = lambda b, i: (0, 0)
    prev = lambda b, i: (jnp.maximum((b * nt + i) * tpb - 1, 0), 0)
    nxt = lambda b, i: (jnp.minimum((b * nt + i + 1) * tpb, last_blk), 0)
    pos = lambda b, i: (i, 0)
    head = lambda b, i: (b, 0, i, 0)
    head_t = lambda b, i: (b, 0, 0, i)
    half = pl.BlockSpec((tm, A_WIDTH), row)
    return pl.pallas_call(
        functools.partial(_mid_kernel, q_scale=q_scale),
        grid=(bsz, nt),
        in_specs=[pl.BlockSpec((tm, d), row),
                  half,
                  pl.BlockSpec((halo, A_WIDTH), prev),
                  pl.BlockSpec((halo, A_WIDTH), nxt),
                  half, half, half,
                  pl.BlockSpec(conv_w.shape, fixed),
                  pl.BlockSpec(w_out.shape, fixed),
                  pl.BlockSpec((1, d), fixed),
                  pl.BlockSpec(w_in.shape, fixed),
                  pl.BlockSpec((1, HEAD_DIM), fixed),
                  pl.BlockSpec((1, HEAD_DIM), fixed),
                  pl.BlockSpec((tm, HEAD_DIM), pos),
                  pl.BlockSpec((tm, HEAD_DIM), pos)],
        out_specs=[pl.BlockSpec((tm, d), row),
                   pl.BlockSpec((1, N_HEADS, HEAD_DIM, tm), head_t),
                   pl.BlockSpec((1, N_HEADS, 1, tm), head_t),
                   pl.BlockSpec((1, N_KV_HEADS, tm, HEAD_DIM), head),
                   pl.BlockSpec((1, N_KV_HEADS, HEAD_DIM + V_PAD_ROWS, tm), head_t),
                   pl.BlockSpec((tm, N_HEADS * HEAD_DIM), row)],
        out_shape=[jax.ShapeDtypeStruct((n, d), F32),
                   jax.ShapeDtypeStruct((bsz, N_HEADS, HEAD_DIM, seq), BF16),
                   jax.ShapeDtypeStruct((bsz, N_HEADS, 1, seq), F32),
                   jax.ShapeDtypeStruct((bsz, N_KV_HEADS, seq, HEAD_DIM), BF16),
                   jax.ShapeDtypeStruct((bsz, N_KV_HEADS, HEAD_DIM + V_PAD_ROWS, seq), BF16),
                   jax.ShapeDtypeStruct((n, N_HEADS * HEAD_DIM), BF16)],
        compiler_params=_params(2),
        name="even_out_odd_in",
    )(x2, cx, cx, cx, ga, f2, gb, conv_w, w_out, g_odd, w_in, q_gain, k_gain, cos, sin)


def _attn_kernel(q_ref, qn_ref, k_ref, v_ref, o_ref, kmax_ref, acc_ref, *, tk):
    _, grp, hd, tq = q_ref.shape
    seq = k_ref.shape[2]
    n_chunks = seq // tk
    v_rows = v_ref.shape[2]

    @pl.when(pl.program_id(2) == 0)
    def _key_norm_bound():
        def body(j, best):
            kc = k_ref[0, 0, pl.ds(pl.multiple_of(j * tk, tk), tk), :].astype(F32)
            return jnp.maximum(best, jnp.sum(kc * kc, axis=1, keepdims=True))
        best = lax.fori_loop(0, n_chunks, body, jnp.zeros((tk, 1), F32))
        kmax_ref[0] = jnp.sqrt(jnp.max(best))

    shift = [qn_ref[0, g] * (kmax_ref[0] * SHIFT_MARGIN) for g in range(grp)]
    span = functools.reduce(jnp.maximum, [jnp.max(c) for c in shift])

    def chunk(j):
        start = pl.multiple_of(j * tk, tk)
        return k_ref[0, 0, pl.ds(start, tk), :], v_ref[0, 0, :, pl.ds(start, tk)]

    bounded = span <= SCORE_SPAN_LIMIT

    @pl.when(bounded)
    def _bounded_shift():
        q_all = jnp.concatenate([q_ref[0, g] for g in range(grp)], axis=1)
        shift_all = jnp.concatenate(shift, axis=1)

        def body(j, acc):
            kc, vc = chunk(j)
            s = jnp.dot(kc, q_all, preferred_element_type=F32)
            p = jnp.exp2(s - shift_all).astype(BF16)
            return acc + jnp.dot(vc, p, preferred_element_type=F32)
        acc = lax.fori_loop(0, n_chunks, body, jnp.zeros((v_rows, grp * tq), F32), unroll=ATTN_UNROLL)
        for g in range(grp):
            acc_ref[g] = acc[:, g * tq:(g + 1) * tq]

    @pl.when(jnp.logical_not(bounded))
    def _online_max():
        for g in range(grp):
            def body(j, carry):
                m, acc = carry
                kc, vc = chunk(j)
                s = jnp.dot(kc, q_ref[0, g], preferred_element_type=F32)
                m_new = jnp.maximum(m, jnp.max(s, axis=0, keepdims=True))
                p = jnp.exp2(s - m_new).astype(BF16)
                acc = jnp.exp2(m - m_new) * acc + jnp.dot(vc, p, preferred_element_type=F32)
                return m_new, acc
            init = (jnp.full((1, tq), -jnp.inf, F32), jnp.zeros((v_rows, tq), F32))
            _, acc = lax.fori_loop(0, n_chunks, body, init)
            acc_ref[g] = acc

    for g in range(grp):
        acc = acc_ref[g]
        o_t = acc[:hd] / acc[hd:hd + 1]
        o_ref[0, :, g * hd:(g + 1) * hd] = o_t.T.astype(o_ref.dtype)


def _attention(qt, qn, k, vt, tq, tk):
    bsz, nh, hd, seq = qt.shape
    nkv = k.shape[1]
    grp = nh // nkv
    v_rows = vt.shape[2]
    return pl.pallas_call(
        functools.partial(_attn_kernel, tk=tk),
        grid=(bsz, nkv, seq // tq),
        in_specs=[pl.BlockSpec((1, grp, hd, tq), lambda b, h, i: (b, h, 0, i)),
                  pl.BlockSpec((1, grp, 1, tq), lambda b, h, i: (b, h, 0, i)),
                  pl.BlockSpec((1, 1, seq, hd), lambda b, h, i: (b, h, 0, 0)),
                  pl.BlockSpec((1, 1, v_rows, seq), lambda b, h, i: (b, h, 0, 0))],
        out_specs=pl.BlockSpec((1, tq, grp * hd), lambda b, h, i: (b, i, h)),
        out_shape=jax.ShapeDtypeStruct((bsz, seq, nh * hd), BF16),
        scratch_shapes=[pltpu.SMEM((1,), F32), pltpu.VMEM((grp, v_rows, tq), F32)],
        compiler_params=_params(3),
        name="gqa_attention",
    )(qt, qn, k, vt)


def _odd_out_kernel(x1_ref, o_ref, sz_ref, w_ref, g_ref, out_ref):
    y = (o_ref[...].astype(F32) * sz_ref[...].astype(F32)).astype(BF16)
    x2 = x1_ref[...] + jnp.dot(y, w_ref[...], preferred_element_type=F32)
    out_ref[...] = _rms_norm(x2, g_ref[...])


def _odd_out(x1, o2, sz, w, g, tm):
    n, d = x1.shape
    row = lambda i: (i, 0)
    fixed = lambda i: (0, 0)
    return pl.pallas_call(
        _odd_out_kernel,
        grid=(n // tm,),
        in_specs=[pl.BlockSpec((tm, d), row),
                  pl.BlockSpec((tm, o2.shape[1]), row),
                  pl.BlockSpec((tm, sz.shape[1]), row),
                  pl.BlockSpec(w.shape, fixed),
                  pl.BlockSpec((1, d), fixed)],
        out_specs=pl.BlockSpec((tm, d), row),
        out_shape=jax.ShapeDtypeStruct((n, d), F32),
        compiler_params=_params(1),
        name="odd_out",
    )(x1, o2, sz, w, g)


def kernel(x, norm_even, w_in_even, conv_w, w_out_even, norm_odd, w_in_odd, q_gain, k_gain, w_out_odd, final_norm):
    bsz, seq, d = x.shape
    assert norm_even.shape[0] == 1 and norm_odd.shape[0] == 1, "one even and one odd layer"
    assert seq % FFT_FAST == 0 and seq % GRID_W == 0
    n = bsz * seq
    slow = seq // FFT_FAST
    tm = min(512, seq)
    x2 = x.reshape(n, d)

    cx, ga, bu, gb = _even_in(x2, norm_even[0][None, :], w_in_even[0].astype(BF16), tm)
    w1, t2, wc = _dft_tables(seq)
    y = _dft1(w1, bu.reshape(bsz, slow, FFT_FAST * B_WIDTH), lanes=min(8192, FFT_FAST * B_WIDTH))
    scale = 1.0 / math.sqrt(seq * B_GROUP_DIM)
    f = _dft2(y.reshape(bsz, 2, seq, B_WIDTH), t2, wc, cc=min(8, slow), scale=scale)
    f2 = f.reshape(n, B_WIDTH)

    cos, sin = _rope_tables(seq)
    q_scale = HEAD_DIM ** -0.5 * math.log2(math.e)
    x1, qt, qn, k, vt, sz = _mid(x2, cx, ga, f2, gb, conv_w[0], w_out_even[0].astype(BF16),
                                 norm_odd[0][None, :], w_in_odd[0].astype(BF16),
                                 q_gain[0][None, :], k_gain[0][None, :], cos, sin, bsz, seq, tm, q_scale)

    o = _attention(qt, qn, k, vt, tq=min(256, seq), tk=min(512, seq))
    out = _odd_out(x1, o.reshape(n, N_HEADS * HEAD_DIM), sz, w_out_odd[0].astype(BF16), final_norm[None, :], tm)
    return out.reshape(bsz, seq, d)
```

```python
import functools
import math

import numpy as np
import jax
import jax.numpy as jnp
from jax import lax
from jax.experimental import pallas as pl
from jax.experimental.pallas import tpu as pltpu

F32 = jnp.float32
BF16 = jnp.bfloat16

EPS = 1e-6
GRID_W = 64
ROPE_THETA = 10000.0
CONV_WIDTH = 3
HEAD_DIM = 128
N_HEADS = 8
N_KV_HEADS = 2
KV_GROUP = N_HEADS // N_KV_HEADS
A_WIDTH = 512
B_WIDTH = 512
B_GROUPS = 4
B_GROUP_DIM = 128
FFT_FAST = 128
V_PAD_ROWS = 16
SCORE_SPAN_LIMIT = 60.0
SHIFT_MARGIN = 1.01
MID_SUBTILES = 2
ATTN_UNROLL = 16

VMEM_LIMIT_BYTES = 56 * 1024 * 1024


def _silu(z):
    return z / (1.0 + jnp.exp(-z))


def _rms_norm(x, g):
    return x * lax.rsqrt(jnp.mean(x * x, axis=-1, keepdims=True) + EPS) * g


def _params(n_grid_dims):
    return pltpu.CompilerParams(
        dimension_semantics=("arbitrary",) * n_grid_dims,
        vmem_limit_bytes=VMEM_LIMIT_BYTES)


def _even_in_kernel(x_ref, g_ref, w_ref, cx_ref, ga_ref, bu_ref, gb_ref):
    h = _rms_norm(x_ref[...], g_ref[...]).astype(BF16)
    p = jnp.dot(h, w_ref[...], preferred_element_type=F32)
    a_x = p[:, 0 * A_WIDTH:1 * A_WIDTH]
    a_b = p[:, 1 * A_WIDTH:2 * A_WIDTH]
    a_c = p[:, 2 * A_WIDTH:3 * A_WIDTH]
    a_z = p[:, 3 * A_WIDTH:4 * A_WIDTH]
    b_u = p[:, 4 * A_WIDTH:4 * A_WIDTH + B_WIDTH]
    b_z = p[:, 4 * A_WIDTH + B_WIDTH:]
    cx_ref[...] = (a_c * a_x).astype(cx_ref.dtype)
    ga_ref[...] = (a_b * _silu(a_z)).astype(ga_ref.dtype)
    bu_ref[...] = b_u.astype(bu_ref.dtype)
    gb_ref[...] = _silu(b_z).astype(gb_ref.dtype)


def _even_in(x2, g, w, tm):
    n, d = x2.shape
    e_in = w.shape[1]
    out = jax.ShapeDtypeStruct((n, A_WIDTH), BF16)
    row = lambda i: (i, 0)
    fixed = lambda i: (0, 0)
    return pl.pallas_call(
        _even_in_kernel,
        grid=(n // tm,),
        in_specs=[pl.BlockSpec((tm, d), row),
                  pl.BlockSpec((1, d), fixed),
                  pl.BlockSpec((d, e_in), fixed)],
        out_specs=[pl.BlockSpec((tm, A_WIDTH), row)] * 4,
        out_shape=[out] * 4,
        compiler_params=_params(1),
        name="even_in",
    )(x2, g, w)


def _dft_tables(seq):
    slow = seq // FFT_FAST
    a = np.arange(slow)
    th = 2.0 * np.pi * np.outer(a, a) / slow
    w1 = np.concatenate([np.cos(th), -np.sin(th)], axis=0)
    b = np.arange(FFT_FAST)
    k = a[:, None, None] + slow * b[None, :, None]
    th2 = 2.0 * np.pi * (k * b[None, None, :]) / seq
    cs, sn = np.cos(th2), np.sin(th2)
    t2 = np.concatenate([np.concatenate([cs, sn], axis=2),
                         np.concatenate([-sn, cs], axis=2)], axis=1)
    ch = np.arange(B_GROUP_DIM)
    thc = 2.0 * np.pi * np.outer(ch, ch) / B_GROUP_DIM
    wc = np.concatenate([np.cos(thc), np.sin(thc)], axis=0)
    return tuple(jnp.asarray(t, dtype=F32) for t in (w1, t2, wc))


def _dft1_kernel(w_ref, u_ref, y_ref):
    _, nb, slow, width = u_ref.shape
    x = jnp.concatenate([u_ref[0, b] for b in range(nb)], axis=1)
    y = jnp.dot(w_ref[...].astype(BF16), x, preferred_element_type=F32)
    for b in range(nb):
        y_ref[0, 0, b] = y[:slow, b * width:(b + 1) * width].astype(y_ref.dtype)
        y_ref[0, 1, b] = y[slow:, b * width:(b + 1) * width].astype(y_ref.dtype)


def _dft1(w1, ut, nb):
    bsz, fast, slow, width = ut.shape
    return pl.pallas_call(
        _dft1_kernel,
        grid=(bsz, fast // nb),
        in_specs=[pl.BlockSpec((2 * slow, slow), lambda b, j: (0, 0)),
                  pl.BlockSpec((1, nb, slow, width), lambda b, j: (b, j, 0, 0))],
        out_specs=pl.BlockSpec((1, 2, nb, slow, width), lambda b, j: (b, 0, j, 0, 0)),
        out_shape=jax.ShapeDtypeStruct((bsz, 2, fast, slow, width), BF16),
        compiler_params=_params(2),
        name="dft_stage1",
    )(w1, ut)


def _dft2_kernel(y_ref, t_ref, wc_ref, f_ref, *, cc, scale):
    gd = B_GROUP_DIM
    for ci in range(cc):
        rows = slice(ci * FFT_FAST, (ci + 1) * FFT_FAST)
        ys = jnp.concatenate([y_ref[0, 0, rows, :], y_ref[0, 1, rows, :]], axis=0)
        p = jnp.dot(t_ref[ci].astype(BF16), ys, preferred_element_type=F32).astype(BF16)
        lhs = jnp.concatenate(
            [jnp.concatenate([p[:FFT_FAST, g * gd:(g + 1) * gd], p[FFT_FAST:, g * gd:(g + 1) * gd]], axis=1)
             for g in range(B_GROUPS)], axis=0)
        f = jnp.dot(lhs, wc_ref[...].astype(BF16), preferred_element_type=F32) * scale
        for g in range(B_GROUPS):
            f_ref[0, ci, :, g * gd:(g + 1) * gd] = f[g * FFT_FAST:(g + 1) * FFT_FAST].astype(f_ref.dtype)


def _dft2(y4, t2, wc, cc, scale):
    bsz, _, seq, width = y4.shape
    slow = seq // FFT_FAST
    return pl.pallas_call(
        functools.partial(_dft2_kernel, cc=cc, scale=scale),
        grid=(slow // cc, bsz),
        in_specs=[pl.BlockSpec((1, 2, cc * FFT_FAST, width), lambda j, b: (b, 0, j, 0)),
                  pl.BlockSpec((cc, 2 * FFT_FAST, 2 * FFT_FAST), lambda j, b: (j, 0, 0)),
                  pl.BlockSpec(wc.shape, lambda j, b: (0, 0))],
        out_specs=pl.BlockSpec((1, cc, FFT_FAST, width), lambda j, b: (b, j, 0, 0)),
        out_shape=jax.ShapeDtypeStruct((bsz, slow, FFT_FAST, width), BF16),
        compiler_params=_params(2),
        name="dft_stage2",
    )(y4, t2, wc)


def _rope_tables(seq):
    t = np.arange(seq)
    row = (t // GRID_W).astype(np.float64)
    col = (t % GRID_W).astype(np.float64)
    n_pair = HEAD_DIM // 4
    inv = ROPE_THETA ** (-np.arange(n_pair, dtype=np.float64) / n_pair)
    ang = np.concatenate([row[:, None] * inv, col[:, None] * inv], axis=-1)
    cos = np.repeat(np.cos(ang), 2, axis=1)
    sin = np.repeat(np.sin(ang), 2, axis=1)
    sign = np.tile(np.array([-1.0, 1.0]), HEAD_DIM // 2)
    return jnp.asarray(cos, dtype=F32), jnp.asarray(sin * sign, dtype=F32)


def _norm_rope(xh, gain, cos, sin_signed, even_lane):
    xn = _rms_norm(xh, gain)
    partner = jnp.where(even_lane, pltpu.roll(xn, HEAD_DIM - 1, axis=1), pltpu.roll(xn, 1, axis=1))
    return xn * cos + partner * sin_signed


def _mid_kernel(x_ref, cx_ref, cxp_ref, cxn_ref, ga_ref, f_ref, gb_ref, cw_ref, wo_ref,
                g_ref, wi_ref, qg_ref, kg_ref, cos_ref, sin_ref,
                x1_ref, q_ref, qn_ref, k_ref, v_ref, sz_ref, *, q_scale):
    i = pl.program_id(1)
    n_i = pl.num_programs(1)
    tm = x_ref.shape[0]
    cx = cx_ref[...].astype(F32)
    prev_row = jnp.where(i > 0, cxp_ref[7:8, :].astype(F32), 0.0)
    next_row = jnp.where(i < n_i - 1, cxn_ref[0:1, :].astype(F32), 0.0)
    r = lax.broadcasted_iota(jnp.int32, cx.shape, 0)
    up = jnp.where(r == 0, prev_row, pltpu.roll(cx, 1, axis=0))
    dn = jnp.where(r == tm - 1, next_row, pltpu.roll(cx, tm - 1, axis=0))
    conv = up * cw_ref[0:1, :] + cx * cw_ref[1:2, :] + dn * cw_ref[2:3, :]
    y_a = ga_ref[...].astype(F32) * conv
    y_b = f_ref[...].astype(F32) * gb_ref[...].astype(F32)
    y = jnp.concatenate([y_a, y_b], axis=1).astype(BF16)
    qw = N_HEADS * HEAD_DIM
    kw = N_KV_HEADS * HEAD_DIM
    ts = tm // MID_SUBTILES
    even_lane = (lax.broadcasted_iota(jnp.int32, (ts, HEAD_DIM), 1) & 1) == 0
    ones_pad = (lax.broadcasted_iota(jnp.int32, (V_PAD_ROWS, ts), 0) == 0).astype(v_ref.dtype)
    for sub in range(MID_SUBTILES):
        rows = slice(sub * ts, (sub + 1) * ts)
        x1 = x_ref[rows, :] + jnp.dot(y[rows], wo_ref[...], preferred_element_type=F32)
        x1_ref[rows, :] = x1
        h = _rms_norm(x1, g_ref[...]).astype(BF16)
        p = jnp.dot(h, wi_ref[...], preferred_element_type=F32)
        cos = cos_ref[rows, :]
        sin = sin_ref[rows, :]
        for hh in range(N_HEADS):
            qh = _norm_rope(p[:, hh * HEAD_DIM:(hh + 1) * HEAD_DIM], qg_ref[...], cos, sin, even_lane)
            qt = (qh * q_scale).T.astype(q_ref.dtype)
            q_ref[0, hh, :, rows] = qt
            qf = qt.astype(F32)
            qn_ref[0, hh, :, rows] = jnp.sqrt(jnp.sum(qf * qf, axis=0, keepdims=True))
        for hh in range(N_KV_HEADS):
            kh = _norm_rope(p[:, qw + hh * HEAD_DIM:qw + (hh + 1) * HEAD_DIM], kg_ref[...], cos, sin, even_lane)
            k_ref[0, hh, rows, :] = kh.astype(k_ref.dtype)
            vt = p[:, qw + kw + hh * HEAD_DIM:qw + kw + (hh + 1) * HEAD_DIM].T.astype(v_ref.dtype)
            v_ref[0, hh, :, rows] = jnp.concatenate([vt, ones_pad], axis=0)
        sz_ref[rows, :] = _silu(p[:, qw + 2 * kw:]).astype(sz_ref.dtype)


def _mid(x2, cx, ga, f2, gb, conv_w, w_out, g_odd, w_in, q_gain, k_gain, cos, sin, bsz, seq, tm, q_scale):
    n, d = x2.shape
    nt = seq // tm
    halo = 8
    tpb = tm // halo
    last_blk = n // halo - 1
    row = lambda b, i: (b * nt + i, 0)
    fixed = lambda b, i: (0, 0)
    prev = lambda b, i: (jnp.maximum((b * nt + i) * tpb - 1, 0), 0)
    nxt = lambda b, i: (jnp.minimum((b * nt + i + 1) * tpb, last_blk), 0)
    pos = lambda b, i: (i, 0)
    head = lambda b, i: (b, 0, i, 0)
    head_t = lambda b, i: (b, 0, 0, i)
    half = pl.BlockSpec((tm, A_WIDTH), row)
    return pl.pallas_call(
        functools.partial(_mid_kernel, q_scale=q_scale),
        grid=(bsz, nt),
        in_specs=[pl.BlockSpec((tm, d), row),
                  half,
                  pl.BlockSpec((halo, A_WIDTH), prev),
                  pl.BlockSpec((halo, A_WIDTH), nxt),
                  half, half, half,
                  pl.BlockSpec(conv_w.shape, fixed),
                  pl.BlockSpec(w_out.shape, fixed),
                  pl.BlockSpec((1, d), fixed),
                  pl.BlockSpec(w_in.shape, fixed),
                  pl.BlockSpec((1, HEAD_DIM), fixed),
                  pl.BlockSpec((1, HEAD_DIM), fixed),
                  pl.BlockSpec((tm, HEAD_DIM), pos),
                  pl.BlockSpec((tm, HEAD_DIM), pos)],
        out_specs=[pl.BlockSpec((tm, d), row),
                   pl.BlockSpec((1, N_HEADS, HEAD_DIM, tm), head_t),
                   pl.BlockSpec((1, N_HEADS, 1, tm), head_t),
                   pl.BlockSpec((1, N_KV_HEADS, tm, HEAD_DIM), head),
                   pl.BlockSpec((1, N_KV_HEADS, HEAD_DIM + V_PAD_ROWS, tm), head_t),
                   pl.BlockSpec((tm, N_HEADS * HEAD_DIM), row)],
        out_shape=[jax.ShapeDtypeStruct((n, d), F32),
                   jax.ShapeDtypeStruct((bsz, N_HEADS, HEAD_DIM, seq), BF16),
                   jax.ShapeDtypeStruct((bsz, N_HEADS, 1, seq), F32),
                   jax.ShapeDtypeStruct((bsz, N_KV_HEADS, seq, HEAD_DIM), BF16),
                   jax.ShapeDtypeStruct((bsz, N_KV_HEADS, HEAD_DIM + V_PAD_ROWS, seq), BF16),
                   jax.ShapeDtypeStruct((n, N_HEADS * HEAD_DIM), BF16)],
        compiler_params=_params(2),
        name="even_out_odd_in",
    )(x2, cx, cx, cx, ga, f2, gb, conv_w, w_out, g_odd, w_in, q_gain, k_gain, cos, sin)


def _attn_kernel(q_ref, qn_ref, k_ref, v_ref, o_ref, kmax_ref, acc_ref, *, tk):
    _, grp, hd, tq = q_ref.shape
    seq = k_ref.shape[2]
    n_chunks = seq // tk
    v_rows = v_ref.shape[2]

    @pl.when(pl.program_id(2) == 0)
    def _key_norm_bound():
        def body(j, best):
            kc = k_ref[0, 0, pl.ds(pl.multiple_of(j * tk, tk), tk), :].astype(F32)
            return jnp.maximum(best, jnp.sum(kc * kc, axis=1, keepdims=True))
        best = lax.fori_loop(0, n_chunks, body, jnp.zeros((tk, 1), F32))
        kmax_ref[0] = jnp.sqrt(jnp.max(best))

    shift = [qn_ref[0, g] * (kmax_ref[0] * SHIFT_MARGIN) for g in range(grp)]
    span = functools.reduce(jnp.maximum, [jnp.max(c) for c in shift])

    def chunk(j):
        start = pl.multiple_of(j * tk, tk)
        return k_ref[0, 0, pl.ds(start, tk), :], v_ref[0, 0, :, pl.ds(start, tk)]

    bounded = span <= SCORE_SPAN_LIMIT

    @pl.when(bounded)
    def _bounded_shift():
        q_all = jnp.concatenate([q_ref[0, g] for g in range(grp)], axis=1)
        shift_all = jnp.concatenate(shift, axis=1)

        def body(j, acc):
            kc, vc = chunk(j)
            s = jnp.dot(kc, q_all, preferred_element_type=F32)
            p = jnp.exp2(s - shift_all).astype(BF16)
            return acc + jnp.dot(vc, p, preferred_element_type=F32)
        acc = lax.fori_loop(0, n_chunks, body, jnp.zeros((v_rows, grp * tq), F32), unroll=ATTN_UNROLL)
        for g in range(grp):
            acc_ref[g] = acc[:, g * tq:(g + 1) * tq]

    @pl.when(jnp.logical_not(bounded))
    def _online_max():
        for g in range(grp):
            def body(j, carry):
                m, acc = carry
                kc, vc = chunk(j)
                s = jnp.dot(kc, q_ref[0, g], preferred_element_type=F32)
                m_new = jnp.maximum(m, jnp.max(s, axis=0, keepdims=True))
                p = jnp.exp2(s - m_new).astype(BF16)
                acc = jnp.exp2(m - m_new) * acc + jnp.dot(vc, p, preferred_element_type=F32)
                return m_new, acc
            init = (jnp.full((1, tq), -jnp.inf, F32), jnp.zeros((v_rows, tq), F32))
            _, acc = lax.fori_loop(0, n_chunks, body, init)
            acc_ref[g] = acc

    for g in range(grp):
        acc = acc_ref[g]
        o_t = acc[:hd] / acc[hd:hd + 1]
        o_ref[0, :, g * hd:(g + 1) * hd] = o_t.T.astype(o_ref.dtype)


def _attention(qt, qn, k, vt, tq, tk):
    bsz, nh, hd, seq = qt.shape
    nkv = k.shape[1]
    grp = nh // nkv
    v_rows = vt.shape[2]
    return pl.pallas_call(
        functools.partial(_attn_kernel, tk=tk),
        grid=(bsz, nkv, seq // tq),
        in_specs=[pl.BlockSpec((1, grp, hd, tq), lambda b, h, i: (b, h, 0, i)),
                  pl.BlockSpec((1, grp, 1, tq), lambda b, h, i: (b, h, 0, i)),
                  pl.BlockSpec((1, 1, seq, hd), lambda b, h, i: (b, h, 0, 0)),
                  pl.BlockSpec((1, 1, v_rows, seq), lambda b, h, i: (b, h, 0, 0))],
        out_specs=pl.BlockSpec((1, tq, grp * hd), lambda b, h, i: (b, i, h)),
        out_shape=jax.ShapeDtypeStruct((bsz, seq, nh * hd), BF16),
        scratch_shapes=[pltpu.SMEM((1,), F32), pltpu.VMEM((grp, v_rows, tq), F32)],
        compiler_params=_params(3),
        name="gqa_attention",
    )(qt, qn, k, vt)


def _odd_out_kernel(x1_ref, o_ref, sz_ref, w_ref, g_ref, out_ref):
    y = (o_ref[...].astype(F32) * sz_ref[...].astype(F32)).astype(BF16)
    x2 = x1_ref[...] + jnp.dot(y, w_ref[...], preferred_element_type=F32)
    out_ref[...] = _rms_norm(x2, g_ref[...])


def _odd_out(x1, o2, sz, w, g, tm):
    n, d = x1.shape
    row = lambda i: (i, 0)
    fixed = lambda i: (0, 0)
    return pl.pallas_call(
        _odd_out_kernel,
        grid=(n // tm,),
        in_specs=[pl.BlockSpec((tm, d), row),
                  pl.BlockSpec((tm, o2.shape[1]), row),
                  pl.BlockSpec((tm, sz.shape[1]), row),
                  pl.BlockSpec(w.shape, fixed),
                  pl.BlockSpec((1, d), fixed)],
        out_specs=pl.BlockSpec((tm, d), row),
        out_shape=jax.ShapeDtypeStruct((n, d), F32),
        compiler_params=_params(1),
        name="odd_out",
    )(x1, o2, sz, w, g)


def kernel(x, norm_even, w_in_even, conv_w, w_out_even, norm_odd, w_in_odd, q_gain, k_gain, w_out_odd, final_norm):
    bsz, seq, d = x.shape
    assert norm_even.shape[0] == 1 and norm_odd.shape[0] == 1, "one even and one odd layer"
    assert seq % FFT_FAST == 0 and seq % GRID_W == 0
    n = bsz * seq
    slow = seq // FFT_FAST
    tm = min(512, seq)
    x2 = x.reshape(n, d)

    cx, ga, bu, gb = _even_in(x2, norm_even[0][None, :], w_in_even[0].astype(BF16), tm)
    w1, t2, wc = _dft_tables(seq)
    ut = bu.reshape(bsz, slow, FFT_FAST, B_WIDTH).transpose(0, 2, 1, 3)
    y = _dft1(w1, ut, nb=16).transpose(0, 1, 3, 2, 4)
    scale = 1.0 / math.sqrt(seq * B_GROUP_DIM)
    f = _dft2(y.reshape(bsz, 2, seq, B_WIDTH), t2, wc, cc=min(8, slow), scale=scale)
    f2 = f.transpose(0, 2, 1, 3).reshape(n, B_WIDTH)

    cos, sin = _rope_tables(seq)
    q_scale = HEAD_DIM ** -0.5 * math.log2(math.e)
    x1, qt, qn, k, vt, sz = _mid(x2, cx, ga, f2, gb, conv_w[0], w_out_even[0].astype(BF16),
                                 norm_odd[0][None, :], w_in_odd[0].astype(BF16),
                                 q_gain[0][None, :], k_gain[0][None, :], cos, sin, bsz, seq, tm, q_scale)

    o = _attention(qt, qn, k, vt, tq=min(256, seq), tk=min(512, seq))
    out = _odd_out(x1, o.reshape(n, N_HEADS * HEAD_DIM), sz, w_out_odd[0].astype(BF16), final_norm[None, :], tm)
    return out.reshape(bsz, seq, d)
```

```python
import functools
import math

import numpy as np
import jax
import jax.numpy as jnp
from jax import lax
from jax.experimental import pallas as pl
from jax.experimental.pallas import tpu as pltpu

F32 = jnp.float32
BF16 = jnp.bfloat16

EPS = 1e-6
GRID_W = 64
ROPE_THETA = 10000.0
CONV_WIDTH = 3
HEAD_DIM = 128
N_HEADS = 8
N_KV_HEADS = 2
KV_GROUP = N_HEADS // N_KV_HEADS
A_WIDTH = 512
B_WIDTH = 512
B_GROUPS = 4
B_GROUP_DIM = 128
FFT_FAST = 128
V_PAD_ROWS = 16
SCORE_SPAN_LIMIT = 60.0
SHIFT_MARGIN = 1.01
MID_SUBTILES = 2
ATTN_UNROLL = 16

VMEM_LIMIT_BYTES = 56 * 1024 * 1024


def _silu(z):
    return z / (1.0 + jnp.exp(-z))


def _rms_norm(x, g):
    return x * lax.rsqrt(jnp.mean(x * x, axis=-1, keepdims=True) + EPS) * g


def _resident(shape):
    zeros = (0,) * len(shape)
    return pl.BlockSpec(shape, lambda *_: zeros, pipeline_mode=pl.Buffered(1))


def _cast_once(first_step, src_ref, dst_ref):
    @pl.when(first_step)
    def _():
        dst_ref[...] = src_ref[...].astype(dst_ref.dtype)


def _params(n_grid_dims):
    return pltpu.CompilerParams(
        dimension_semantics=("arbitrary",) * n_grid_dims,
        vmem_limit_bytes=VMEM_LIMIT_BYTES)


def _even_in_kernel(x_ref, g_ref, w_ref, cx_ref, ga_ref, bu_ref, gb_ref, wb_ref):
    _cast_once(pl.program_id(0) == 0, w_ref, wb_ref)
    h = _rms_norm(x_ref[...], g_ref[...]).astype(BF16)
    p = jnp.dot(h, wb_ref[...], preferred_element_type=F32)
    a_x = p[:, 0 * A_WIDTH:1 * A_WIDTH]
    a_b = p[:, 1 * A_WIDTH:2 * A_WIDTH]
    a_c = p[:, 2 * A_WIDTH:3 * A_WIDTH]
    a_z = p[:, 3 * A_WIDTH:4 * A_WIDTH]
    b_u = p[:, 4 * A_WIDTH:4 * A_WIDTH + B_WIDTH]
    b_z = p[:, 4 * A_WIDTH + B_WIDTH:]
    cx_ref[...] = (a_c * a_x).astype(cx_ref.dtype)
    ga_ref[...] = (a_b * _silu(a_z)).astype(ga_ref.dtype)
    bu_ref[...] = b_u.astype(bu_ref.dtype)
    gb_ref[...] = _silu(b_z).astype(gb_ref.dtype)


def _even_in(x2, g, w, tm):
    n, d = x2.shape
    e_in = w.shape[1]
    out = jax.ShapeDtypeStruct((n, A_WIDTH), BF16)
    row = lambda i: (i, 0)
    fixed = lambda i: (0, 0)
    return pl.pallas_call(
        _even_in_kernel,
        grid=(n // tm,),
        in_specs=[pl.BlockSpec((tm, d), row),
                  pl.BlockSpec((1, d), fixed),
                  _resident((d, e_in))],
        out_specs=[pl.BlockSpec((tm, A_WIDTH), row)] * 4,
        out_shape=[out] * 4,
        scratch_shapes=[pltpu.VMEM((d, e_in), BF16)],
        compiler_params=_params(1),
        name="even_in",
    )(x2, g, w)


def _dft_tables(seq):
    slow = seq // FFT_FAST
    a = np.arange(slow)
    th = 2.0 * np.pi * np.outer(a, a) / slow
    w1 = np.concatenate([np.cos(th), -np.sin(th)], axis=0)
    b = np.arange(FFT_FAST)
    k = a[:, None, None] + slow * b[None, :, None]
    th2 = 2.0 * np.pi * (k * b[None, None, :]) / seq
    cs, sn = np.cos(th2), np.sin(th2)
    t2 = np.concatenate([np.concatenate([cs, sn], axis=2),
                         np.concatenate([-sn, cs], axis=2)], axis=1)
    ch = np.arange(B_GROUP_DIM)
    thc = 2.0 * np.pi * np.outer(ch, ch) / B_GROUP_DIM
    wc = np.concatenate([np.cos(thc), np.sin(thc)], axis=0)
    return tuple(jnp.asarray(t, dtype=F32) for t in (w1, t2, wc))


def _dft1_kernel(w_ref, u_ref, y_ref):
    _, nb, slow, width = u_ref.shape
    x = jnp.concatenate([u_ref[0, b] for b in range(nb)], axis=1)
    y = jnp.dot(w_ref[...].astype(BF16), x, preferred_element_type=F32)
    for b in range(nb):
        y_ref[0, 0, b] = y[:slow, b * width:(b + 1) * width].astype(y_ref.dtype)
        y_ref[0, 1, b] = y[slow:, b * width:(b + 1) * width].astype(y_ref.dtype)


def _dft1(w1, ut, nb):
    bsz, fast, slow, width = ut.shape
    return pl.pallas_call(
        _dft1_kernel,
        grid=(bsz, fast // nb),
        in_specs=[pl.BlockSpec((2 * slow, slow), lambda b, j: (0, 0)),
                  pl.BlockSpec((1, nb, slow, width), lambda b, j: (b, j, 0, 0))],
        out_specs=pl.BlockSpec((1, 2, nb, slow, width), lambda b, j: (b, 0, j, 0, 0)),
        out_shape=jax.ShapeDtypeStruct((bsz, 2, fast, slow, width), BF16),
        compiler_params=_params(2),
        name="dft_stage1",
    )(w1, ut)


def _dft2_kernel(y_ref, t_ref, wc_ref, f_ref, *, cc, scale):
    gd = B_GROUP_DIM
    for ci in range(cc):
        rows = slice(ci * FFT_FAST, (ci + 1) * FFT_FAST)
        ys = jnp.concatenate([y_ref[0, 0, rows, :], y_ref[0, 1, rows, :]], axis=0)
        p = jnp.dot(t_ref[ci].astype(BF16), ys, preferred_element_type=F32).astype(BF16)
        lhs = jnp.concatenate(
            [jnp.concatenate([p[:FFT_FAST, g * gd:(g + 1) * gd], p[FFT_FAST:, g * gd:(g + 1) * gd]], axis=1)
             for g in range(B_GROUPS)], axis=0)
        f = jnp.dot(lhs, wc_ref[...].astype(BF16), preferred_element_type=F32) * scale
        for g in range(B_GROUPS):
            f_ref[0, ci, :, g * gd:(g + 1) * gd] = f[g * FFT_FAST:(g + 1) * FFT_FAST].astype(f_ref.dtype)


def _dft2(y4, t2, wc, cc, scale):
    bsz, _, seq, width = y4.shape
    slow = seq // FFT_FAST
    return pl.pallas_call(
        functools.partial(_dft2_kernel, cc=cc, scale=scale),
        grid=(slow // cc, bsz),
        in_specs=[pl.BlockSpec((1, 2, cc * FFT_FAST, width), lambda j, b: (b, 0, j, 0)),
                  pl.BlockSpec((cc, 2 * FFT_FAST, 2 * FFT_FAST), lambda j, b: (j, 0, 0)),
                  pl.BlockSpec(wc.shape, lambda j, b: (0, 0))],
        out_specs=pl.BlockSpec((1, cc, FFT_FAST, width), lambda j, b: (b, j, 0, 0)),
        out_shape=jax.ShapeDtypeStruct((bsz, slow, FFT_FAST, width), BF16),
        compiler_params=_params(2),
        name="dft_stage2",
    )(y4, t2, wc)


def _rope_tables(seq):
    t = np.arange(seq)
    row = (t // GRID_W).astype(np.float64)
    col = (t % GRID_W).astype(np.float64)
    n_pair = HEAD_DIM // 4
    inv = ROPE_THETA ** (-np.arange(n_pair, dtype=np.float64) / n_pair)
    ang = np.concatenate([row[:, None] * inv, col[:, None] * inv], axis=-1)
    cos = np.repeat(np.cos(ang), 2, axis=1)
    sin = np.repeat(np.sin(ang), 2, axis=1)
    sign = np.tile(np.array([-1.0, 1.0]), HEAD_DIM // 2)
    return jnp.asarray(cos, dtype=F32), jnp.asarray(sin * sign, dtype=F32)


def _norm_rope(xh, gain, cos, sin_signed, even_lane):
    xn = _rms_norm(xh, gain)
    partner = jnp.where(even_lane, pltpu.roll(xn, HEAD_DIM - 1, axis=1), pltpu.roll(xn, 1, axis=1))
    return xn * cos + partner * sin_signed


def _mid_kernel(x_ref, cx_ref, cxp_ref, cxn_ref, ga_ref, f_ref, gb_ref, cw_ref, wo_ref,
                g_ref, wi_ref, qg_ref, kg_ref, cos_ref, sin_ref,
                x1_ref, q_ref, qn_ref, k_ref, v_ref, sz_ref, wob_ref, wib_ref, *, q_scale):
    first_step = (pl.program_id(0) == 0) & (pl.program_id(1) == 0)
    _cast_once(first_step, wo_ref, wob_ref)
    _cast_once(first_step, wi_ref, wib_ref)
    i = pl.program_id(1)
    n_i = pl.num_programs(1)
    tm = x_ref.shape[0]
    cx = cx_ref[...].astype(F32)
    prev_row = jnp.where(i > 0, cxp_ref[7:8, :].astype(F32), 0.0)
    next_row = jnp.where(i < n_i - 1, cxn_ref[0:1, :].astype(F32), 0.0)
    r = lax.broadcasted_iota(jnp.int32, cx.shape, 0)
    up = jnp.where(r == 0, prev_row, pltpu.roll(cx, 1, axis=0))
    dn = jnp.where(r == tm - 1, next_row, pltpu.roll(cx, tm - 1, axis=0))
    conv = up * cw_ref[0:1, :] + cx * cw_ref[1:2, :] + dn * cw_ref[2:3, :]
    y_a = ga_ref[...].astype(F32) * conv
    y_b = f_ref[...].astype(F32) * gb_ref[...].astype(F32)
    y = jnp.concatenate([y_a, y_b], axis=1).astype(BF16)
    qw = N_HEADS * HEAD_DIM
    kw = N_KV_HEADS * HEAD_DIM
    ts = tm // MID_SUBTILES
    even_lane = (lax.broadcasted_iota(jnp.int32, (ts, HEAD_DIM), 1) & 1) == 0
    ones_pad = (lax.broadcasted_iota(jnp.int32, (V_PAD_ROWS, ts), 0) == 0).astype(v_ref.dtype)
    for sub in range(MID_SUBTILES):
        rows = slice(sub * ts, (sub + 1) * ts)
        x1 = x_ref[rows, :] + jnp.dot(y[rows], wob_ref[...], preferred_element_type=F32)
        x1_ref[rows, :] = x1
        h = _rms_norm(x1, g_ref[...]).astype(BF16)
        p = jnp.dot(h, wib_ref[...], preferred_element_type=F32)
        cos = cos_ref[rows, :]
        sin = sin_ref[rows, :]
        for hh in range(N_HEADS):
            qh = _norm_rope(p[:, hh * HEAD_DIM:(hh + 1) * HEAD_DIM], qg_ref[...], cos, sin, even_lane)
            qt = (qh * q_scale).T.astype(q_ref.dtype)
            q_ref[0, hh, :, rows] = qt
            qf = qt.astype(F32)
            qn_ref[0, hh, :, rows] = jnp.sqrt(jnp.sum(qf * qf, axis=0, keepdims=True))
        for hh in range(N_KV_HEADS):
            kh = _norm_rope(p[:, qw + hh * HEAD_DIM:qw + (hh + 1) * HEAD_DIM], kg_ref[...], cos, sin, even_lane)
            k_ref[0, hh, rows, :] = kh.astype(k_ref.dtype)
            vt = p[:, qw + kw + hh * HEAD_DIM:qw + kw + (hh + 1) * HEAD_DIM].T.astype(v_ref.dtype)
            v_ref[0, hh, :, rows] = jnp.concatenate([vt, ones_pad], axis=0)
        sz_ref[rows, :] = _silu(p[:, qw + 2 * kw:]).astype(sz_ref.dtype)


def _mid(x2, cx, ga, f2, gb, conv_w, w_out, g_odd, w_in, q_gain, k_gain, cos, sin, bsz, seq, tm, q_scale):
    n, d = x2.shape
    nt = seq // tm
    halo = 8
    tpb = tm // halo
    last_blk = n // halo - 1
    row = lambda b, i: (b * nt + i, 0)
    fixed = lambda b, i: (0, 0)
    prev = lambda b, i: (jnp.maximum((b * nt + i) * tpb - 1, 0), 0)
    nxt = lambda b, i: (jnp.minimum((b * nt + i + 1) * tpb, last_blk), 0)
    pos = lambda b, i: (i, 0)
    head = lambda b, i: (b, 0, i, 0)
    head_t = lambda b, i: (b, 0, 0, i)
    half = pl.BlockSpec((tm, A_WIDTH), row)
    return pl.pallas_call(
        functools.partial(_mid_kernel, q_scale=q_scale),
        grid=(bsz, nt),
        in_specs=[pl.BlockSpec((tm, d), row),
                  half,
                  pl.BlockSpec((halo, A_WIDTH), prev),
                  pl.BlockSpec((halo, A_WIDTH), nxt),
                  half, half, half,
                  pl.BlockSpec(conv_w.shape, fixed),
                  _resident(w_out.shape),
                  pl.BlockSpec((1, d), fixed),
                  _resident(w_in.shape),
                  pl.BlockSpec((1, HEAD_DIM), fixed),
                  pl.BlockSpec((1, HEAD_DIM), fixed),
                  pl.BlockSpec((tm, HEAD_DIM), pos),
                  pl.BlockSpec((tm, HEAD_DIM), pos)],
        out_specs=[pl.BlockSpec((tm, d), row),
                   pl.BlockSpec((1, N_HEADS, HEAD_DIM, tm), head_t),
                   pl.BlockSpec((1, N_HEADS, 1, tm), head_t),
                   pl.BlockSpec((1, N_KV_HEADS, tm, HEAD_DIM), head),
                   pl.BlockSpec((1, N_KV_HEADS, HEAD_DIM + V_PAD_ROWS, tm), head_t),
                   pl.BlockSpec((tm, N_HEADS * HEAD_DIM), row)],
        out_shape=[jax.ShapeDtypeStruct((n, d), F32),
                   jax.ShapeDtypeStruct((bsz, N_HEADS, HEAD_DIM, seq), BF16),
                   jax.ShapeDtypeStruct((bsz, N_HEADS, 1, seq), F32),
                   jax.ShapeDtypeStruct((bsz, N_KV_HEADS, seq, HEAD_DIM), BF16),
                   jax.ShapeDtypeStruct((bsz, N_KV_HEADS, HEAD_DIM + V_PAD_ROWS, seq), BF16),
                   jax.ShapeDtypeStruct((n, N_HEADS * HEAD_DIM), BF16)],
        scratch_shapes=[pltpu.VMEM(w_out.shape, BF16), pltpu.VMEM(w_in.shape, BF16)],
        compiler_params=_params(2),
        name="even_out_odd_in",
    )(x2, cx, cx, cx, ga, f2, gb, conv_w, w_out, g_odd, w_in, q_gain, k_gain, cos, sin)


def _attn_kernel(q_ref, qn_ref, k_ref, v_ref, o_ref, kmax_ref, acc_ref, *, tk):
    _, grp, hd, tq = q_ref.shape
    seq = k_ref.shape[2]
    n_chunks = seq // tk
    v_rows = v_ref.shape[2]

    @pl.when(pl.program_id(2) == 0)
    def _key_norm_bound():
        def body(j, best):
            kc = k_ref[0, 0, pl.ds(pl.multiple_of(j * tk, tk), tk), :].astype(F32)
            return jnp.maximum(best, jnp.sum(kc * kc, axis=1, keepdims=True))
        best = lax.fori_loop(0, n_chunks, body, jnp.zeros((tk, 1), F32))
        kmax_ref[0] = jnp.sqrt(jnp.max(best))

    shift = [qn_ref[0, g] * (kmax_ref[0] * SHIFT_MARGIN) for g in range(grp)]
    span = functools.reduce(jnp.maximum, [jnp.max(c) for c in shift])

    def chunk(j):
        start = pl.multiple_of(j * tk, tk)
        return k_ref[0, 0, pl.ds(start, tk), :], v_ref[0, 0, :, pl.ds(start, tk)]

    bounded = span <= SCORE_SPAN_LIMIT

    @pl.when(bounded)
    def _bounded_shift():
        q_all = jnp.concatenate([q_ref[0, g] for g in range(grp)], axis=1)
        shift_all = jnp.concatenate(shift, axis=1)

        def body(j, acc):
            kc, vc = chunk(j)
            s = jnp.dot(kc, q_all, preferred_element_type=F32)
            p = jnp.exp2(s - shift_all).astype(BF16)
            return acc + jnp.dot(vc, p, preferred_element_type=F32)
        acc = lax.fori_loop(0, n_chunks, body, jnp.zeros((v_rows, grp * tq), F32), unroll=ATTN_UNROLL)
        for g in range(grp):
            acc_ref[g] = acc[:, g * tq:(g + 1) * tq]

    @pl.when(jnp.logical_not(bounded))
    def _online_max():
        for g in range(grp):
            def body(j, carry):
                m, acc = carry
                kc, vc = chunk(j)
                s = jnp.dot(kc, q_ref[0, g], preferred_element_type=F32)
                m_new = jnp.maximum(m, jnp.max(s, axis=0, keepdims=True))
                p = jnp.exp2(s - m_new).astype(BF16)
                acc = jnp.exp2(m - m_new) * acc + jnp.dot(vc, p, preferred_element_type=F32)
                return m_new, acc
            init = (jnp.full((1, tq), -jnp.inf, F32), jnp.zeros((v_rows, tq), F32))
            _, acc = lax.fori_loop(0, n_chunks, body, init)
            acc_ref[g] = acc

    for g in range(grp):
        acc = acc_ref[g]
        o_t = acc[:hd] / acc[hd:hd + 1]
        o_ref[0, :, g * hd:(g + 1) * hd] = o_t.T.astype(o_ref.dtype)


def _attention(qt, qn, k, vt, tq, tk):
    bsz, nh, hd, seq = qt.shape
    nkv = k.shape[1]
    grp = nh // nkv
    v_rows = vt.shape[2]
    return pl.pallas_call(
        functools.partial(_attn_kernel, tk=tk),
        grid=(bsz, nkv, seq // tq),
        in_specs=[pl.BlockSpec((1, grp, hd, tq), lambda b, h, i: (b, h, 0, i)),
                  pl.BlockSpec((1, grp, 1, tq), lambda b, h, i: (b, h, 0, i)),
                  pl.BlockSpec((1, 1, seq, hd), lambda b, h, i: (b, h, 0, 0)),
                  pl.BlockSpec((1, 1, v_rows, seq), lambda b, h, i: (b, h, 0, 0))],
        out_specs=pl.BlockSpec((1, tq, grp * hd), lambda b, h, i: (b, i, h)),
        out_shape=jax.ShapeDtypeStruct((bsz, seq, nh * hd), BF16),
        scratch_shapes=[pltpu.SMEM((1,), F32), pltpu.VMEM((grp, v_rows, tq), F32)],
        compiler_params=_params(3),
        name="gqa_attention",
    )(qt, qn, k, vt)


def _odd_out_kernel(x1_ref, o_ref, sz_ref, w_ref, g_ref, out_ref, wb_ref):
    _cast_once(pl.program_id(0) == 0, w_ref, wb_ref)
    y = (o_ref[...].astype(F32) * sz_ref[...].astype(F32)).astype(BF16)
    x2 = x1_ref[...] + jnp.dot(y, wb_ref[...], preferred_element_type=F32)
    out_ref[...] = _rms_norm(x2, g_ref[...])


def _odd_out(x1, o2, sz, w, g, tm):
    n, d = x1.shape
    row = lambda i: (i, 0)
    fixed = lambda i: (0, 0)
    return pl.pallas_call(
        _odd_out_kernel,
        grid=(n // tm,),
        in_specs=[pl.BlockSpec((tm, d), row),
                  pl.BlockSpec((tm, o2.shape[1]), row),
                  pl.BlockSpec((tm, sz.shape[1]), row),
                  _resident(w.shape),
                  pl.BlockSpec((1, d), fixed)],
        out_specs=pl.BlockSpec((tm, d), row),
        out_shape=jax.ShapeDtypeStruct((n, d), F32),
        scratch_shapes=[pltpu.VMEM(w.shape, BF16)],
        compiler_params=_params(1),
        name="odd_out",
    )(x1, o2, sz, w, g)


def kernel(x, norm_even, w_in_even, conv_w, w_out_even, norm_odd, w_in_odd, q_gain, k_gain, w_out_odd, final_norm):
    bsz, seq, d = x.shape
    assert norm_even.shape[0] == 1 and norm_odd.shape[0] == 1, "one even and one odd layer"
    assert seq % FFT_FAST == 0 and seq % GRID_W == 0
    n = bsz * seq
    slow = seq // FFT_FAST
    tm = min(512, seq)
    x2 = x.reshape(n, d)

    cx, ga, bu, gb = _even_in(x2, norm_even[0][None, :], w_in_even[0], tm)
    w1, t2, wc = _dft_tables(seq)
    ut = bu.reshape(bsz, slow, FFT_FAST, B_WIDTH).transpose(0, 2, 1, 3)
    y = _dft1(w1, ut, nb=16).transpose(0, 1, 3, 2, 4)
    scale = 1.0 / math.sqrt(seq * B_GROUP_DIM)
    f = _dft2(y.reshape(bsz, 2, seq, B_WIDTH), t2, wc, cc=min(8, slow), scale=scale)
    f2 = f.transpose(0, 2, 1, 3).reshape(n, B_WIDTH)

    cos, sin = _rope_tables(seq)
    q_scale = HEAD_DIM ** -0.5 * math.log2(math.e)
    x1, qt, qn, k, vt, sz = _mid(x2, cx, ga, f2, gb, conv_w[0], w_out_even[0],
                                 norm_odd[0][None, :], w_in_odd[0],
                                 q_gain[0][None, :], k_gain[0][None, :], cos, sin, bsz, seq, tm, q_scale)

    o = _attention(qt, qn, k, vt, tq=min(256, seq), tk=min(512, seq))
    out = _odd_out(x1, o.reshape(n, N_HEADS * HEAD_DIM), sz, w_out_odd[0], final_norm[None, :], tm)
    return out.reshape(bsz, seq, d)
```

```python
import functools
import math

import numpy as np
import jax
import jax.numpy as jnp
from jax import lax
from jax.experimental import pallas as pl
from jax.experimental.pallas import tpu as pltpu

F32 = jnp.float32
BF16 = jnp.bfloat16

EPS = 1e-6
GRID_W = 64
ROPE_THETA = 10000.0
CONV_WIDTH = 3
HEAD_DIM = 128
N_HEADS = 8
N_KV_HEADS = 2
KV_GROUP = N_HEADS // N_KV_HEADS
A_WIDTH = 512
B_WIDTH = 512
B_GROUPS = 4
B_GROUP_DIM = 128
FFT_FAST = 128
DFT_ROWS_PER_STEP = 16
V_PAD_ROWS = 16
SCORE_SPAN_LIMIT = 60.0
SHIFT_MARGIN = 1.01
MID_SUBTILES = 2
ATTN_UNROLL = 16

VMEM_LIMIT_BYTES = 56 * 1024 * 1024


def _silu(z):
    return z / (1.0 + jnp.exp(-z))


def _rms_norm(x, g):
    return x * lax.rsqrt(jnp.mean(x * x, axis=-1, keepdims=True) + EPS) * g


def _resident(shape):
    zeros = (0,) * len(shape)
    return pl.BlockSpec(shape, lambda *_: zeros, pipeline_mode=pl.Buffered(1))


def _cast_once(first_step, src_ref, dst_ref):
    @pl.when(first_step)
    def _():
        dst_ref[...] = src_ref[...].astype(dst_ref.dtype)


def _params(n_grid_dims):
    return pltpu.CompilerParams(
        dimension_semantics=("arbitrary",) * n_grid_dims,
        vmem_limit_bytes=VMEM_LIMIT_BYTES)


def _even_in_kernel(x_ref, g_ref, w_ref, cx_ref, ga_ref, bu_ref, gb_ref, wb_ref):
    _cast_once(pl.program_id(0) == 0, w_ref, wb_ref)
    h = _rms_norm(x_ref[...], g_ref[...]).astype(BF16)
    p = jnp.dot(h, wb_ref[...], preferred_element_type=F32)
    a_x = p[:, 0 * A_WIDTH:1 * A_WIDTH]
    a_b = p[:, 1 * A_WIDTH:2 * A_WIDTH]
    a_c = p[:, 2 * A_WIDTH:3 * A_WIDTH]
    a_z = p[:, 3 * A_WIDTH:4 * A_WIDTH]
    b_u = p[:, 4 * A_WIDTH:4 * A_WIDTH + B_WIDTH]
    b_z = p[:, 4 * A_WIDTH + B_WIDTH:]
    cx_ref[...] = (a_c * a_x).astype(cx_ref.dtype)
    ga_ref[...] = (a_b * _silu(a_z)).astype(ga_ref.dtype)
    bu_ref[...] = b_u.astype(bu_ref.dtype)
    gb_ref[...] = _silu(b_z).astype(gb_ref.dtype)


def _even_in(x2, g, w, tm):
    n, d = x2.shape
    e_in = w.shape[1]
    out = jax.ShapeDtypeStruct((n, A_WIDTH), BF16)
    row = lambda i: (i, 0)
    fixed = lambda i: (0, 0)
    return pl.pallas_call(
        _even_in_kernel,
        grid=(n // tm,),
        in_specs=[pl.BlockSpec((tm, d), row),
                  pl.BlockSpec((1, d), fixed),
                  _resident((d, e_in))],
        out_specs=[pl.BlockSpec((tm, A_WIDTH), row)] * 4,
        out_shape=[out] * 4,
        scratch_shapes=[pltpu.VMEM((d, e_in), BF16)],
        compiler_params=_params(1),
        name="even_in",
    )(x2, g, w)


def _dft_tables(seq):
    slow = seq // FFT_FAST
    a = np.arange(slow)
    th = 2.0 * np.pi * np.outer(a, a) / slow
    w1 = np.concatenate([np.cos(th), -np.sin(th)], axis=0)
    b = np.arange(FFT_FAST)
    k = a[:, None, None] + slow * b[None, :, None]
    th2 = 2.0 * np.pi * (k * b[None, None, :]) / seq
    cs, sn = np.cos(th2), np.sin(th2)
    t2 = np.concatenate([np.concatenate([cs, sn], axis=2),
                         np.concatenate([-sn, cs], axis=2)], axis=1)
    ch = np.arange(B_GROUP_DIM)
    thc = 2.0 * np.pi * np.outer(ch, ch) / B_GROUP_DIM
    wc = np.concatenate([np.cos(thc), np.sin(thc)], axis=0)
    return tuple(jnp.asarray(t, dtype=F32) for t in (w1, t2, wc))


def _dft1_kernel(w_ref, u_ref, y_ref):
    _, slow, nb, width = u_ref.shape
    ut = jnp.transpose(u_ref[0], (1, 0, 2))
    x = jnp.concatenate([ut[b] for b in range(nb)], axis=1)
    y = jnp.dot(w_ref[...].astype(BF16), x, preferred_element_type=F32)
    yb = y.astype(y_ref.dtype)
    for part in range(2):
        rows = slice(part * slow, (part + 1) * slow)
        by_b = jnp.stack([yb[rows, b * width:(b + 1) * width] for b in range(nb)], axis=0)
        y_ref[0, part] = jnp.transpose(by_b, (1, 0, 2))


def _dft1(w1, u4, nb):
    bsz, slow, fast, width = u4.shape
    return pl.pallas_call(
        _dft1_kernel,
        grid=(bsz, fast // nb),
        in_specs=[pl.BlockSpec((2 * slow, slow), lambda b, j: (0, 0)),
                  pl.BlockSpec((1, slow, nb, width), lambda b, j: (b, 0, j, 0))],
        out_specs=pl.BlockSpec((1, 2, slow, nb, width), lambda b, j: (b, 0, 0, j, 0)),
        out_shape=jax.ShapeDtypeStruct((bsz, 2, slow, fast, width), BF16),
        compiler_params=_params(2),
        name="dft_stage1",
    )(w1, u4)


def _dft2_kernel(y_ref, t_ref, wc_ref, f_ref, *, cc, scale):
    gd = B_GROUP_DIM
    by_c = []
    for ci in range(cc):
        rows = slice(ci * FFT_FAST, (ci + 1) * FFT_FAST)
        ys = jnp.concatenate([y_ref[0, 0, rows, :], y_ref[0, 1, rows, :]], axis=0)
        p = jnp.dot(t_ref[ci].astype(BF16), ys, preferred_element_type=F32).astype(BF16)
        lhs = jnp.concatenate(
            [jnp.concatenate([p[:FFT_FAST, g * gd:(g + 1) * gd], p[FFT_FAST:, g * gd:(g + 1) * gd]], axis=1)
             for g in range(B_GROUPS)], axis=0)
        f = jnp.dot(lhs, wc_ref[...].astype(BF16), preferred_element_type=F32) * scale
        by_c.append(jnp.concatenate([f[g * FFT_FAST:(g + 1) * FFT_FAST] for g in range(B_GROUPS)],
                                    axis=1).astype(f_ref.dtype))
    f_ref[0] = jnp.transpose(jnp.stack(by_c, axis=0), (1, 0, 2))


def _dft2(y4, t2, wc, cc, scale):
    bsz, _, seq, width = y4.shape
    slow = seq // FFT_FAST
    return pl.pallas_call(
        functools.partial(_dft2_kernel, cc=cc, scale=scale),
        grid=(slow // cc, bsz),
        in_specs=[pl.BlockSpec((1, 2, cc * FFT_FAST, width), lambda j, b: (b, 0, j, 0)),
                  pl.BlockSpec((cc, 2 * FFT_FAST, 2 * FFT_FAST), lambda j, b: (j, 0, 0)),
                  pl.BlockSpec(wc.shape, lambda j, b: (0, 0))],
        out_specs=pl.BlockSpec((1, FFT_FAST, cc, width), lambda j, b: (b, 0, j, 0)),
        out_shape=jax.ShapeDtypeStruct((bsz, FFT_FAST, slow, width), BF16),
        compiler_params=_params(2),
        name="dft_stage2",
    )(y4, t2, wc)


def _rope_tables(seq):
    t = np.arange(seq)
    row = (t // GRID_W).astype(np.float64)
    col = (t % GRID_W).astype(np.float64)
    n_pair = HEAD_DIM // 4
    inv = ROPE_THETA ** (-np.arange(n_pair, dtype=np.float64) / n_pair)
    ang = np.concatenate([row[:, None] * inv, col[:, None] * inv], axis=-1)
    cos = np.repeat(np.cos(ang), 2, axis=1)
    sin = np.repeat(np.sin(ang), 2, axis=1)
    sign = np.tile(np.array([-1.0, 1.0]), HEAD_DIM // 2)
    return jnp.asarray(cos, dtype=F32), jnp.asarray(sin * sign, dtype=F32)


def _norm_rope(xh, gain, cos, sin_signed, even_lane):
    xn = _rms_norm(xh, gain)
    partner = jnp.where(even_lane, pltpu.roll(xn, HEAD_DIM - 1, axis=1), pltpu.roll(xn, 1, axis=1))
    return xn * cos + partner * sin_signed


def _mid_kernel(x_ref, cx_ref, cxp_ref, cxn_ref, ga_ref, f_ref, gb_ref, cw_ref, wo_ref,
                g_ref, wi_ref, qg_ref, kg_ref, cos_ref, sin_ref,
                x1_ref, q_ref, qn_ref, k_ref, v_ref, sz_ref, wob_ref, wib_ref, *, q_scale):
    first_step = (pl.program_id(0) == 0) & (pl.program_id(1) == 0)
    _cast_once(first_step, wo_ref, wob_ref)
    _cast_once(first_step, wi_ref, wib_ref)
    i = pl.program_id(1)
    n_i = pl.num_programs(1)
    tm = x_ref.shape[0]
    cx = cx_ref[...].astype(F32)
    prev_row = jnp.where(i > 0, cxp_ref[7:8, :].astype(F32), 0.0)
    next_row = jnp.where(i < n_i - 1, cxn_ref[0:1, :].astype(F32), 0.0)
    r = lax.broadcasted_iota(jnp.int32, cx.shape, 0)
    up = jnp.where(r == 0, prev_row, pltpu.roll(cx, 1, axis=0))
    dn = jnp.where(r == tm - 1, next_row, pltpu.roll(cx, tm - 1, axis=0))
    conv = up * cw_ref[0:1, :] + cx * cw_ref[1:2, :] + dn * cw_ref[2:3, :]
    y_a = ga_ref[...].astype(F32) * conv
    y_b = f_ref[...].astype(F32) * gb_ref[...].astype(F32)
    y = jnp.concatenate([y_a, y_b], axis=1).astype(BF16)
    qw = N_HEADS * HEAD_DIM
    kw = N_KV_HEADS * HEAD_DIM
    ts = tm // MID_SUBTILES
    even_lane = (lax.broadcasted_iota(jnp.int32, (ts, HEAD_DIM), 1) & 1) == 0
    ones_pad = (lax.broadcasted_iota(jnp.int32, (V_PAD_ROWS, ts), 0) == 0).astype(v_ref.dtype)
    for sub in range(MID_SUBTILES):
        rows = slice(sub * ts, (sub + 1) * ts)
        x1 = x_ref[rows, :] + jnp.dot(y[rows], wob_ref[...], preferred_element_type=F32)
        x1_ref[rows, :] = x1
        h = _rms_norm(x1, g_ref[...]).astype(BF16)
        p = jnp.dot(h, wib_ref[...], preferred_element_type=F32)
        cos = cos_ref[rows, :]
        sin = sin_ref[rows, :]
        for hh in range(N_HEADS):
            qh = _norm_rope(p[:, hh * HEAD_DIM:(hh + 1) * HEAD_DIM], qg_ref[...], cos, sin, even_lane)
            qt = (qh * q_scale).T.astype(q_ref.dtype)
            q_ref[0, hh, :, rows] = qt
            qf = qt.astype(F32)
            qn_ref[0, hh, :, rows] = jnp.sqrt(jnp.sum(qf * qf, axis=0, keepdims=True))
        for hh in range(N_KV_HEADS):
            kh = _norm_rope(p[:, qw + hh * HEAD_DIM:qw + (hh + 1) * HEAD_DIM], kg_ref[...], cos, sin, even_lane)
            k_ref[0, hh, rows, :] = kh.astype(k_ref.dtype)
            vt = p[:, qw + kw + hh * HEAD_DIM:qw + kw + (hh + 1) * HEAD_DIM].T.astype(v_ref.dtype)
            v_ref[0, hh, :, rows] = jnp.concatenate([vt, ones_pad], axis=0)
        sz_ref[rows, :] = _silu(p[:, qw + 2 * kw:]).astype(sz_ref.dtype)


def _mid(x2, cx, ga, f2, gb, conv_w, w_out, g_odd, w_in, q_gain, k_gain, cos, sin, bsz, seq, tm, q_scale):
    n, d = x2.shape
    nt = seq // tm
    halo = 8
    tpb = tm // halo
    last_blk = n // halo - 1
    row = lambda b, i: (b * nt + i, 0)
    fixed = lambda b, i: (0, 0)
    prev = lambda b, i: (jnp.maximum((b * nt + i) * tpb - 1, 0), 0)
    nxt = lambda b, i: (jnp.minimum((b * nt + i + 1) * tpb, last_blk), 0)
    pos = lambda b, i: (i, 0)
    head = lambda b, i: (b, 0, i, 0)
    head_t = lambda b, i: (b, 0, 0, i)
    half = pl.BlockSpec((tm, A_WIDTH), row)
    return pl.pallas_call(
        functools.partial(_mid_kernel, q_scale=q_scale),
        grid=(bsz, nt),
        in_specs=[pl.BlockSpec((tm, d), row),
                  half,
                  pl.BlockSpec((halo, A_WIDTH), prev),
                  pl.BlockSpec((halo, A_WIDTH), nxt),
                  half, half, half,
                  pl.BlockSpec(conv_w.shape, fixed),
                  _resident(w_out.shape),
                  pl.BlockSpec((1, d), fixed),
                  _resident(w_in.shape),
                  pl.BlockSpec((1, HEAD_DIM), fixed),
                  pl.BlockSpec((1, HEAD_DIM), fixed),
                  pl.BlockSpec((tm, HEAD_DIM), pos),
                  pl.BlockSpec((tm, HEAD_DIM), pos)],
        out_specs=[pl.BlockSpec((tm, d), row),
                   pl.BlockSpec((1, N_HEADS, HEAD_DIM, tm), head_t),
                   pl.BlockSpec((1, N_HEADS, 1, tm), head_t),
                   pl.BlockSpec((1, N_KV_HEADS, tm, HEAD_DIM), head),
                   pl.BlockSpec((1, N_KV_HEADS, HEAD_DIM + V_PAD_ROWS, tm), head_t),
                   pl.BlockSpec((tm, N_HEADS * HEAD_DIM), row)],
        out_shape=[jax.ShapeDtypeStruct((n, d), F32),
                   jax.ShapeDtypeStruct((bsz, N_HEADS, HEAD_DIM, seq), BF16),
                   jax.ShapeDtypeStruct((bsz, N_HEADS, 1, seq), F32),
                   jax.ShapeDtypeStruct((bsz, N_KV_HEADS, seq, HEAD_DIM), BF16),
                   jax.ShapeDtypeStruct((bsz, N_KV_HEADS, HEAD_DIM + V_PAD_ROWS, seq), BF16),
                   jax.ShapeDtypeStruct((n, N_HEADS * HEAD_DIM), BF16)],
        scratch_shapes=[pltpu.VMEM(w_out.shape, BF16), pltpu.VMEM(w_in.shape, BF16)],
        compiler_params=_params(2),
        name="even_out_odd_in",
    )(x2, cx, cx, cx, ga, f2, gb, conv_w, w_out, g_odd, w_in, q_gain, k_gain, cos, sin)


def _attn_kernel(q_ref, qn_ref, k_ref, v_ref, o_ref, kmax_ref, acc_ref, *, tk):
    _, grp, hd, tq = q_ref.shape
    seq = k_ref.shape[2]
    n_chunks = seq // tk
    v_rows = v_ref.shape[2]

    @pl.when(pl.program_id(2) == 0)
    def _key_norm_bound():
        def body(j, best):
            kc = k_ref[0, 0, pl.ds(pl.multiple_of(j * tk, tk), tk), :].astype(F32)
            return jnp.maximum(best, jnp.sum(kc * kc, axis=1, keepdims=True))
        best = lax.fori_loop(0, n_chunks, body, jnp.zeros((tk, 1), F32))
        kmax_ref[0] = jnp.sqrt(jnp.max(best))

    shift = [qn_ref[0, g] * (kmax_ref[0] * SHIFT_MARGIN) for g in range(grp)]
    span = functools.reduce(jnp.maximum, [jnp.max(c) for c in shift])

    def chunk(j):
        start = pl.multiple_of(j * tk, tk)
        return k_ref[0, 0, pl.ds(start, tk), :], v_ref[0, 0, :, pl.ds(start, tk)]

    bounded = span <= SCORE_SPAN_LIMIT

    @pl.when(bounded)
    def _bounded_shift():
        q_all = jnp.concatenate([q_ref[0, g] for g in range(grp)], axis=1)
        shift_all = jnp.concatenate(shift, axis=1)

        def body(j, acc):
            kc, vc = chunk(j)
            s = jnp.dot(kc, q_all, preferred_element_type=F32)
            p = jnp.exp2(s - shift_all).astype(BF16)
            return acc + jnp.dot(vc, p, preferred_element_type=F32)
        acc = lax.fori_loop(0, n_chunks, body, jnp.zeros((v_rows, grp * tq), F32), unroll=ATTN_UNROLL)
        for g in range(grp):
            acc_ref[g] = acc[:, g * tq:(g + 1) * tq]

    @pl.when(jnp.logical_not(bounded))
    def _online_max():
        for g in range(grp):
            def body(j, carry):
                m, acc = carry
                kc, vc = chunk(j)
                s = jnp.dot(kc, q_ref[0, g], preferred_element_type=F32)
                m_new = jnp.maximum(m, jnp.max(s, axis=0, keepdims=True))
                p = jnp.exp2(s - m_new).astype(BF16)
                acc = jnp.exp2(m - m_new) * acc + jnp.dot(vc, p, preferred_element_type=F32)
                return m_new, acc
            init = (jnp.full((1, tq), -jnp.inf, F32), jnp.zeros((v_rows, tq), F32))
            _, acc = lax.fori_loop(0, n_chunks, body, init)
            acc_ref[g] = acc

    for g in range(grp):
        acc = acc_ref[g]
        o_t = acc[:hd] / acc[hd:hd + 1]
        o_ref[0, :, g * hd:(g + 1) * hd] = o_t.T.astype(o_ref.dtype)


def _attention(qt, qn, k, vt, tq, tk):
    bsz, nh, hd, seq = qt.shape
    nkv = k.shape[1]
    grp = nh // nkv
    v_rows = vt.shape[2]
    return pl.pallas_call(
        functools.partial(_attn_kernel, tk=tk),
        grid=(bsz, nkv, seq // tq),
        in_specs=[pl.BlockSpec((1, grp, hd, tq), lambda b, h, i: (b, h, 0, i)),
                  pl.BlockSpec((1, grp, 1, tq), lambda b, h, i: (b, h, 0, i)),
                  pl.BlockSpec((1, 1, seq, hd), lambda b, h, i: (b, h, 0, 0)),
                  pl.BlockSpec((1, 1, v_rows, seq), lambda b, h, i: (b, h, 0, 0))],
        out_specs=pl.BlockSpec((1, tq, grp * hd), lambda b, h, i: (b, i, h)),
        out_shape=jax.ShapeDtypeStruct((bsz, seq, nh * hd), BF16),
        scratch_shapes=[pltpu.SMEM((1,), F32), pltpu.VMEM((grp, v_rows, tq), F32)],
        compiler_params=_params(3),
        name="gqa_attention",
    )(qt, qn, k, vt)


def _odd_out_kernel(x1_ref, o_ref, sz_ref, w_ref, g_ref, out_ref, wb_ref):
    _cast_once(pl.program_id(0) == 0, w_ref, wb_ref)
    y = (o_ref[...].astype(F32) * sz_ref[...].astype(F32)).astype(BF16)
    x2 = x1_ref[...] + jnp.dot(y, wb_ref[...], preferred_element_type=F32)
    out_ref[...] = _rms_norm(x2, g_ref[...])


def _odd_out(x1, o2, sz, w, g, tm):
    n, d = x1.shape
    row = lambda i: (i, 0)
    fixed = lambda i: (0, 0)
    return pl.pallas_call(
        _odd_out_kernel,
        grid=(n // tm,),
        in_specs=[pl.BlockSpec((tm, d), row),
                  pl.BlockSpec((tm, o2.shape[1]), row),
                  pl.BlockSpec((tm, sz.shape[1]), row),
                  _resident(w.shape),
                  pl.BlockSpec((1, d), fixed)],
        out_specs=pl.BlockSpec((tm, d), row),
        out_shape=jax.ShapeDtypeStruct((n, d), F32),
        scratch_shapes=[pltpu.VMEM(w.shape, BF16)],
        compiler_params=_params(1),
        name="odd_out",
    )(x1, o2, sz, w, g)


def kernel(x, norm_even, w_in_even, conv_w, w_out_even, norm_odd, w_in_odd, q_gain, k_gain, w_out_odd, final_norm):
    bsz, seq, d = x.shape
    assert norm_even.shape[0] == 1 and norm_odd.shape[0] == 1, "one even and one odd layer"
    assert seq % FFT_FAST == 0 and seq % GRID_W == 0
    n = bsz * seq
    slow = seq // FFT_FAST
    tm = min(512, seq)
    x2 = x.reshape(n, d)

    cx, ga, bu, gb = _even_in(x2, norm_even[0][None, :], w_in_even[0], tm)
    w1, t2, wc = _dft_tables(seq)
    y = _dft1(w1, bu.reshape(bsz, slow, FFT_FAST, B_WIDTH), nb=DFT_ROWS_PER_STEP)
    scale = 1.0 / math.sqrt(seq * B_GROUP_DIM)
    f = _dft2(y.reshape(bsz, 2, seq, B_WIDTH), t2, wc, cc=min(DFT_ROWS_PER_STEP, slow), scale=scale)
    f2 = f.reshape(n, B_WIDTH)

    cos, sin = _rope_tables(seq)
    q_scale = HEAD_DIM ** -0.5 * math.log2(math.e)
    x1, qt, qn, k, vt, sz = _mid(x2, cx, ga, f2, gb, conv_w[0], w_out_even[0],
                                 norm_odd[0][None, :], w_in_odd[0],
                                 q_gain[0][None, :], k_gain[0][None, :], cos, sin, bsz, seq, tm, q_scale)

    o = _attention(qt, qn, k, vt, tq=min(256, seq), tk=min(512, seq))
    out = _odd_out(x1, o.reshape(n, N_HEADS * HEAD_DIM), sz, w_out_odd[0], final_norm[None, :], tm)
    return out.reshape(bsz, seq, d)
```

```python
import functools
import math

import numpy as np
import jax
import jax.numpy as jnp
from jax import lax
from jax.experimental import pallas as pl
from jax.experimental.pallas import tpu as pltpu

F32 = jnp.float32
BF16 = jnp.bfloat16

EPS = 1e-6
GRID_W = 64
ROPE_THETA = 10000.0
CONV_WIDTH = 3
HEAD_DIM = 128
N_HEADS = 8
N_KV_HEADS = 2
KV_GROUP = N_HEADS // N_KV_HEADS
A_WIDTH = 512
B_WIDTH = 512
B_GROUPS = 4
B_GROUP_DIM = 128
FFT_FAST = 128
DFT_ROWS_PER_STEP = 16
V_PAD_ROWS = 16
SCORE_SPAN_LIMIT = 60.0
SHIFT_MARGIN = 1.01
MID_SUBTILES = 2
ATTN_UNROLL = 16

VMEM_LIMIT_BYTES = 56 * 1024 * 1024


def _silu(z):
    return z / (1.0 + jnp.exp(-z))


def _rms_norm(x, g):
    return x * lax.rsqrt(jnp.mean(x * x, axis=-1, keepdims=True) + EPS) * g


def _resident(shape):
    zeros = (0,) * len(shape)
    return pl.BlockSpec(shape, lambda *_: zeros, pipeline_mode=pl.Buffered(1))


def _cast_once(first_step, src_ref, dst_ref):
    @pl.when(first_step)
    def _():
        dst_ref[...] = src_ref[...].astype(dst_ref.dtype)


def _params(n_grid_dims):
    return pltpu.CompilerParams(
        dimension_semantics=("arbitrary",) * n_grid_dims,
        vmem_limit_bytes=VMEM_LIMIT_BYTES)


def _even_in_kernel(x_ref, g_ref, w_ref, cx_ref, ga_ref, bu_ref, gb_ref, wb_ref):
    _cast_once(pl.program_id(0) == 0, w_ref, wb_ref)
    h = _rms_norm(x_ref[...], g_ref[...]).astype(BF16)
    p = jnp.dot(h, wb_ref[...], preferred_element_type=F32)
    a_x = p[:, 0 * A_WIDTH:1 * A_WIDTH]
    a_b = p[:, 1 * A_WIDTH:2 * A_WIDTH]
    a_c = p[:, 2 * A_WIDTH:3 * A_WIDTH]
    a_z = p[:, 3 * A_WIDTH:4 * A_WIDTH]
    b_u = p[:, 4 * A_WIDTH:4 * A_WIDTH + B_WIDTH]
    b_z = p[:, 4 * A_WIDTH + B_WIDTH:]
    cx_ref[...] = (a_c * a_x).astype(cx_ref.dtype)
    ga_ref[...] = (a_b * _silu(a_z)).astype(ga_ref.dtype)
    bu_ref[...] = b_u.astype(bu_ref.dtype)
    gb_ref[...] = _silu(b_z).astype(gb_ref.dtype)


def _even_in(x2, g, w, tm):
    n, d = x2.shape
    e_in = w.shape[1]
    out = jax.ShapeDtypeStruct((n, A_WIDTH), BF16)
    row = lambda i: (i, 0)
    fixed = lambda i: (0, 0)
    return pl.pallas_call(
        _even_in_kernel,
        grid=(n // tm,),
        in_specs=[pl.BlockSpec((tm, d), row),
                  pl.BlockSpec((1, d), fixed),
                  _resident((d, e_in))],
        out_specs=[pl.BlockSpec((tm, A_WIDTH), row)] * 4,
        out_shape=[out] * 4,
        scratch_shapes=[pltpu.VMEM((d, e_in), BF16)],
        compiler_params=_params(1),
        name="even_in",
    )(x2, g, w)


def _dft_tables(seq):
    slow = seq // FFT_FAST
    a = np.arange(slow)
    th = 2.0 * np.pi * np.outer(a, a) / slow
    w1 = np.concatenate([np.cos(th), -np.sin(th)], axis=0)
    b = np.arange(FFT_FAST)
    k = a[:, None, None] + slow * b[None, :, None]
    th2 = 2.0 * np.pi * (k * b[None, None, :]) / seq
    cs, sn = np.cos(th2), np.sin(th2)
    t2 = np.concatenate([np.concatenate([cs, sn], axis=2),
                         np.concatenate([-sn, cs], axis=2)], axis=1)
    ch = np.arange(B_GROUP_DIM)
    thc = 2.0 * np.pi * np.outer(ch, ch) / B_GROUP_DIM
    wc = np.concatenate([np.cos(thc), np.sin(thc)], axis=0)
    return tuple(jnp.asarray(t, dtype=F32) for t in (w1, t2, wc))


def _dft1_kernel(w_ref, u_ref, y_ref):
    _, slow, nb, width = u_ref.shape
    ut = jnp.transpose(u_ref[0], (1, 0, 2))
    x = jnp.concatenate([ut[b] for b in range(nb)], axis=1)
    y = jnp.dot(w_ref[...].astype(BF16), x, preferred_element_type=F32)
    yb = y.astype(y_ref.dtype)
    for part in range(2):
        rows = slice(part * slow, (part + 1) * slow)
        by_b = jnp.stack([yb[rows, b * width:(b + 1) * width] for b in range(nb)], axis=0)
        y_ref[0, part] = jnp.transpose(by_b, (1, 0, 2))


def _dft1(w1, u4, nb):
    bsz, slow, fast, width = u4.shape
    return pl.pallas_call(
        _dft1_kernel,
        grid=(bsz, fast // nb),
        in_specs=[pl.BlockSpec((2 * slow, slow), lambda b, j: (0, 0)),
                  pl.BlockSpec((1, slow, nb, width), lambda b, j: (b, 0, j, 0))],
        out_specs=pl.BlockSpec((1, 2, slow, nb, width), lambda b, j: (b, 0, 0, j, 0)),
        out_shape=jax.ShapeDtypeStruct((bsz, 2, slow, fast, width), BF16),
        compiler_params=_params(2),
        name="dft_stage1",
    )(w1, u4)


def _dft2_kernel(y_ref, t_ref, wc_ref, f_ref, *, cc, scale):
    gd = B_GROUP_DIM
    by_c = []
    for ci in range(cc):
        rows = slice(ci * FFT_FAST, (ci + 1) * FFT_FAST)
        ys = jnp.concatenate([y_ref[0, 0, rows, :], y_ref[0, 1, rows, :]], axis=0)
        p = jnp.dot(t_ref[ci].astype(BF16), ys, preferred_element_type=F32).astype(BF16)
        lhs = jnp.concatenate(
            [jnp.concatenate([p[:FFT_FAST, g * gd:(g + 1) * gd], p[FFT_FAST:, g * gd:(g + 1) * gd]], axis=1)
             for g in range(B_GROUPS)], axis=0)
        f = jnp.dot(lhs, wc_ref[...].astype(BF16), preferred_element_type=F32) * scale
        by_c.append(jnp.concatenate([f[g * FFT_FAST:(g + 1) * FFT_FAST] for g in range(B_GROUPS)],
                                    axis=1).astype(f_ref.dtype))
    f_ref[0] = jnp.transpose(jnp.stack(by_c, axis=0), (1, 0, 2))


def _dft2(y4, t2, wc, cc, scale):
    bsz, _, seq, width = y4.shape
    slow = seq // FFT_FAST
    return pl.pallas_call(
        functools.partial(_dft2_kernel, cc=cc, scale=scale),
        grid=(slow // cc, bsz),
        in_specs=[pl.BlockSpec((1, 2, cc * FFT_FAST, width), lambda j, b: (b, 0, j, 0)),
                  pl.BlockSpec((cc, 2 * FFT_FAST, 2 * FFT_FAST), lambda j, b: (j, 0, 0)),
                  pl.BlockSpec(wc.shape, lambda j, b: (0, 0))],
        out_specs=pl.BlockSpec((1, FFT_FAST, cc, width), lambda j, b: (b, 0, j, 0)),
        out_shape=jax.ShapeDtypeStruct((bsz, FFT_FAST, slow, width), BF16),
        compiler_params=_params(2),
        name="dft_stage2",
    )(y4, t2, wc)


def _rope_tables(seq):
    t = np.arange(seq)
    row = (t // GRID_W).astype(np.float64)
    col = (t % GRID_W).astype(np.float64)
    n_pair = HEAD_DIM // 4
    inv = ROPE_THETA ** (-np.arange(n_pair, dtype=np.float64) / n_pair)
    ang = np.concatenate([row[:, None] * inv, col[:, None] * inv], axis=-1)
    cos = np.repeat(np.cos(ang), 2, axis=1)
    sin = np.repeat(np.sin(ang), 2, axis=1)
    sign = np.tile(np.array([-1.0, 1.0]), HEAD_DIM // 2)
    return jnp.asarray(cos, dtype=F32), jnp.asarray(sin * sign, dtype=F32)


def _norm_rope(xh, gain, cos, sin_signed, even_lane):
    xn = _rms_norm(xh, gain)
    partner = jnp.where(even_lane, pltpu.roll(xn, HEAD_DIM - 1, axis=1), pltpu.roll(xn, 1, axis=1))
    return xn * cos + partner * sin_signed


def _mid_kernel(x_ref, cx_ref, cxp_ref, cxn_ref, ga_ref, f_ref, gb_ref, cw_ref, wo_ref,
                g_ref, wi_ref, qg_ref, kg_ref, cos_ref, sin_ref,
                x1_ref, q_ref, qn_ref, k_ref, v_ref, sz_ref, wob_ref, wib_ref, *, q_scale):
    first_step = (pl.program_id(0) == 0) & (pl.program_id(1) == 0)
    _cast_once(first_step, wo_ref, wob_ref)
    _cast_once(first_step, wi_ref, wib_ref)
    i = pl.program_id(1)
    n_i = pl.num_programs(1)
    tm = x_ref.shape[0]
    cx = cx_ref[...].astype(F32)
    prev_row = jnp.where(i > 0, cxp_ref[7:8, :].astype(F32), 0.0)
    next_row = jnp.where(i < n_i - 1, cxn_ref[0:1, :].astype(F32), 0.0)
    r = lax.broadcasted_iota(jnp.int32, cx.shape, 0)
    up = jnp.where(r == 0, prev_row, pltpu.roll(cx, 1, axis=0))
    dn = jnp.where(r == tm - 1, next_row, pltpu.roll(cx, tm - 1, axis=0))
    conv = up * cw_ref[0:1, :] + cx * cw_ref[1:2, :] + dn * cw_ref[2:3, :]
    y_a = ga_ref[...].astype(F32) * conv
    y_b = f_ref[...].astype(F32) * gb_ref[...].astype(F32)
    y = jnp.concatenate([y_a, y_b], axis=1).astype(BF16)
    qw = N_HEADS * HEAD_DIM
    kw = N_KV_HEADS * HEAD_DIM
    ts = tm // MID_SUBTILES
    even_lane = (lax.broadcasted_iota(jnp.int32, (ts, HEAD_DIM), 1) & 1) == 0
    ones_pad = (lax.broadcasted_iota(jnp.int32, (V_PAD_ROWS, ts), 0) == 0).astype(v_ref.dtype)
    for sub in range(MID_SUBTILES):
        rows = slice(sub * ts, (sub + 1) * ts)
        x1 = x_ref[rows, :] + jnp.dot(y[rows], wob_ref[...], preferred_element_type=F32)
        x1_ref[rows, :] = x1
        h = _rms_norm(x1, g_ref[...]).astype(BF16)
        p = jnp.dot(h, wib_ref[...], preferred_element_type=F32)
        cos = cos_ref[rows, :]
        sin = sin_ref[rows, :]
        for hh in range(N_HEADS):
            qh = _norm_rope(p[:, hh * HEAD_DIM:(hh + 1) * HEAD_DIM], qg_ref[...], cos, sin, even_lane)
            qt = (qh * q_scale).T.astype(q_ref.dtype)
            q_ref[0, hh, :, rows] = qt
            qf = qt.astype(F32)
            qn_ref[0, hh, :, rows] = jnp.sqrt(jnp.sum(qf * qf, axis=0, keepdims=True))
        for hh in range(N_KV_HEADS):
            kh = _norm_rope(p[:, qw + hh * HEAD_DIM:qw + (hh + 1) * HEAD_DIM], kg_ref[...], cos, sin, even_lane)
            k_ref[0, hh, rows, :] = kh.astype(k_ref.dtype)
            vt = p[:, qw + kw + hh * HEAD_DIM:qw + kw + (hh + 1) * HEAD_DIM].T.astype(v_ref.dtype)
            v_ref[0, hh, :, rows] = jnp.concatenate([vt, ones_pad], axis=0)
        sz_ref[rows, :] = _silu(p[:, qw + 2 * kw:]).astype(sz_ref.dtype)


def _mid(x2, cx, ga, f2, gb, conv_w, w_out, g_odd, w_in, q_gain, k_gain, cos, sin, bsz, seq, tm, q_scale):
    n, d = x2.shape
    nt = seq // tm
    halo = 8
    tpb = tm // halo
    last_blk = n // halo - 1
    row = lambda b, i: (b * nt + i, 0)
    fixed = lambda b, i: (0, 0)
    prev = lambda b, i: (jnp.maximum((b * nt + i) * tpb - 1, 0), 0)
    nxt = lambda b, i: (jnp.minimum((b * nt + i + 1) * tpb, last_blk), 0)
    pos = lambda b, i: (i, 0)
    head = lambda b, i: (b, 0, i, 0)
    head_t = lambda b, i: (b, 0, 0, i)
    half = pl.BlockSpec((tm, A_WIDTH), row)
    return pl.pallas_call(
        functools.partial(_mid_kernel, q_scale=q_scale),
        grid=(bsz, nt),
        in_specs=[pl.BlockSpec((tm, d), row),
                  half,
                  pl.BlockSpec((halo, A_WIDTH), prev),
                  pl.BlockSpec((halo, A_WIDTH), nxt),
                  half, half, half,
                  pl.BlockSpec(conv_w.shape, fixed),
                  _resident(w_out.shape),
                  pl.BlockSpec((1, d), fixed),
                  _resident(w_in.shape),
                  pl.BlockSpec((1, HEAD_DIM), fixed),
                  pl.BlockSpec((1, HEAD_DIM), fixed),
                  pl.BlockSpec((tm, HEAD_DIM), pos),
                  pl.BlockSpec((tm, HEAD_DIM), pos)],
        out_specs=[pl.BlockSpec((tm, d), row),
                   pl.BlockSpec((1, N_HEADS, HEAD_DIM, tm), head_t),
                   pl.BlockSpec((1, N_HEADS, 1, tm), head_t),
                   pl.BlockSpec((1, N_KV_HEADS, tm, HEAD_DIM), head),
                   pl.BlockSpec((1, N_KV_HEADS, HEAD_DIM + V_PAD_ROWS, tm), head_t),
                   pl.BlockSpec((tm, N_HEADS * HEAD_DIM), row)],
        out_shape=[jax.ShapeDtypeStruct((n, d), F32),
                   jax.ShapeDtypeStruct((bsz, N_HEADS, HEAD_DIM, seq), BF16),
                   jax.ShapeDtypeStruct((bsz, N_HEADS, 1, seq), F32),
                   jax.ShapeDtypeStruct((bsz, N_KV_HEADS, seq, HEAD_DIM), BF16),
                   jax.ShapeDtypeStruct((bsz, N_KV_HEADS, HEAD_DIM + V_PAD_ROWS, seq), BF16),
                   jax.ShapeDtypeStruct((n, N_HEADS * HEAD_DIM), BF16)],
        scratch_shapes=[pltpu.VMEM(w_out.shape, BF16), pltpu.VMEM(w_in.shape, BF16)],
        compiler_params=_params(2),
        name="even_out_odd_in",
    )(x2, cx, cx, cx, ga, f2, gb, conv_w, w_out, g_odd, w_in, q_gain, k_gain, cos, sin)


def _attn_kernel(q_ref, qn_ref, k_ref, v_ref, sz_ref, y_ref, kmax_ref, acc_ref, *, tk):
    _, grp, hd, tq = q_ref.shape
    seq = k_ref.shape[2]
    n_chunks = seq // tk
    v_rows = v_ref.shape[2]

    @pl.when(pl.program_id(2) == 0)
    def _key_norm_bound():
        def body(j, best):
            kc = k_ref[0, 0, pl.ds(pl.multiple_of(j * tk, tk), tk), :].astype(F32)
            return jnp.maximum(best, jnp.sum(kc * kc, axis=1, keepdims=True))
        best = lax.fori_loop(0, n_chunks, body, jnp.zeros((tk, 1), F32))
        kmax_ref[0] = jnp.sqrt(jnp.max(best))

    shift = [qn_ref[0, g] * (kmax_ref[0] * SHIFT_MARGIN) for g in range(grp)]
    span = functools.reduce(jnp.maximum, [jnp.max(c) for c in shift])

    def chunk(j):
        start = pl.multiple_of(j * tk, tk)
        return k_ref[0, 0, pl.ds(start, tk), :], v_ref[0, 0, :, pl.ds(start, tk)]

    bounded = span <= SCORE_SPAN_LIMIT

    @pl.when(bounded)
    def _bounded_shift():
        q_all = jnp.concatenate([q_ref[0, g] for g in range(grp)], axis=1)
        shift_all = jnp.concatenate(shift, axis=1)

        def body(j, acc):
            kc, vc = chunk(j)
            s = jnp.dot(kc, q_all, preferred_element_type=F32)
            p = jnp.exp2(s - shift_all).astype(BF16)
            return acc + jnp.dot(vc, p, preferred_element_type=F32)
        acc = lax.fori_loop(0, n_chunks, body, jnp.zeros((v_rows, grp * tq), F32), unroll=ATTN_UNROLL)
        for g in range(grp):
            acc_ref[g] = acc[:, g * tq:(g + 1) * tq]

    @pl.when(jnp.logical_not(bounded))
    def _online_max():
        for g in range(grp):
            def body(j, carry):
                m, acc = carry
                kc, vc = chunk(j)
                s = jnp.dot(kc, q_ref[0, g], preferred_element_type=F32)
                m_new = jnp.maximum(m, jnp.max(s, axis=0, keepdims=True))
                p = jnp.exp2(s - m_new).astype(BF16)
                acc = jnp.exp2(m - m_new) * acc + jnp.dot(vc, p, preferred_element_type=F32)
                return m_new, acc
            init = (jnp.full((1, tq), -jnp.inf, F32), jnp.zeros((v_rows, tq), F32))
            _, acc = lax.fori_loop(0, n_chunks, body, init)
            acc_ref[g] = acc

    for g in range(grp):
        acc = acc_ref[g]
        o_t = acc[:hd] / acc[hd:hd + 1]
        gate = sz_ref[:, g * hd:(g + 1) * hd].astype(F32)
        y_ref[0, :, g * hd:(g + 1) * hd] = (o_t.T * gate).astype(y_ref.dtype)


def _attention(qt, qn, k, vt, sz, tq, tk):
    bsz, nh, hd, seq = qt.shape
    nt = seq // tq
    nkv = k.shape[1]
    grp = nh // nkv
    v_rows = vt.shape[2]
    return pl.pallas_call(
        functools.partial(_attn_kernel, tk=tk),
        grid=(bsz, nkv, seq // tq),
        in_specs=[pl.BlockSpec((1, grp, hd, tq), lambda b, h, i: (b, h, 0, i)),
                  pl.BlockSpec((1, grp, 1, tq), lambda b, h, i: (b, h, 0, i)),
                  pl.BlockSpec((1, 1, seq, hd), lambda b, h, i: (b, h, 0, 0)),
                  pl.BlockSpec((1, 1, v_rows, seq), lambda b, h, i: (b, h, 0, 0)),
                  pl.BlockSpec((tq, grp * hd), lambda b, h, i: (b * nt + i, h))],
        out_specs=pl.BlockSpec((1, tq, grp * hd), lambda b, h, i: (b, i, h)),
        out_shape=jax.ShapeDtypeStruct((bsz, seq, nh * hd), BF16),
        scratch_shapes=[pltpu.SMEM((1,), F32), pltpu.VMEM((grp, v_rows, tq), F32)],
        compiler_params=_params(3),
        name="gqa_attention",
    )(qt, qn, k, vt, sz)


def _odd_out_kernel(x1_ref, y_ref, w_ref, g_ref, out_ref, wb_ref):
    _cast_once(pl.program_id(0) == 0, w_ref, wb_ref)
    x2 = x1_ref[...] + jnp.dot(y_ref[...], wb_ref[...], preferred_element_type=F32)
    out_ref[...] = _rms_norm(x2, g_ref[...])


def _odd_out(x1, y2, w, g, tm):
    n, d = x1.shape
    row = lambda i: (i, 0)
    fixed = lambda i: (0, 0)
    return pl.pallas_call(
        _odd_out_kernel,
        grid=(n // tm,),
        in_specs=[pl.BlockSpec((tm, d), row),
                  pl.BlockSpec((tm, y2.shape[1]), row),
                  _resident(w.shape),
                  pl.BlockSpec((1, d), fixed)],
        out_specs=pl.BlockSpec((tm, d), row),
        out_shape=jax.ShapeDtypeStruct((n, d), F32),
        scratch_shapes=[pltpu.VMEM(w.shape, BF16)],
        compiler_params=_params(1),
        name="odd_out",
    )(x1, y2, w, g)


def kernel(x, norm_even, w_in_even, conv_w, w_out_even, norm_odd, w_in_odd, q_gain, k_gain, w_out_odd, final_norm):
    bsz, seq, d = x.shape
    assert norm_even.shape[0] == 1 and norm_odd.shape[0] == 1, "one even and one odd layer"
    assert seq % FFT_FAST == 0 and seq % GRID_W == 0
    n = bsz * seq
    slow = seq // FFT_FAST
    tm = min(512, seq)
    x2 = x.reshape(n, d)

    cx, ga, bu, gb = _even_in(x2, norm_even[0][None, :], w_in_even[0], min(1024, seq))
    w1, t2, wc = _dft_tables(seq)
    y = _dft1(w1, bu.reshape(bsz, slow, FFT_FAST, B_WIDTH), nb=DFT_ROWS_PER_STEP)
    scale = 1.0 / math.sqrt(seq * B_GROUP_DIM)
    f = _dft2(y.reshape(bsz, 2, seq, B_WIDTH), t2, wc, cc=min(DFT_ROWS_PER_STEP, slow), scale=scale)
    f2 = f.reshape(n, B_WIDTH)

    cos, sin = _rope_tables(seq)
    q_scale = HEAD_DIM ** -0.5 * math.log2(math.e)
    x1, qt, qn, k, vt, sz = _mid(x2, cx, ga, f2, gb, conv_w[0], w_out_even[0],
                                 norm_odd[0][None, :], w_in_odd[0],
                                 q_gain[0][None, :], k_gain[0][None, :], cos, sin, bsz, seq, tm, q_scale)

    yg = _attention(qt, qn, k, vt, sz, tq=min(256, seq), tk=min(512, seq))
    out = _odd_out(x1, yg.reshape(n, N_HEADS * HEAD_DIM), w_out_odd[0], final_norm[None, :], tm)
    return out.reshape(bsz, seq, d)
```

```python
import functools
import math

import numpy as np
import jax
import jax.numpy as jnp
from jax import lax
from jax.experimental import pallas as pl
from jax.experimental.pallas import tpu as pltpu

F32 = jnp.float32
BF16 = jnp.bfloat16

EPS = 1e-6
GRID_W = 64
ROPE_THETA = 10000.0
CONV_WIDTH = 3
HEAD_DIM = 128
N_HEADS = 8
N_KV_HEADS = 2
KV_GROUP = N_HEADS // N_KV_HEADS
A_WIDTH = 512
B_WIDTH = 512
B_GROUPS = 4
B_GROUP_DIM = 128
FFT_FAST = 128
DFT_ROWS_PER_STEP = 16
V_PAD_ROWS = 16
SCORE_SPAN_LIMIT = 60.0
SHIFT_MARGIN = 1.01
MID_SUBTILES = 2
ATTN_UNROLL = 16

VMEM_LIMIT_BYTES = 56 * 1024 * 1024


def _silu(z):
    return z / (1.0 + jnp.exp(-z))


def _rms_norm(x, g):
    return x * lax.rsqrt(jnp.mean(x * x, axis=-1, keepdims=True) + EPS) * g


def _resident(shape):
    zeros = (0,) * len(shape)
    return pl.BlockSpec(shape, lambda *_: zeros, pipeline_mode=pl.Buffered(1))


def _cast_once(first_step, src_ref, dst_ref):
    @pl.when(first_step)
    def _():
        dst_ref[...] = src_ref[...].astype(dst_ref.dtype)


def _params(n_grid_dims):
    return pltpu.CompilerParams(
        dimension_semantics=("arbitrary",) * n_grid_dims,
        vmem_limit_bytes=VMEM_LIMIT_BYTES)


def _even_in_kernel(x_ref, g_ref, w_ref, cx_ref, ga_ref, bu_ref, gb_ref, wb_ref):
    _cast_once(pl.program_id(0) == 0, w_ref, wb_ref)
    h = _rms_norm(x_ref[...], g_ref[...]).astype(BF16)
    p = jnp.dot(h, wb_ref[...], preferred_element_type=F32)
    a_x = p[:, 0 * A_WIDTH:1 * A_WIDTH]
    a_b = p[:, 1 * A_WIDTH:2 * A_WIDTH]
    a_c = p[:, 2 * A_WIDTH:3 * A_WIDTH]
    a_z = p[:, 3 * A_WIDTH:4 * A_WIDTH]
    b_u = p[:, 4 * A_WIDTH:4 * A_WIDTH + B_WIDTH]
    b_z = p[:, 4 * A_WIDTH + B_WIDTH:]
    cx_ref[...] = (a_c * a_x).astype(cx_ref.dtype)
    ga_ref[...] = (a_b * _silu(a_z)).astype(ga_ref.dtype)
    bu_ref[...] = b_u.astype(bu_ref.dtype)
    gb_ref[...] = _silu(b_z).astype(gb_ref.dtype)


def _even_in(x2, g, w, tm):
    n, d = x2.shape
    e_in = w.shape[1]
    out = jax.ShapeDtypeStruct((n, A_WIDTH), BF16)
    row = lambda i: (i, 0)
    fixed = lambda i: (0, 0)
    return pl.pallas_call(
        _even_in_kernel,
        grid=(n // tm,),
        in_specs=[pl.BlockSpec((tm, d), row),
                  pl.BlockSpec((1, d), fixed),
                  _resident((d, e_in))],
        out_specs=[pl.BlockSpec((tm, A_WIDTH), row)] * 4,
        out_shape=[out] * 4,
        scratch_shapes=[pltpu.VMEM((d, e_in), BF16)],
        compiler_params=_params(1),
        name="even_in",
    )(x2, g, w)


def _dft_tables(seq):
    slow = seq // FFT_FAST
    a = np.arange(slow)
    th = 2.0 * np.pi * np.outer(a, a) / slow
    w1 = np.concatenate([np.cos(th), -np.sin(th)], axis=0)
    b = np.arange(FFT_FAST)
    k = a[:, None, None] + slow * b[None, :, None]
    th2 = 2.0 * np.pi * (k * b[None, None, :]) / seq
    cs, sn = np.cos(th2), np.sin(th2)
    t2 = np.concatenate([np.concatenate([cs, sn], axis=2),
                         np.concatenate([-sn, cs], axis=2)], axis=1)
    ch = np.arange(B_GROUP_DIM)
    thc = 2.0 * np.pi * np.outer(ch, ch) / B_GROUP_DIM
    wc = np.concatenate([np.cos(thc), np.sin(thc)], axis=0)
    return tuple(jnp.asarray(t, dtype=F32) for t in (w1, t2, wc))


def _dft1_kernel(w_ref, u_ref, y_ref):
    _, slow, nb, width = u_ref.shape
    ut = jnp.transpose(u_ref[0], (1, 0, 2))
    x = jnp.concatenate([ut[b] for b in range(nb)], axis=1)
    y = jnp.dot(w_ref[...].astype(BF16), x, preferred_element_type=F32)
    yb = y.astype(y_ref.dtype)
    for part in range(2):
        rows = slice(part * slow, (part + 1) * slow)
        by_b = jnp.stack([yb[rows, b * width:(b + 1) * width] for b in range(nb)], axis=0)
        y_ref[0, part] = jnp.transpose(by_b, (1, 0, 2))


def _dft1(w1, u4, nb):
    bsz, slow, fast, width = u4.shape
    return pl.pallas_call(
        _dft1_kernel,
        grid=(bsz, fast // nb),
        in_specs=[pl.BlockSpec((2 * slow, slow), lambda b, j: (0, 0)),
                  pl.BlockSpec((1, slow, nb, width), lambda b, j: (b, 0, j, 0))],
        out_specs=pl.BlockSpec((1, 2, slow, nb, width), lambda b, j: (b, 0, 0, j, 0)),
        out_shape=jax.ShapeDtypeStruct((bsz, 2, slow, fast, width), BF16),
        compiler_params=_params(2),
        name="dft_stage1",
    )(w1, u4)


def _dft2_kernel(y_ref, t_ref, wc_ref, f_ref, *, cc, scale):
    gd = B_GROUP_DIM
    by_c = []
    for ci in range(cc):
        rows = slice(ci * FFT_FAST, (ci + 1) * FFT_FAST)
        ys = jnp.concatenate([y_ref[0, 0, rows, :], y_ref[0, 1, rows, :]], axis=0)
        p = jnp.dot(t_ref[ci].astype(BF16), ys, preferred_element_type=F32).astype(BF16)
        lhs = jnp.concatenate(
            [jnp.concatenate([p[:FFT_FAST, g * gd:(g + 1) * gd], p[FFT_FAST:, g * gd:(g + 1) * gd]], axis=1)
             for g in range(B_GROUPS)], axis=0)
        f = jnp.dot(lhs, wc_ref[...].astype(BF16), preferred_element_type=F32) * scale
        by_c.append(jnp.concatenate([f[g * FFT_FAST:(g + 1) * FFT_FAST] for g in range(B_GROUPS)],
                                    axis=1).astype(f_ref.dtype))
    f_ref[0] = jnp.transpose(jnp.stack(by_c, axis=0), (1, 0, 2))


def _dft2(y4, t2, wc, cc, scale):
    bsz, _, seq, width = y4.shape
    slow = seq // FFT_FAST
    return pl.pallas_call(
        functools.partial(_dft2_kernel, cc=cc, scale=scale),
        grid=(slow // cc, bsz),
        in_specs=[pl.BlockSpec((1, 2, cc * FFT_FAST, width), lambda j, b: (b, 0, j, 0)),
                  pl.BlockSpec((cc, 2 * FFT_FAST, 2 * FFT_FAST), lambda j, b: (j, 0, 0)),
                  pl.BlockSpec(wc.shape, lambda j, b: (0, 0))],
        out_specs=pl.BlockSpec((1, FFT_FAST, cc, width), lambda j, b: (b, 0, j, 0)),
        out_shape=jax.ShapeDtypeStruct((bsz, FFT_FAST, slow, width), BF16),
        compiler_params=_params(2),
        name="dft_stage2",
    )(y4, t2, wc)


def _rope_tables(seq):
    t = np.arange(seq)
    row = (t // GRID_W).astype(np.float64)
    col = (t % GRID_W).astype(np.float64)
    n_pair = HEAD_DIM // 4
    inv = ROPE_THETA ** (-np.arange(n_pair, dtype=np.float64) / n_pair)
    ang = np.concatenate([row[:, None] * inv, col[:, None] * inv], axis=-1)
    cos = np.repeat(np.cos(ang), 2, axis=1)
    sin = np.repeat(np.sin(ang), 2, axis=1)
    sign = np.tile(np.array([-1.0, 1.0]), HEAD_DIM // 2)
    return jnp.asarray(cos, dtype=F32), jnp.asarray(sin * sign, dtype=F32)


def _norm_rope(xh, gain, cos, sin_signed, even_lane):
    xn = _rms_norm(xh, gain)
    partner = jnp.where(even_lane, pltpu.roll(xn, HEAD_DIM - 1, axis=1), pltpu.roll(xn, 1, axis=1))
    return xn * cos + partner * sin_signed


def _mid_kernel(x_ref, cx_ref, cxp_ref, cxn_ref, ga_ref, f_ref, gb_ref, cw_ref, wo_ref,
                g_ref, wi_ref, qg_ref, kg_ref, cos_ref, sin_ref,
                x1_ref, q_ref, qn_ref, k_ref, v_ref, sz_ref, wob_ref, wib_ref, *, q_scale):
    first_step = (pl.program_id(0) == 0) & (pl.program_id(1) == 0)
    _cast_once(first_step, wo_ref, wob_ref)
    _cast_once(first_step, wi_ref, wib_ref)
    i = pl.program_id(1)
    n_i = pl.num_programs(1)
    tm = x_ref.shape[0]
    cx = cx_ref[...].astype(F32)
    prev_row = jnp.where(i > 0, cxp_ref[7:8, :].astype(F32), 0.0)
    next_row = jnp.where(i < n_i - 1, cxn_ref[0:1, :].astype(F32), 0.0)
    r = lax.broadcasted_iota(jnp.int32, cx.shape, 0)
    up = jnp.where(r == 0, prev_row, pltpu.roll(cx, 1, axis=0))
    dn = jnp.where(r == tm - 1, next_row, pltpu.roll(cx, tm - 1, axis=0))
    conv = up * cw_ref[0:1, :] + cx * cw_ref[1:2, :] + dn * cw_ref[2:3, :]
    y_a = ga_ref[...].astype(F32) * conv
    y_b = f_ref[...].astype(F32) * gb_ref[...].astype(F32)
    y = jnp.concatenate([y_a, y_b], axis=1).astype(BF16)
    qw = N_HEADS * HEAD_DIM
    kw = N_KV_HEADS * HEAD_DIM
    ts = tm // MID_SUBTILES
    even_lane = (lax.broadcasted_iota(jnp.int32, (ts, HEAD_DIM), 1) & 1) == 0
    ones_pad = (lax.broadcasted_iota(jnp.int32, (V_PAD_ROWS, ts), 0) == 0).astype(v_ref.dtype)
    for sub in range(MID_SUBTILES):
        rows = slice(sub * ts, (sub + 1) * ts)
        x1 = x_ref[rows, :] + jnp.dot(y[rows], wob_ref[...], preferred_element_type=F32)
        x1_ref[rows, :] = x1
        h = _rms_norm(x1, g_ref[...]).astype(BF16)
        p = jnp.dot(h, wib_ref[...], preferred_element_type=F32)
        cos = cos_ref[rows, :]
        sin = sin_ref[rows, :]
        for hh in range(N_HEADS):
            qh = _norm_rope(p[:, hh * HEAD_DIM:(hh + 1) * HEAD_DIM], qg_ref[...], cos, sin, even_lane)
            qt = (qh * q_scale).T.astype(q_ref.dtype)
            q_ref[0, hh, :, rows] = qt
            qf = qt.astype(F32)
            qn_ref[0, hh, :, rows] = jnp.sqrt(jnp.sum(qf * qf, axis=0, keepdims=True))
        for hh in range(N_KV_HEADS):
            kh = _norm_rope(p[:, qw + hh * HEAD_DIM:qw + (hh + 1) * HEAD_DIM], kg_ref[...], cos, sin, even_lane)
            k_ref[0, hh, rows, :] = kh.astype(k_ref.dtype)
            vt = p[:, qw + kw + hh * HEAD_DIM:qw + kw + (hh + 1) * HEAD_DIM].T.astype(v_ref.dtype)
            v_ref[0, hh, :, rows] = jnp.concatenate([vt, ones_pad], axis=0)
        sz_ref[rows, :] = _silu(p[:, qw + 2 * kw:]).astype(sz_ref.dtype)


def _mid(x2, cx, ga, f2, gb, conv_w, w_out, g_odd, w_in, q_gain, k_gain, cos, sin, bsz, seq, tm, q_scale):
    n, d = x2.shape
    nt = seq // tm
    halo = 8
    tpb = tm // halo
    last_blk = n // halo - 1
    row = lambda b, i: (b * nt + i, 0)
    fixed = lambda b, i: (0, 0)
    prev = lambda b, i: (jnp.maximum((b * nt + i) * tpb - 1, 0), 0)
    nxt = lambda b, i: (jnp.minimum((b * nt + i + 1) * tpb, last_blk), 0)
    pos = lambda b, i: (i, 0)
    head = lambda b, i: (b, 0, i, 0)
    head_t = lambda b, i: (b, 0, 0, i)
    half = pl.BlockSpec((tm, A_WIDTH), row)
    return pl.pallas_call(
        functools.partial(_mid_kernel, q_scale=q_scale),
        grid=(bsz, nt),
        in_specs=[pl.BlockSpec((tm, d), row),
                  half,
                  pl.BlockSpec((halo, A_WIDTH), prev),
                  pl.BlockSpec((halo, A_WIDTH), nxt),
                  half, half, half,
                  pl.BlockSpec(conv_w.shape, fixed),
                  _resident(w_out.shape),
                  pl.BlockSpec((1, d), fixed),
                  _resident(w_in.shape),
                  pl.BlockSpec((1, HEAD_DIM), fixed),
                  pl.BlockSpec((1, HEAD_DIM), fixed),
                  pl.BlockSpec((tm, HEAD_DIM), pos),
                  pl.BlockSpec((tm, HEAD_DIM), pos)],
        out_specs=[pl.BlockSpec((tm, d), row),
                   pl.BlockSpec((1, N_HEADS, HEAD_DIM, tm), head_t),
                   pl.BlockSpec((1, N_HEADS, 1, tm), head_t),
                   pl.BlockSpec((1, N_KV_HEADS, tm, HEAD_DIM), head),
                   pl.BlockSpec((1, N_KV_HEADS, HEAD_DIM + V_PAD_ROWS, tm), head_t),
                   pl.BlockSpec((tm, N_HEADS * HEAD_DIM), row)],
        out_shape=[jax.ShapeDtypeStruct((n, d), F32),
                   jax.ShapeDtypeStruct((bsz, N_HEADS, HEAD_DIM, seq), BF16),
                   jax.ShapeDtypeStruct((bsz, N_HEADS, 1, seq), F32),
                   jax.ShapeDtypeStruct((bsz, N_KV_HEADS, seq, HEAD_DIM), BF16),
                   jax.ShapeDtypeStruct((bsz, N_KV_HEADS, HEAD_DIM + V_PAD_ROWS, seq), BF16),
                   jax.ShapeDtypeStruct((n, N_HEADS * HEAD_DIM), BF16)],
        scratch_shapes=[pltpu.VMEM(w_out.shape, BF16), pltpu.VMEM(w_in.shape, BF16)],
        compiler_params=_params(2),
        name="even_out_odd_in",
    )(x2, cx, cx, cx, ga, f2, gb, conv_w, w_out, g_odd, w_in, q_gain, k_gain, cos, sin)


def _attn_kernel(q_ref, qn_ref, k_ref, v_ref, o_ref, kmax_ref, acc_ref, *, tk):
    _, grp, hd, tq = q_ref.shape
    seq = k_ref.shape[2]
    n_chunks = seq // tk
    v_rows = v_ref.shape[2]

    @pl.when(pl.program_id(2) == 0)
    def _key_norm_bound():
        def body(j, best):
            kc = k_ref[0, 0, pl.ds(pl.multiple_of(j * tk, tk), tk), :].astype(F32)
            return jnp.maximum(best, jnp.sum(kc * kc, axis=1, keepdims=True))
        best = lax.fori_loop(0, n_chunks, body, jnp.zeros((tk, 1), F32))
        kmax_ref[0] = jnp.sqrt(jnp.max(best))

    shift = [qn_ref[0, g] * (kmax_ref[0] * SHIFT_MARGIN) for g in range(grp)]
    span = functools.reduce(jnp.maximum, [jnp.max(c) for c in shift])

    def chunk(j):
        start = pl.multiple_of(j * tk, tk)
        return k_ref[0, 0, pl.ds(start, tk), :], v_ref[0, 0, :, pl.ds(start, tk)]

    bounded = span <= SCORE_SPAN_LIMIT

    @pl.when(bounded)
    def _bounded_shift():
        q_all = jnp.concatenate([q_ref[0, g] for g in range(grp)], axis=1)
        shift_all = jnp.concatenate(shift, axis=1)

        def body(j, acc):
            kc, vc = chunk(j)
            s = jnp.dot(kc, q_all, preferred_element_type=F32)
            p = jnp.exp2(s - shift_all).astype(BF16)
            return acc + jnp.dot(vc, p, preferred_element_type=F32)
        acc = lax.fori_loop(0, n_chunks, body, jnp.zeros((v_rows, grp * tq), F32), unroll=ATTN_UNROLL)
        for g in range(grp):
            acc_ref[g] = acc[:, g * tq:(g + 1) * tq]

    @pl.when(jnp.logical_not(bounded))
    def _online_max():
        for g in range(grp):
            def body(j, carry):
                m, acc = carry
                kc, vc = chunk(j)
                s = jnp.dot(kc, q_ref[0, g], preferred_element_type=F32)
                m_new = jnp.maximum(m, jnp.max(s, axis=0, keepdims=True))
                p = jnp.exp2(s - m_new).astype(BF16)
                acc = jnp.exp2(m - m_new) * acc + jnp.dot(vc, p, preferred_element_type=F32)
                return m_new, acc
            init = (jnp.full((1, tq), -jnp.inf, F32), jnp.zeros((v_rows, tq), F32))
            _, acc = lax.fori_loop(0, n_chunks, body, init)
            acc_ref[g] = acc

    for g in range(grp):
        acc = acc_ref[g]
        o_t = acc[:hd] / acc[hd:hd + 1]
        o_ref[0, :, g * hd:(g + 1) * hd] = o_t.T.astype(o_ref.dtype)


def _attention(qt, qn, k, vt, tq, tk):
    bsz, nh, hd, seq = qt.shape
    nkv = k.shape[1]
    grp = nh // nkv
    v_rows = vt.shape[2]
    return pl.pallas_call(
        functools.partial(_attn_kernel, tk=tk),
        grid=(bsz, nkv, seq // tq),
        in_specs=[pl.BlockSpec((1, grp, hd, tq), lambda b, h, i: (b, h, 0, i)),
                  pl.BlockSpec((1, grp, 1, tq), lambda b, h, i: (b, h, 0, i)),
                  pl.BlockSpec((1, 1, seq, hd), lambda b, h, i: (b, h, 0, 0)),
                  pl.BlockSpec((1, 1, v_rows, seq), lambda b, h, i: (b, h, 0, 0))],
        out_specs=pl.BlockSpec((1, tq, grp * hd), lambda b, h, i: (b, i, h)),
        out_shape=jax.ShapeDtypeStruct((bsz, seq, nh * hd), BF16),
        scratch_shapes=[pltpu.SMEM((1,), F32), pltpu.VMEM((grp, v_rows, tq), F32)],
        compiler_params=_params(3),
        name="gqa_attention",
    )(qt, qn, k, vt)


def _odd_out_kernel(x1_ref, o_ref, sz_ref, w_ref, g_ref, out_ref, wb_ref):
    _cast_once(pl.program_id(0) == 0, w_ref, wb_ref)
    y = (o_ref[...].astype(F32) * sz_ref[...].astype(F32)).astype(BF16)
    x2 = x1_ref[...] + jnp.dot(y, wb_ref[...], preferred_element_type=F32)
    out_ref[...] = _rms_norm(x2, g_ref[...])


def _odd_out(x1, o2, sz, w, g, tm):
    n, d = x1.shape
    row = lambda i: (i, 0)
    fixed = lambda i: (0, 0)
    return pl.pallas_call(
        _odd_out_kernel,
        grid=(n // tm,),
        in_specs=[pl.BlockSpec((tm, d), row),
                  pl.BlockSpec((tm, o2.shape[1]), row),
                  pl.BlockSpec((tm, sz.shape[1]), row),
                  _resident(w.shape),
                  pl.BlockSpec((1, d), fixed)],
        out_specs=pl.BlockSpec((tm, d), row),
        out_shape=jax.ShapeDtypeStruct((n, d), F32),
        scratch_shapes=[pltpu.VMEM(w.shape, BF16)],
        compiler_params=_params(1),
        name="odd_out",
    )(x1, o2, sz, w, g)


def kernel(x, norm_even, w_in_even, conv_w, w_out_even, norm_odd, w_in_odd, q_gain, k_gain, w_out_odd, final_norm):
    bsz, seq, d = x.shape
    assert norm_even.shape[0] == 1 and norm_odd.shape[0] == 1, "one even and one odd layer"
    assert seq % FFT_FAST == 0 and seq % GRID_W == 0
    n = bsz * seq
    slow = seq // FFT_FAST
    tm = min(512, seq)
    x2 = x.reshape(n, d)

    cx, ga, bu, gb = _even_in(x2, norm_even[0][None, :], w_in_even[0], min(1024, seq))
    w1, t2, wc = _dft_tables(seq)
    y = _dft1(w1, bu.reshape(bsz, slow, FFT_FAST, B_WIDTH), nb=DFT_ROWS_PER_STEP)
    scale = 1.0 / math.sqrt(seq * B_GROUP_DIM)
    f = _dft2(y.reshape(bsz, 2, seq, B_WIDTH), t2, wc, cc=min(DFT_ROWS_PER_STEP, slow), scale=scale)
    f2 = f.reshape(n, B_WIDTH)

    cos, sin = _rope_tables(seq)
    q_scale = HEAD_DIM ** -0.5 * math.log2(math.e)
    x1, qt, qn, k, vt, sz = _mid(x2, cx, ga, f2, gb, conv_w[0], w_out_even[0],
                                 norm_odd[0][None, :], w_in_odd[0],
                                 q_gain[0][None, :], k_gain[0][None, :], cos, sin, bsz, seq, tm, q_scale)

    o = _attention(qt, qn, k, vt, tq=min(512, seq), tk=min(512, seq))
    out = _odd_out(x1, o.reshape(n, N_HEADS * HEAD_DIM), sz, w_out_odd[0], final_norm[None, :], tm)
    return out.reshape(bsz, seq, d)
```

```python
import functools
import math

import numpy as np
import jax
import jax.numpy as jnp
from jax import lax
from jax.experimental import pallas as pl
from jax.experimental.pallas import tpu as pltpu

F32 = jnp.float32
BF16 = jnp.bfloat16

EPS = 1e-6
GRID_W = 64
ROPE_THETA = 10000.0
CONV_WIDTH = 3
HEAD_DIM = 128
N_HEADS = 8
N_KV_HEADS = 2
KV_GROUP = N_HEADS // N_KV_HEADS
A_WIDTH = 512
B_WIDTH = 512
B_GROUPS = 4
B_GROUP_DIM = 128
FFT_FAST = 128
DFT_ROWS_PER_STEP = 16
V_PAD_ROWS = 16
SCORE_SPAN_LIMIT = 60.0
SHIFT_MARGIN = 1.01
MID_SUBTILES = 2
ATTN_UNROLL = 8

VMEM_LIMIT_BYTES = 56 * 1024 * 1024


def _silu(z):
    return z / (1.0 + jnp.exp(-z))


def _rms_norm(x, g):
    return x * lax.rsqrt(jnp.mean(x * x, axis=-1, keepdims=True) + EPS) * g


def _resident(shape):
    zeros = (0,) * len(shape)
    return pl.BlockSpec(shape, lambda *_: zeros, pipeline_mode=pl.Buffered(1))


def _cast_once(first_step, src_ref, dst_ref):
    @pl.when(first_step)
    def _():
        dst_ref[...] = src_ref[...].astype(dst_ref.dtype)


def _params(n_grid_dims):
    return pltpu.CompilerParams(
        dimension_semantics=("arbitrary",) * n_grid_dims,
        vmem_limit_bytes=VMEM_LIMIT_BYTES)


def _even_in_kernel(x_ref, g_ref, w_ref, cx_ref, ga_ref, bu_ref, gb_ref, wb_ref):
    _cast_once(pl.program_id(0) == 0, w_ref, wb_ref)
    h = _rms_norm(x_ref[...], g_ref[...]).astype(BF16)
    p = jnp.dot(h, wb_ref[...], preferred_element_type=F32)
    a_x = p[:, 0 * A_WIDTH:1 * A_WIDTH]
    a_b = p[:, 1 * A_WIDTH:2 * A_WIDTH]
    a_c = p[:, 2 * A_WIDTH:3 * A_WIDTH]
    a_z = p[:, 3 * A_WIDTH:4 * A_WIDTH]
    b_u = p[:, 4 * A_WIDTH:4 * A_WIDTH + B_WIDTH]
    b_z = p[:, 4 * A_WIDTH + B_WIDTH:]
    cx_ref[...] = (a_c * a_x).astype(cx_ref.dtype)
    ga_ref[...] = (a_b * _silu(a_z)).astype(ga_ref.dtype)
    bu_ref[...] = b_u.astype(bu_ref.dtype)
    gb_ref[...] = _silu(b_z).astype(gb_ref.dtype)


def _even_in(x2, g, w, tm):
    n, d = x2.shape
    e_in = w.shape[1]
    out = jax.ShapeDtypeStruct((n, A_WIDTH), BF16)
    row = lambda i: (i, 0)
    fixed = lambda i: (0, 0)
    return pl.pallas_call(
        _even_in_kernel,
        grid=(n // tm,),
        in_specs=[pl.BlockSpec((tm, d), row),
                  pl.BlockSpec((1, d), fixed),
                  _resident((d, e_in))],
        out_specs=[pl.BlockSpec((tm, A_WIDTH), row)] * 4,
        out_shape=[out] * 4,
        scratch_shapes=[pltpu.VMEM((d, e_in), BF16)],
        compiler_params=_params(1),
        name="even_in",
    )(x2, g, w)


def _dft_tables(seq):
    slow = seq // FFT_FAST
    a = np.arange(slow)
    th = 2.0 * np.pi * np.outer(a, a) / slow
    w1 = np.concatenate([np.cos(th), -np.sin(th)], axis=0)
    b = np.arange(FFT_FAST)
    k = a[:, None, None] + slow * b[None, :, None]
    th2 = 2.0 * np.pi * (k * b[None, None, :]) / seq
    cs, sn = np.cos(th2), np.sin(th2)
    t2 = np.concatenate([np.concatenate([cs, sn], axis=2),
                         np.concatenate([-sn, cs], axis=2)], axis=1)
    ch = np.arange(B_GROUP_DIM)
    thc = 2.0 * np.pi * np.outer(ch, ch) / B_GROUP_DIM
    wc = np.concatenate([np.cos(thc), np.sin(thc)], axis=0)
    return tuple(jnp.asarray(t, dtype=F32) for t in (w1, t2, wc))


def _dft1_kernel(w_ref, u_ref, y_ref):
    _, slow, nb, width = u_ref.shape
    ut = jnp.transpose(u_ref[0], (1, 0, 2))
    x = jnp.concatenate([ut[b] for b in range(nb)], axis=1)
    y = jnp.dot(w_ref[...].astype(BF16), x, preferred_element_type=F32)
    yb = y.astype(y_ref.dtype)
    for part in range(2):
        rows = slice(part * slow, (part + 1) * slow)
        by_b = jnp.stack([yb[rows, b * width:(b + 1) * width] for b in range(nb)], axis=0)
        y_ref[0, part] = jnp.transpose(by_b, (1, 0, 2))


def _dft1(w1, u4, nb):
    bsz, slow, fast, width = u4.shape
    return pl.pallas_call(
        _dft1_kernel,
        grid=(bsz, fast // nb),
        in_specs=[pl.BlockSpec((2 * slow, slow), lambda b, j: (0, 0)),
                  pl.BlockSpec((1, slow, nb, width), lambda b, j: (b, 0, j, 0))],
        out_specs=pl.BlockSpec((1, 2, slow, nb, width), lambda b, j: (b, 0, 0, j, 0)),
        out_shape=jax.ShapeDtypeStruct((bsz, 2, slow, fast, width), BF16),
        compiler_params=_params(2),
        name="dft_stage1",
    )(w1, u4)


def _dft2_kernel(y_ref, t_ref, wc_ref, f_ref, *, cc, scale):
    gd = B_GROUP_DIM
    by_c = []
    for ci in range(cc):
        rows = slice(ci * FFT_FAST, (ci + 1) * FFT_FAST)
        ys = jnp.concatenate([y_ref[0, 0, rows, :], y_ref[0, 1, rows, :]], axis=0)
        p = jnp.dot(t_ref[ci].astype(BF16), ys, preferred_element_type=F32).astype(BF16)
        lhs = jnp.concatenate(
            [jnp.concatenate([p[:FFT_FAST, g * gd:(g + 1) * gd], p[FFT_FAST:, g * gd:(g + 1) * gd]], axis=1)
             for g in range(B_GROUPS)], axis=0)
        f = jnp.dot(lhs, wc_ref[...].astype(BF16), preferred_element_type=F32) * scale
        by_c.append(jnp.concatenate([f[g * FFT_FAST:(g + 1) * FFT_FAST] for g in range(B_GROUPS)],
                                    axis=1).astype(f_ref.dtype))
    f_ref[0] = jnp.transpose(jnp.stack(by_c, axis=0), (1, 0, 2))


def _dft2(y4, t2, wc, cc, scale):
    bsz, _, seq, width = y4.shape
    slow = seq // FFT_FAST
    return pl.pallas_call(
        functools.partial(_dft2_kernel, cc=cc, scale=scale),
        grid=(slow // cc, bsz),
        in_specs=[pl.BlockSpec((1, 2, cc * FFT_FAST, width), lambda j, b: (b, 0, j, 0)),
                  pl.BlockSpec((cc, 2 * FFT_FAST, 2 * FFT_FAST), lambda j, b: (j, 0, 0)),
                  pl.BlockSpec(wc.shape, lambda j, b: (0, 0))],
        out_specs=pl.BlockSpec((1, FFT_FAST, cc, width), lambda j, b: (b, 0, j, 0)),
        out_shape=jax.ShapeDtypeStruct((bsz, FFT_FAST, slow, width), BF16),
        compiler_params=_params(2),
        name="dft_stage2",
    )(y4, t2, wc)


def _rope_tables(seq):
    t = np.arange(seq)
    row = (t // GRID_W).astype(np.float64)
    col = (t % GRID_W).astype(np.float64)
    n_pair = HEAD_DIM // 4
    inv = ROPE_THETA ** (-np.arange(n_pair, dtype=np.float64) / n_pair)
    ang = np.concatenate([row[:, None] * inv, col[:, None] * inv], axis=-1)
    cos = np.repeat(np.cos(ang), 2, axis=1)
    sin = np.repeat(np.sin(ang), 2, axis=1)
    sign = np.tile(np.array([-1.0, 1.0]), HEAD_DIM // 2)
    return jnp.asarray(cos, dtype=F32), jnp.asarray(sin * sign, dtype=F32)


def _norm_rope(xh, gain, cos, sin_signed, even_lane):
    xn = _rms_norm(xh, gain)
    partner = jnp.where(even_lane, pltpu.roll(xn, HEAD_DIM - 1, axis=1), pltpu.roll(xn, 1, axis=1))
    return xn * cos + partner * sin_signed


def _mid_kernel(x_ref, cx_ref, cxp_ref, cxn_ref, ga_ref, f_ref, gb_ref, cw_ref, wo_ref,
                g_ref, wi_ref, qg_ref, kg_ref, cos_ref, sin_ref,
                x1_ref, q_ref, qn_ref, k_ref, v_ref, sz_ref, wob_ref, wib_ref, *, q_scale):
    first_step = (pl.program_id(0) == 0) & (pl.program_id(1) == 0)
    _cast_once(first_step, wo_ref, wob_ref)
    _cast_once(first_step, wi_ref, wib_ref)
    i = pl.program_id(1)
    n_i = pl.num_programs(1)
    tm = x_ref.shape[0]
    cx = cx_ref[...].astype(F32)
    prev_row = jnp.where(i > 0, cxp_ref[7:8, :].astype(F32), 0.0)
    next_row = jnp.where(i < n_i - 1, cxn_ref[0:1, :].astype(F32), 0.0)
    r = lax.broadcasted_iota(jnp.int32, cx.shape, 0)
    up = jnp.where(r == 0, prev_row, pltpu.roll(cx, 1, axis=0))
    dn = jnp.where(r == tm - 1, next_row, pltpu.roll(cx, tm - 1, axis=0))
    conv = up * cw_ref[0:1, :] + cx * cw_ref[1:2, :] + dn * cw_ref[2:3, :]
    y_a = ga_ref[...].astype(F32) * conv
    y_b = f_ref[...].astype(F32) * gb_ref[...].astype(F32)
    y = jnp.concatenate([y_a, y_b], axis=1).astype(BF16)
    qw = N_HEADS * HEAD_DIM
    kw = N_KV_HEADS * HEAD_DIM
    ts = tm // MID_SUBTILES
    even_lane = (lax.broadcasted_iota(jnp.int32, (ts, HEAD_DIM), 1) & 1) == 0
    ones_pad = (lax.broadcasted_iota(jnp.int32, (V_PAD_ROWS, ts), 0) == 0).astype(v_ref.dtype)
    for sub in range(MID_SUBTILES):
        rows = slice(sub * ts, (sub + 1) * ts)
        x1 = x_ref[rows, :] + jnp.dot(y[rows], wob_ref[...], preferred_element_type=F32)
        x1_ref[rows, :] = x1
        h = _rms_norm(x1, g_ref[...]).astype(BF16)
        p = jnp.dot(h, wib_ref[...], preferred_element_type=F32)
        cos = cos_ref[rows, :]
        sin = sin_ref[rows, :]
        for hh in range(N_HEADS):
            qh = _norm_rope(p[:, hh * HEAD_DIM:(hh + 1) * HEAD_DIM], qg_ref[...], cos, sin, even_lane)
            qt = (qh * q_scale).T.astype(q_ref.dtype)
            q_ref[0, hh, :, rows] = qt
            qf = qt.astype(F32)
            qn_ref[0, hh, :, rows] = jnp.sqrt(jnp.sum(qf * qf, axis=0, keepdims=True))
        for hh in range(N_KV_HEADS):
            kh = _norm_rope(p[:, qw + hh * HEAD_DIM:qw + (hh + 1) * HEAD_DIM], kg_ref[...], cos, sin, even_lane)
            k_ref[0, hh, rows, :] = kh.astype(k_ref.dtype)
            vt = p[:, qw + kw + hh * HEAD_DIM:qw + kw + (hh + 1) * HEAD_DIM].T.astype(v_ref.dtype)
            v_ref[0, hh, :, rows] = jnp.concatenate([vt, ones_pad], axis=0)
        sz_ref[rows, :] = _silu(p[:, qw + 2 * kw:]).astype(sz_ref.dtype)


def _mid(x2, cx, ga, f2, gb, conv_w, w_out, g_odd, w_in, q_gain, k_gain, cos, sin, bsz, seq, tm, q_scale):
    n, d = x2.shape
    nt = seq // tm
    halo = 8
    tpb = tm // halo
    last_blk = n // halo - 1
    row = lambda b, i: (b * nt + i, 0)
    fixed = lambda b, i: (0, 0)
    prev = lambda b, i: (jnp.maximum((b * nt + i) * tpb - 1, 0), 0)
    nxt = lambda b, i: (jnp.minimum((b * nt + i + 1) * tpb, last_blk), 0)
    pos = lambda b, i: (i, 0)
    head = lambda b, i: (b, 0, i, 0)
    head_t = lambda b, i: (b, 0, 0, i)
    half = pl.BlockSpec((tm, A_WIDTH), row)
    return pl.pallas_call(
        functools.partial(_mid_kernel, q_scale=q_scale),
        grid=(bsz, nt),
        in_specs=[pl.BlockSpec((tm, d), row),
                  half,
                  pl.BlockSpec((halo, A_WIDTH), prev),
                  pl.BlockSpec((halo, A_WIDTH), nxt),
                  half, half, half,
                  pl.BlockSpec(conv_w.shape, fixed),
                  _resident(w_out.shape),
                  pl.BlockSpec((1, d), fixed),
                  _resident(w_in.shape),
                  pl.BlockSpec((1, HEAD_DIM), fixed),
                  pl.BlockSpec((1, HEAD_DIM), fixed),
                  pl.BlockSpec((tm, HEAD_DIM), pos),
                  pl.BlockSpec((tm, HEAD_DIM), pos)],
        out_specs=[pl.BlockSpec((tm, d), row),
                   pl.BlockSpec((1, N_HEADS, HEAD_DIM, tm), head_t),
                   pl.BlockSpec((1, N_HEADS, 1, tm), head_t),
                   pl.BlockSpec((1, N_KV_HEADS, tm, HEAD_DIM), head),
                   pl.BlockSpec((1, N_KV_HEADS, HEAD_DIM + V_PAD_ROWS, tm), head_t),
                   pl.BlockSpec((tm, N_HEADS * HEAD_DIM), row)],
        out_shape=[jax.ShapeDtypeStruct((n, d), F32),
                   jax.ShapeDtypeStruct((bsz, N_HEADS, HEAD_DIM, seq), BF16),
                   jax.ShapeDtypeStruct((bsz, N_HEADS, 1, seq), F32),
                   jax.ShapeDtypeStruct((bsz, N_KV_HEADS, seq, HEAD_DIM), BF16),
                   jax.ShapeDtypeStruct((bsz, N_KV_HEADS, HEAD_DIM + V_PAD_ROWS, seq), BF16),
                   jax.ShapeDtypeStruct((n, N_HEADS * HEAD_DIM), BF16)],
        scratch_shapes=[pltpu.VMEM(w_out.shape, BF16), pltpu.VMEM(w_in.shape, BF16)],
        compiler_params=_params(2),
        name="even_out_odd_in",
    )(x2, cx, cx, cx, ga, f2, gb, conv_w, w_out, g_odd, w_in, q_gain, k_gain, cos, sin)


def _attn_kernel(q_ref, qn_ref, k_ref, v_ref, o_ref, kmax_ref, acc_ref, *, tk):
    _, grp, hd, tq = q_ref.shape
    seq = k_ref.shape[2]
    n_chunks = seq // tk
    v_rows = v_ref.shape[2]

    @pl.when(pl.program_id(2) == 0)
    def _key_norm_bound():
        def body(j, best):
            kc = k_ref[0, 0, pl.ds(pl.multiple_of(j * tk, tk), tk), :].astype(F32)
            return jnp.maximum(best, jnp.sum(kc * kc, axis=1, keepdims=True))
        best = lax.fori_loop(0, n_chunks, body, jnp.zeros((tk, 1), F32))
        kmax_ref[0] = jnp.sqrt(jnp.max(best))

    shift = [qn_ref[0, g] * (kmax_ref[0] * SHIFT_MARGIN) for g in range(grp)]
    span = functools.reduce(jnp.maximum, [jnp.max(c) for c in shift])

    def chunk(j):
        start = pl.multiple_of(j * tk, tk)
        return k_ref[0, 0, pl.ds(start, tk), :], v_ref[0, 0, :, pl.ds(start, tk)]

    bounded = span <= SCORE_SPAN_LIMIT

    @pl.when(bounded)
    def _bounded_shift():
        q_all = jnp.concatenate([q_ref[0, g] for g in range(grp)], axis=1)
        shift_all = jnp.concatenate(shift, axis=1)

        def body(j, acc):
            kc, vc = chunk(j)
            s = jnp.dot(kc, q_all, preferred_element_type=F32)
            p = jnp.exp2(s - shift_all).astype(BF16)
            return acc + jnp.dot(vc, p, preferred_element_type=F32)
        acc = lax.fori_loop(0, n_chunks, body, jnp.zeros((v_rows, grp * tq), F32), unroll=ATTN_UNROLL)
        for g in range(grp):
            acc_ref[g] = acc[:, g * tq:(g + 1) * tq]

    @pl.when(jnp.logical_not(bounded))
    def _online_max():
        for g in range(grp):
            def body(j, carry):
                m, acc = carry
                kc, vc = chunk(j)
                s = jnp.dot(kc, q_ref[0, g], preferred_element_type=F32)
                m_new = jnp.maximum(m, jnp.max(s, axis=0, keepdims=True))
                p = jnp.exp2(s - m_new).astype(BF16)
                acc = jnp.exp2(m - m_new) * acc + jnp.dot(vc, p, preferred_element_type=F32)
                return m_new, acc
            init = (jnp.full((1, tq), -jnp.inf, F32), jnp.zeros((v_rows, tq), F32))
            _, acc = lax.fori_loop(0, n_chunks, body, init)
            acc_ref[g] = acc

    for g in range(grp):
        acc = acc_ref[g]
        o_t = acc[:hd] / acc[hd:hd + 1]
        o_ref[0, :, g * hd:(g + 1) * hd] = o_t.T.astype(o_ref.dtype)


def _attention(qt, qn, k, vt, tq, tk):
    bsz, nh, hd, seq = qt.shape
    nkv = k.shape[1]
    grp = nh // nkv
    v_rows = vt.shape[2]
    return pl.pallas_call(
        functools.partial(_attn_kernel, tk=tk),
        grid=(bsz, nkv, seq // tq),
        in_specs=[pl.BlockSpec((1, grp, hd, tq), lambda b, h, i: (b, h, 0, i)),
                  pl.BlockSpec((1, grp, 1, tq), lambda b, h, i: (b, h, 0, i)),
                  pl.BlockSpec((1, 1, seq, hd), lambda b, h, i: (b, h, 0, 0)),
                  pl.BlockSpec((1, 1, v_rows, seq), lambda b, h, i: (b, h, 0, 0))],
        out_specs=pl.BlockSpec((1, tq, grp * hd), lambda b, h, i: (b, i, h)),
        out_shape=jax.ShapeDtypeStruct((bsz, seq, nh * hd), BF16),
        scratch_shapes=[pltpu.SMEM((1,), F32), pltpu.VMEM((grp, v_rows, tq), F32)],
        compiler_params=_params(3),
        name="gqa_attention",
    )(qt, qn, k, vt)


def _odd_out_kernel(x1_ref, o_ref, sz_ref, w_ref, g_ref, out_ref, wb_ref):
    _cast_once(pl.program_id(0) == 0, w_ref, wb_ref)
    y = (o_ref[...].astype(F32) * sz_ref[...].astype(F32)).astype(BF16)
    x2 = x1_ref[...] + jnp.dot(y, wb_ref[...], preferred_element_type=F32)
    out_ref[...] = _rms_norm(x2, g_ref[...])


def _odd_out(x1, o2, sz, w, g, tm):
    n, d = x1.shape
    row = lambda i: (i, 0)
    fixed = lambda i: (0, 0)
    return pl.pallas_call(
        _odd_out_kernel,
        grid=(n // tm,),
        in_specs=[pl.BlockSpec((tm, d), row),
                  pl.BlockSpec((tm, o2.shape[1]), row),
                  pl.BlockSpec((tm, sz.shape[1]), row),
                  _resident(w.shape),
                  pl.BlockSpec((1, d), fixed)],
        out_specs=pl.BlockSpec((tm, d), row),
        out_shape=jax.ShapeDtypeStruct((n, d), F32),
        scratch_shapes=[pltpu.VMEM(w.shape, BF16)],
        compiler_params=_params(1),
        name="odd_out",
    )(x1, o2, sz, w, g)


def kernel(x, norm_even, w_in_even, conv_w, w_out_even, norm_odd, w_in_odd, q_gain, k_gain, w_out_odd, final_norm):
    bsz, seq, d = x.shape
    assert norm_even.shape[0] == 1 and norm_odd.shape[0] == 1, "one even and one odd layer"
    assert seq % FFT_FAST == 0 and seq % GRID_W == 0
    n = bsz * seq
    slow = seq // FFT_FAST
    tm = min(512, seq)
    x2 = x.reshape(n, d)

    cx, ga, bu, gb = _even_in(x2, norm_even[0][None, :], w_in_even[0], min(1024, seq))
    w1, t2, wc = _dft_tables(seq)
    y = _dft1(w1, bu.reshape(bsz, slow, FFT_FAST, B_WIDTH), nb=DFT_ROWS_PER_STEP)
    scale = 1.0 / math.sqrt(seq * B_GROUP_DIM)
    f = _dft2(y.reshape(bsz, 2, seq, B_WIDTH), t2, wc, cc=min(DFT_ROWS_PER_STEP, slow), scale=scale)
    f2 = f.reshape(n, B_WIDTH)

    cos, sin = _rope_tables(seq)
    q_scale = HEAD_DIM ** -0.5 * math.log2(math.e)
    x1, qt, qn, k, vt, sz = _mid(x2, cx, ga, f2, gb, conv_w[0], w_out_even[0],
                                 norm_odd[0][None, :], w_in_odd[0],
                                 q_gain[0][None, :], k_gain[0][None, :], cos, sin, bsz, seq, tm, q_scale)

    o = _attention(qt, qn, k, vt, tq=min(1024, seq), tk=min(512, seq))
    out = _odd_out(x1, o.reshape(n, N_HEADS * HEAD_DIM), sz, w_out_odd[0], final_norm[None, :], min(1024, seq))
    return out.reshape(bsz, seq, d)
```

```python
import functools
import math

import numpy as np
import jax
import jax.numpy as jnp
from jax import lax
from jax.experimental import pallas as pl
from jax.experimental.pallas import tpu as pltpu

F32 = jnp.float32
BF16 = jnp.bfloat16

EPS = 1e-6
GRID_W = 64
ROPE_THETA = 10000.0
CONV_WIDTH = 3
HEAD_DIM = 128
N_HEADS = 8
N_KV_HEADS = 2
KV_GROUP = N_HEADS // N_KV_HEADS
A_WIDTH = 512
B_WIDTH = 512
B_GROUPS = 4
B_GROUP_DIM = 128
FFT_FAST = 128
DFT_ROWS_PER_STEP = 16
V_PAD_ROWS = 16
SCORE_SPAN_LIMIT = 60.0
SHIFT_MARGIN = 1.01
MID_SUBTILES = 2
ATTN_QUERY_LANES = 1024
ATTN_UNROLL = 16

VMEM_LIMIT_BYTES = 56 * 1024 * 1024


def _silu(z):
    return z / (1.0 + jnp.exp(-z))


def _rms_norm(x, g):
    return x * lax.rsqrt(jnp.mean(x * x, axis=-1, keepdims=True) + EPS) * g


def _resident(shape):
    zeros = (0,) * len(shape)
    return pl.BlockSpec(shape, lambda *_: zeros, pipeline_mode=pl.Buffered(1))


def _cast_once(first_step, src_ref, dst_ref):
    @pl.when(first_step)
    def _():
        dst_ref[...] = src_ref[...].astype(dst_ref.dtype)


def _params(n_grid_dims):
    return pltpu.CompilerParams(
        dimension_semantics=("arbitrary",) * n_grid_dims,
        vmem_limit_bytes=VMEM_LIMIT_BYTES)


def _even_in_kernel(x_ref, g_ref, w_ref, cx_ref, ga_ref, bu_ref, gb_ref, wb_ref):
    _cast_once(pl.program_id(0) == 0, w_ref, wb_ref)
    h = _rms_norm(x_ref[...], g_ref[...]).astype(BF16)
    p = jnp.dot(h, wb_ref[...], preferred_element_type=F32)
    a_x = p[:, 0 * A_WIDTH:1 * A_WIDTH]
    a_b = p[:, 1 * A_WIDTH:2 * A_WIDTH]
    a_c = p[:, 2 * A_WIDTH:3 * A_WIDTH]
    a_z = p[:, 3 * A_WIDTH:4 * A_WIDTH]
    b_u = p[:, 4 * A_WIDTH:4 * A_WIDTH + B_WIDTH]
    b_z = p[:, 4 * A_WIDTH + B_WIDTH:]
    cx_ref[...] = (a_c * a_x).astype(cx_ref.dtype)
    ga_ref[...] = (a_b * _silu(a_z)).astype(ga_ref.dtype)
    bu_ref[...] = b_u.astype(bu_ref.dtype)
    gb_ref[...] = _silu(b_z).astype(gb_ref.dtype)


def _even_in(x2, g, w, tm):
    n, d = x2.shape
    e_in = w.shape[1]
    out = jax.ShapeDtypeStruct((n, A_WIDTH), BF16)
    row = lambda i: (i, 0)
    fixed = lambda i: (0, 0)
    return pl.pallas_call(
        _even_in_kernel,
        grid=(n // tm,),
        in_specs=[pl.BlockSpec((tm, d), row),
                  pl.BlockSpec((1, d), fixed),
                  _resident((d, e_in))],
        out_specs=[pl.BlockSpec((tm, A_WIDTH), row)] * 4,
        out_shape=[out] * 4,
        scratch_shapes=[pltpu.VMEM((d, e_in), BF16)],
        compiler_params=_params(1),
        name="even_in",
    )(x2, g, w)


def _dft_tables(seq):
    slow = seq // FFT_FAST
    a = np.arange(slow)
    th = 2.0 * np.pi * np.outer(a, a) / slow
    w1 = np.concatenate([np.cos(th), -np.sin(th)], axis=0)
    b = np.arange(FFT_FAST)
    k = a[:, None, None] + slow * b[None, :, None]
    th2 = 2.0 * np.pi * (k * b[None, None, :]) / seq
    cs, sn = np.cos(th2), np.sin(th2)
    t2 = np.concatenate([np.concatenate([cs, sn], axis=2),
                         np.concatenate([-sn, cs], axis=2)], axis=1)
    ch = np.arange(B_GROUP_DIM)
    thc = 2.0 * np.pi * np.outer(ch, ch) / B_GROUP_DIM
    wc = np.concatenate([np.cos(thc), np.sin(thc)], axis=0)
    return tuple(jnp.asarray(t, dtype=F32) for t in (w1, t2, wc))


def _dft1_kernel(w_ref, u_ref, y_ref):
    _, slow, nb, width = u_ref.shape
    ut = jnp.transpose(u_ref[0], (1, 0, 2))
    x = jnp.concatenate([ut[b] for b in range(nb)], axis=1)
    y = jnp.dot(w_ref[...].astype(BF16), x, preferred_element_type=F32)
    yb = y.astype(y_ref.dtype)
    for part in range(2):
        rows = slice(part * slow, (part + 1) * slow)
        by_b = jnp.stack([yb[rows, b * width:(b + 1) * width] for b in range(nb)], axis=0)
        y_ref[0, part] = jnp.transpose(by_b, (1, 0, 2))


def _dft1(w1, u4, nb):
    bsz, slow, fast, width = u4.shape
    return pl.pallas_call(
        _dft1_kernel,
        grid=(bsz, fast // nb),
        in_specs=[pl.BlockSpec((2 * slow, slow), lambda b, j: (0, 0)),
                  pl.BlockSpec((1, slow, nb, width), lambda b, j: (b, 0, j, 0))],
        out_specs=pl.BlockSpec((1, 2, slow, nb, width), lambda b, j: (b, 0, 0, j, 0)),
        out_shape=jax.ShapeDtypeStruct((bsz, 2, slow, fast, width), BF16),
        compiler_params=_params(2),
        name="dft_stage1",
    )(w1, u4)


def _dft2_kernel(y_ref, t_ref, wc_ref, f_ref, *, cc, scale):
    gd = B_GROUP_DIM
    by_c = []
    for ci in range(cc):
        rows = slice(ci * FFT_FAST, (ci + 1) * FFT_FAST)
        ys = jnp.concatenate([y_ref[0, 0, rows, :], y_ref[0, 1, rows, :]], axis=0)
        p = jnp.dot(t_ref[ci].astype(BF16), ys, preferred_element_type=F32).astype(BF16)
        lhs = jnp.concatenate(
            [jnp.concatenate([p[:FFT_FAST, g * gd:(g + 1) * gd], p[FFT_FAST:, g * gd:(g + 1) * gd]], axis=1)
             for g in range(B_GROUPS)], axis=0)
        f = jnp.dot(lhs, wc_ref[...].astype(BF16), preferred_element_type=F32) * scale
        by_c.append(jnp.concatenate([f[g * FFT_FAST:(g + 1) * FFT_FAST] for g in range(B_GROUPS)],
                                    axis=1).astype(f_ref.dtype))
    f_ref[0] = jnp.transpose(jnp.stack(by_c, axis=0), (1, 0, 2))


def _dft2(y4, t2, wc, cc, scale):
    bsz, _, seq, width = y4.shape
    slow = seq // FFT_FAST
    return pl.pallas_call(
        functools.partial(_dft2_kernel, cc=cc, scale=scale),
        grid=(slow // cc, bsz),
        in_specs=[pl.BlockSpec((1, 2, cc * FFT_FAST, width), lambda j, b: (b, 0, j, 0)),
                  pl.BlockSpec((cc, 2 * FFT_FAST, 2 * FFT_FAST), lambda j, b: (j, 0, 0)),
                  pl.BlockSpec(wc.shape, lambda j, b: (0, 0))],
        out_specs=pl.BlockSpec((1, FFT_FAST, cc, width), lambda j, b: (b, 0, j, 0)),
        out_shape=jax.ShapeDtypeStruct((bsz, FFT_FAST, slow, width), BF16),
        compiler_params=_params(2),
        name="dft_stage2",
    )(y4, t2, wc)


def _rope_tables(seq):
    t = np.arange(seq)
    row = (t // GRID_W).astype(np.float64)
    col = (t % GRID_W).astype(np.float64)
    n_pair = HEAD_DIM // 4
    inv = ROPE_THETA ** (-np.arange(n_pair, dtype=np.float64) / n_pair)
    ang = np.concatenate([row[:, None] * inv, col[:, None] * inv], axis=-1)
    cos = np.repeat(np.cos(ang), 2, axis=1)
    sin = np.repeat(np.sin(ang), 2, axis=1)
    sign = np.tile(np.array([-1.0, 1.0]), HEAD_DIM // 2)
    return jnp.asarray(cos, dtype=F32), jnp.asarray(sin * sign, dtype=F32)


def _norm_rope(xh, gain, cos, sin_signed, even_lane):
    xn = _rms_norm(xh, gain)
    partner = jnp.where(even_lane, pltpu.roll(xn, HEAD_DIM - 1, axis=1), pltpu.roll(xn, 1, axis=1))
    return xn * cos + partner * sin_signed


def _mid_kernel(x_ref, cx_ref, cxp_ref, cxn_ref, ga_ref, f_ref, gb_ref, cw_ref, wo_ref,
                g_ref, wi_ref, qg_ref, kg_ref, cos_ref, sin_ref,
                x1_ref, q_ref, qn_ref, k_ref, v_ref, sz_ref, wob_ref, wib_ref, *, q_scale):
    first_step = (pl.program_id(0) == 0) & (pl.program_id(1) == 0)
    _cast_once(first_step, wo_ref, wob_ref)
    _cast_once(first_step, wi_ref, wib_ref)
    i = pl.program_id(1)
    n_i = pl.num_programs(1)
    tm = x_ref.shape[0]
    cx = cx_ref[...].astype(F32)
    prev_row = jnp.where(i > 0, cxp_ref[7:8, :].astype(F32), 0.0)
    next_row = jnp.where(i < n_i - 1, cxn_ref[0:1, :].astype(F32), 0.0)
    r = lax.broadcasted_iota(jnp.int32, cx.shape, 0)
    up = jnp.where(r == 0, prev_row, pltpu.roll(cx, 1, axis=0))
    dn = jnp.where(r == tm - 1, next_row, pltpu.roll(cx, tm - 1, axis=0))
    conv = up * cw_ref[0:1, :] + cx * cw_ref[1:2, :] + dn * cw_ref[2:3, :]
    y_a = ga_ref[...].astype(F32) * conv
    y_b = f_ref[...].astype(F32) * gb_ref[...].astype(F32)
    y = jnp.concatenate([y_a, y_b], axis=1).astype(BF16)
    qw = N_HEADS * HEAD_DIM
    kw = N_KV_HEADS * HEAD_DIM
    ts = tm // MID_SUBTILES
    even_lane = (lax.broadcasted_iota(jnp.int32, (ts, HEAD_DIM), 1) & 1) == 0
    ones_pad = (lax.broadcasted_iota(jnp.int32, (V_PAD_ROWS, ts), 0) == 0).astype(v_ref.dtype)
    for sub in range(MID_SUBTILES):
        rows = slice(sub * ts, (sub + 1) * ts)
        x1 = x_ref[rows, :] + jnp.dot(y[rows], wob_ref[...], preferred_element_type=F32)
        x1_ref[rows, :] = x1
        h = _rms_norm(x1, g_ref[...]).astype(BF16)
        p = jnp.dot(h, wib_ref[...], preferred_element_type=F32)
        cos = cos_ref[rows, :]
        sin = sin_ref[rows, :]
        for hh in range(N_HEADS):
            qh = _norm_rope(p[:, hh * HEAD_DIM:(hh + 1) * HEAD_DIM], qg_ref[...], cos, sin, even_lane)
            qt = (qh * q_scale).T.astype(q_ref.dtype)
            q_ref[0, hh, :, rows] = qt
            qf = qt.astype(F32)
            qn_ref[0, hh, :, rows] = jnp.sqrt(jnp.sum(qf * qf, axis=0, keepdims=True))
        for hh in range(N_KV_HEADS):
            kh = _norm_rope(p[:, qw + hh * HEAD_DIM:qw + (hh + 1) * HEAD_DIM], kg_ref[...], cos, sin, even_lane)
            k_ref[0, hh, rows, :] = kh.astype(k_ref.dtype)
            vt = p[:, qw + kw + hh * HEAD_DIM:qw + kw + (hh + 1) * HEAD_DIM].T.astype(v_ref.dtype)
            v_ref[0, hh, :, rows] = jnp.concatenate([vt, ones_pad], axis=0)
        sz_ref[rows, :] = _silu(p[:, qw + 2 * kw:]).astype(sz_ref.dtype)


def _mid(x2, cx, ga, f2, gb, conv_w, w_out, g_odd, w_in, q_gain, k_gain, cos, sin, bsz, seq, tm, q_scale):
    n, d = x2.shape
    nt = seq // tm
    halo = 8
    tpb = tm // halo
    last_blk = n // halo - 1
    row = lambda b, i: (b * nt + i, 0)
    fixed = lambda b, i: (0, 0)
    prev = lambda b, i: (jnp.maximum((b * nt + i) * tpb - 1, 0), 0)
    nxt = lambda b, i: (jnp.minimum((b * nt + i + 1) * tpb, last_blk), 0)
    pos = lambda b, i: (i, 0)
    head = lambda b, i: (b, 0, i, 0)
    head_t = lambda b, i: (b, 0, 0, i)
    half = pl.BlockSpec((tm, A_WIDTH), row)
    return pl.pallas_call(
        functools.partial(_mid_kernel, q_scale=q_scale),
        grid=(bsz, nt),
        in_specs=[pl.BlockSpec((tm, d), row),
                  half,
                  pl.BlockSpec((halo, A_WIDTH), prev),
                  pl.BlockSpec((halo, A_WIDTH), nxt),
                  half, half, half,
                  pl.BlockSpec(conv_w.shape, fixed),
                  _resident(w_out.shape),
                  pl.BlockSpec((1, d), fixed),
                  _resident(w_in.shape),
                  pl.BlockSpec((1, HEAD_DIM), fixed),
                  pl.BlockSpec((1, HEAD_DIM), fixed),
                  pl.BlockSpec((tm, HEAD_DIM), pos),
                  pl.BlockSpec((tm, HEAD_DIM), pos)],
        out_specs=[pl.BlockSpec((tm, d), row),
                   pl.BlockSpec((1, N_HEADS, HEAD_DIM, tm), head_t),
                   pl.BlockSpec((1, N_HEADS, 1, tm), head_t),
                   pl.BlockSpec((1, N_KV_HEADS, tm, HEAD_DIM), head),
                   pl.BlockSpec((1, N_KV_HEADS, HEAD_DIM + V_PAD_ROWS, tm), head_t),
                   pl.BlockSpec((tm, N_HEADS * HEAD_DIM), row)],
        out_shape=[jax.ShapeDtypeStruct((n, d), F32),
                   jax.ShapeDtypeStruct((bsz, N_HEADS, HEAD_DIM, seq), BF16),
                   jax.ShapeDtypeStruct((bsz, N_HEADS, 1, seq), F32),
                   jax.ShapeDtypeStruct((bsz, N_KV_HEADS, seq, HEAD_DIM), BF16),
                   jax.ShapeDtypeStruct((bsz, N_KV_HEADS, HEAD_DIM + V_PAD_ROWS, seq), BF16),
                   jax.ShapeDtypeStruct((n, N_HEADS * HEAD_DIM), BF16)],
        scratch_shapes=[pltpu.VMEM(w_out.shape, BF16), pltpu.VMEM(w_in.shape, BF16)],
        compiler_params=_params(2),
        name="even_out_odd_in",
    )(x2, cx, cx, cx, ga, f2, gb, conv_w, w_out, g_odd, w_in, q_gain, k_gain, cos, sin)


def _attn_kernel(q_ref, qn_ref, k_ref, v_ref, o_ref, kmax_ref, acc_ref, *, tk):
    _, grp, hd, tq = q_ref.shape
    seq = k_ref.shape[2]
    n_chunks = seq // tk
    v_rows = v_ref.shape[2]

    @pl.when(pl.program_id(2) == 0)
    def _key_norm_bound():
        def body(j, best):
            kc = k_ref[0, 0, pl.ds(pl.multiple_of(j * tk, tk), tk), :].astype(F32)
            return jnp.maximum(best, jnp.sum(kc * kc, axis=1, keepdims=True))
        best = lax.fori_loop(0, n_chunks, body, jnp.zeros((tk, 1), F32))
        kmax_ref[0] = jnp.sqrt(jnp.max(best))

    shift = [qn_ref[0, g] * (kmax_ref[0] * SHIFT_MARGIN) for g in range(grp)]
    span = functools.reduce(jnp.maximum, [jnp.max(c) for c in shift])

    def chunk(j):
        start = pl.multiple_of(j * tk, tk)
        return k_ref[0, 0, pl.ds(start, tk), :], v_ref[0, 0, :, pl.ds(start, tk)]

    bounded = span <= SCORE_SPAN_LIMIT

    @pl.when(bounded)
    def _bounded_shift():
        n_parts = max(1, grp * tq // ATTN_QUERY_LANES)
        hp = grp // n_parts
        q_parts = [jnp.concatenate([q_ref[0, g] for g in range(s * hp, (s + 1) * hp)], axis=1)
                   for s in range(n_parts)]
        shift_parts = [jnp.concatenate(shift[s * hp:(s + 1) * hp], axis=1) for s in range(n_parts)]

        def body(j, accs):
            kc, vc = chunk(j)
            out = []
            for s in range(n_parts):
                sc = jnp.dot(kc, q_parts[s], preferred_element_type=F32)
                p = jnp.exp2(sc - shift_parts[s]).astype(BF16)
                out.append(accs[s] + jnp.dot(vc, p, preferred_element_type=F32))
            return tuple(out)
        init = tuple(jnp.zeros((v_rows, hp * tq), F32) for _ in range(n_parts))
        accs = lax.fori_loop(0, n_chunks, body, init, unroll=ATTN_UNROLL)
        for g in range(grp):
            acc_ref[g] = accs[g // hp][:, (g % hp) * tq:(g % hp + 1) * tq]

    @pl.when(jnp.logical_not(bounded))
    def _online_max():
        for g in range(grp):
            def body(j, carry):
                m, acc = carry
                kc, vc = chunk(j)
                s = jnp.dot(kc, q_ref[0, g], preferred_element_type=F32)
                m_new = jnp.maximum(m, jnp.max(s, axis=0, keepdims=True))
                p = jnp.exp2(s - m_new).astype(BF16)
                acc = jnp.exp2(m - m_new) * acc + jnp.dot(vc, p, preferred_element_type=F32)
                return m_new, acc
            init = (jnp.full((1, tq), -jnp.inf, F32), jnp.zeros((v_rows, tq), F32))
            _, acc = lax.fori_loop(0, n_chunks, body, init)
            acc_ref[g] = acc

    for g in range(grp):
        acc = acc_ref[g]
        o_t = acc[:hd] / acc[hd:hd + 1]
        o_ref[0, :, g * hd:(g + 1) * hd] = o_t.T.astype(o_ref.dtype)


def _attention(qt, qn, k, vt, tq, tk):
    bsz, nh, hd, seq = qt.shape
    nkv = k.shape[1]
    grp = nh // nkv
    v_rows = vt.shape[2]
    return pl.pallas_call(
        functools.partial(_attn_kernel, tk=tk),
        grid=(bsz, nkv, seq // tq),
        in_specs=[pl.BlockSpec((1, grp, hd, tq), lambda b, h, i: (b, h, 0, i)),
                  pl.BlockSpec((1, grp, 1, tq), lambda b, h, i: (b, h, 0, i)),
                  pl.BlockSpec((1, 1, seq, hd), lambda b, h, i: (b, h, 0, 0)),
                  pl.BlockSpec((1, 1, v_rows, seq), lambda b, h, i: (b, h, 0, 0))],
        out_specs=pl.BlockSpec((1, tq, grp * hd), lambda b, h, i: (b, i, h)),
        out_shape=jax.ShapeDtypeStruct((bsz, seq, nh * hd), BF16),
        scratch_shapes=[pltpu.SMEM((1,), F32), pltpu.VMEM((grp, v_rows, tq), F32)],
        compiler_params=_params(3),
        name="gqa_attention",
    )(qt, qn, k, vt)


def _odd_out_kernel(x1_ref, o_ref, sz_ref, w_ref, g_ref, out_ref, wb_ref):
    _cast_once(pl.program_id(0) == 0, w_ref, wb_ref)
    y = (o_ref[...].astype(F32) * sz_ref[...].astype(F32)).astype(BF16)
    x2 = x1_ref[...] + jnp.dot(y, wb_ref[...], preferred_element_type=F32)
    out_ref[...] = _rms_norm(x2, g_ref[...])


def _odd_out(x1, o2, sz, w, g, tm):
    n, d = x1.shape
    row = lambda i: (i, 0)
    fixed = lambda i: (0, 0)
    return pl.pallas_call(
        _odd_out_kernel,
        grid=(n // tm,),
        in_specs=[pl.BlockSpec((tm, d), row),
                  pl.BlockSpec((tm, o2.shape[1]), row),
                  pl.BlockSpec((tm, sz.shape[1]), row),
                  _resident(w.shape),
                  pl.BlockSpec((1, d), fixed)],
        out_specs=pl.BlockSpec((tm, d), row),
        out_shape=jax.ShapeDtypeStruct((n, d), F32),
        scratch_shapes=[pltpu.VMEM(w.shape, BF16)],
        compiler_params=_params(1),
        name="odd_out",
    )(x1, o2, sz, w, g)


def kernel(x, norm_even, w_in_even, conv_w, w_out_even, norm_odd, w_in_odd, q_gain, k_gain, w_out_odd, final_norm):
    bsz, seq, d = x.shape
    assert norm_even.shape[0] == 1 and norm_odd.shape[0] == 1, "one even and one odd layer"
    assert seq % FFT_FAST == 0 and seq % GRID_W == 0
    n = bsz * seq
    slow = seq // FFT_FAST
    tm = min(512, seq)
    x2 = x.reshape(n, d)

    cx, ga, bu, gb = _even_in(x2, norm_even[0][None, :], w_in_even[0], min(1024, seq))
    w1, t2, wc = _dft_tables(seq)
    y = _dft1(w1, bu.reshape(bsz, slow, FFT_FAST, B_WIDTH), nb=DFT_ROWS_PER_STEP)
    scale = 1.0 / math.sqrt(seq * B_GROUP_DIM)
    f = _dft2(y.reshape(bsz, 2, seq, B_WIDTH), t2, wc, cc=min(DFT_ROWS_PER_STEP, slow), scale=scale)
    f2 = f.reshape(n, B_WIDTH)

    cos, sin = _rope_tables(seq)
    q_scale = HEAD_DIM ** -0.5 * math.log2(math.e)
    x1, qt, qn, k, vt, sz = _mid(x2, cx, ga, f2, gb, conv_w[0], w_out_even[0],
                                 norm_odd[0][None, :], w_in_odd[0],
                                 q_gain[0][None, :], k_gain[0][None, :], cos, sin, bsz, seq, tm, q_scale)

    o = _attention(qt, qn, k, vt, tq=min(512, seq), tk=min(512, seq))
    out = _odd_out(x1, o.reshape(n, N_HEADS * HEAD_DIM), sz, w_out_odd[0], final_norm[None, :], min(1024, seq))
    return out.reshape(bsz, seq, d)
```

```python
import functools
import math

import numpy as np
import jax
import jax.numpy as jnp
from jax import lax
from jax.experimental import pallas as pl
from jax.experimental.pallas import tpu as pltpu

F32 = jnp.float32
BF16 = jnp.bfloat16

EPS = 1e-6
GRID_W = 64
ROPE_THETA = 10000.0
CONV_WIDTH = 3
HEAD_DIM = 128
N_HEADS = 8
N_KV_HEADS = 2
KV_GROUP = N_HEADS // N_KV_HEADS
A_WIDTH = 512
B_WIDTH = 512
B_GROUPS = 4
B_GROUP_DIM = 128
FFT_FAST = 128
DFT_ROWS_PER_STEP = 16
V_PAD_ROWS = 16
SCORE_SPAN_LIMIT = 60.0
SHIFT_MARGIN = 1.01
MID_SUBTILES = 2
ATTN_QUERY_LANES = 2048
ATTN_UNROLL = 16

VMEM_LIMIT_BYTES = 56 * 1024 * 1024


def _silu(z):
    return z / (1.0 + jnp.exp(-z))


def _rms_norm(x, g):
    return x * lax.rsqrt(jnp.mean(x * x, axis=-1, keepdims=True) + EPS) * g


def _resident(shape):
    zeros = (0,) * len(shape)
    return pl.BlockSpec(shape, lambda *_: zeros, pipeline_mode=pl.Buffered(1))


def _cast_once(first_step, src_ref, dst_ref):
    @pl.when(first_step)
    def _():
        dst_ref[...] = src_ref[...].astype(dst_ref.dtype)


def _params(n_grid_dims):
    return pltpu.CompilerParams(
        dimension_semantics=("arbitrary",) * n_grid_dims,
        vmem_limit_bytes=VMEM_LIMIT_BYTES)


def _even_in_kernel(x_ref, g_ref, w_ref, cx_ref, ga_ref, bu_ref, gb_ref, wb_ref):
    _cast_once(pl.program_id(0) == 0, w_ref, wb_ref)
    h = _rms_norm(x_ref[...], g_ref[...]).astype(BF16)
    p = jnp.dot(h, wb_ref[...], preferred_element_type=F32)
    a_x = p[:, 0 * A_WIDTH:1 * A_WIDTH]
    a_b = p[:, 1 * A_WIDTH:2 * A_WIDTH]
    a_c = p[:, 2 * A_WIDTH:3 * A_WIDTH]
    a_z = p[:, 3 * A_WIDTH:4 * A_WIDTH]
    b_u = p[:, 4 * A_WIDTH:4 * A_WIDTH + B_WIDTH]
    b_z = p[:, 4 * A_WIDTH + B_WIDTH:]
    cx_ref[...] = (a_c * a_x).astype(cx_ref.dtype)
    ga_ref[...] = (a_b * _silu(a_z)).astype(ga_ref.dtype)
    bu_ref[...] = b_u.astype(bu_ref.dtype)
    gb_ref[...] = _silu(b_z).astype(gb_ref.dtype)


def _even_in(x2, g, w, tm):
    n, d = x2.shape
    e_in = w.shape[1]
    out = jax.ShapeDtypeStruct((n, A_WIDTH), BF16)
    row = lambda i: (i, 0)
    fixed = lambda i: (0, 0)
    return pl.pallas_call(
        _even_in_kernel,
        grid=(n // tm,),
        in_specs=[pl.BlockSpec((tm, d), row),
                  pl.BlockSpec((1, d), fixed),
                  _resident((d, e_in))],
        out_specs=[pl.BlockSpec((tm, A_WIDTH), row)] * 4,
        out_shape=[out] * 4,
        scratch_shapes=[pltpu.VMEM((d, e_in), BF16)],
        compiler_params=_params(1),
        name="even_in",
    )(x2, g, w)


def _dft_tables(seq):
    slow = seq // FFT_FAST
    a = np.arange(slow)
    th = 2.0 * np.pi * np.outer(a, a) / slow
    w1 = np.concatenate([np.cos(th), -np.sin(th)], axis=0)
    b = np.arange(FFT_FAST)
    k = a[:, None, None] + slow * b[None, :, None]
    th2 = 2.0 * np.pi * (k * b[None, None, :]) / seq
    cs, sn = np.cos(th2), np.sin(th2)
    t2 = np.concatenate([np.concatenate([cs, sn], axis=2),
                         np.concatenate([-sn, cs], axis=2)], axis=1)
    ch = np.arange(B_GROUP_DIM)
    thc = 2.0 * np.pi * np.outer(ch, ch) / B_GROUP_DIM
    wc = np.concatenate([np.cos(thc), np.sin(thc)], axis=0)
    return tuple(jnp.asarray(t, dtype=F32) for t in (w1, t2, wc))


def _dft1_kernel(w_ref, u_ref, y_ref):
    _, slow, nb, width = u_ref.shape
    ut = jnp.transpose(u_ref[0], (1, 0, 2))
    x = jnp.concatenate([ut[b] for b in range(nb)], axis=1)
    y = jnp.dot(w_ref[...].astype(BF16), x, preferred_element_type=F32)
    yb = y.astype(y_ref.dtype)
    for part in range(2):
        rows = slice(part * slow, (part + 1) * slow)
        by_b = jnp.stack([yb[rows, b * width:(b + 1) * width] for b in range(nb)], axis=0)
        y_ref[0, part] = jnp.transpose(by_b, (1, 0, 2))


def _dft1(w1, u4, nb):
    bsz, slow, fast, width = u4.shape
    return pl.pallas_call(
        _dft1_kernel,
        grid=(bsz, fast // nb),
        in_specs=[pl.BlockSpec((2 * slow, slow), lambda b, j: (0, 0)),
                  pl.BlockSpec((1, slow, nb, width), lambda b, j: (b, 0, j, 0))],
        out_specs=pl.BlockSpec((1, 2, slow, nb, width), lambda b, j: (b, 0, 0, j, 0)),
        out_shape=jax.ShapeDtypeStruct((bsz, 2, slow, fast, width), BF16),
        compiler_params=_params(2),
        name="dft_stage1",
    )(w1, u4)


def _dft2_kernel(y_ref, t_ref, wc_ref, f_ref, *, cc, scale):
    gd = B_GROUP_DIM
    by_c = []
    for ci in range(cc):
        rows = slice(ci * FFT_FAST, (ci + 1) * FFT_FAST)
        ys = jnp.concatenate([y_ref[0, 0, rows, :], y_ref[0, 1, rows, :]], axis=0)
        p = jnp.dot(t_ref[ci].astype(BF16), ys, preferred_element_type=F32).astype(BF16)
        lhs = jnp.concatenate(
            [jnp.concatenate([p[:FFT_FAST, g * gd:(g + 1) * gd], p[FFT_FAST:, g * gd:(g + 1) * gd]], axis=1)
             for g in range(B_GROUPS)], axis=0)
        f = jnp.dot(lhs, wc_ref[...].astype(BF16), preferred_element_type=F32) * scale
        by_c.append(jnp.concatenate([f[g * FFT_FAST:(g + 1) * FFT_FAST] for g in range(B_GROUPS)],
                                    axis=1).astype(f_ref.dtype))
    f_ref[0] = jnp.transpose(jnp.stack(by_c, axis=0), (1, 0, 2))


def _dft2(y4, t2, wc, cc, scale):
    bsz, _, seq, width = y4.shape
    slow = seq // FFT_FAST
    return pl.pallas_call(
        functools.partial(_dft2_kernel, cc=cc, scale=scale),
        grid=(slow // cc, bsz),
        in_specs=[pl.BlockSpec((1, 2, cc * FFT_FAST, width), lambda j, b: (b, 0, j, 0)),
                  pl.BlockSpec((cc, 2 * FFT_FAST, 2 * FFT_FAST), lambda j, b: (j, 0, 0)),
                  pl.BlockSpec(wc.shape, lambda j, b: (0, 0))],
        out_specs=pl.BlockSpec((1, FFT_FAST, cc, width), lambda j, b: (b, 0, j, 0)),
        out_shape=jax.ShapeDtypeStruct((bsz, FFT_FAST, slow, width), BF16),
        compiler_params=_params(2),
        name="dft_stage2",
    )(y4, t2, wc)


def _rope_tables(seq):
    t = np.arange(seq)
    row = (t // GRID_W).astype(np.float64)
    col = (t % GRID_W).astype(np.float64)
    n_pair = HEAD_DIM // 4
    inv = ROPE_THETA ** (-np.arange(n_pair, dtype=np.float64) / n_pair)
    ang = np.concatenate([row[:, None] * inv, col[:, None] * inv], axis=-1)
    cos = np.repeat(np.cos(ang), 2, axis=1)
    sin = np.repeat(np.sin(ang), 2, axis=1)
    sign = np.tile(np.array([-1.0, 1.0]), HEAD_DIM // 2)
    return jnp.asarray(cos, dtype=F32), jnp.asarray(sin * sign, dtype=F32)


def _norm_rope(xh, gain, cos, sin_signed, even_lane):
    xn = _rms_norm(xh, gain)
    partner = jnp.where(even_lane, pltpu.roll(xn, HEAD_DIM - 1, axis=1), pltpu.roll(xn, 1, axis=1))
    return xn * cos + partner * sin_signed


def _mid_kernel(x_ref, cx_ref, cxp_ref, cxn_ref, ga_ref, f_ref, gb_ref, cw_ref, wo_ref,
                g_ref, wi_ref, qg_ref, kg_ref, cos_ref, sin_ref,
                x1_ref, q_ref, qn_ref, k_ref, v_ref, sz_ref, wob_ref, wib_ref, *, q_scale):
    first_step = (pl.program_id(0) == 0) & (pl.program_id(1) == 0)
    _cast_once(first_step, wo_ref, wob_ref)
    _cast_once(first_step, wi_ref, wib_ref)
    i = pl.program_id(1)
    n_i = pl.num_programs(1)
    tm = x_ref.shape[0]
    cx = cx_ref[...].astype(F32)
    prev_row = jnp.where(i > 0, cxp_ref[7:8, :].astype(F32), 0.0)
    next_row = jnp.where(i < n_i - 1, cxn_ref[0:1, :].astype(F32), 0.0)
    r = lax.broadcasted_iota(jnp.int32, cx.shape, 0)
    up = jnp.where(r == 0, prev_row, pltpu.roll(cx, 1, axis=0))
    dn = jnp.where(r == tm - 1, next_row, pltpu.roll(cx, tm - 1, axis=0))
    conv = up * cw_ref[0:1, :] + cx * cw_ref[1:2, :] + dn * cw_ref[2:3, :]
    y_a = ga_ref[...].astype(F32) * conv
    y_b = f_ref[...].astype(F32) * gb_ref[...].astype(F32)
    y = jnp.concatenate([y_a, y_b], axis=1).astype(BF16)
    qw = N_HEADS * HEAD_DIM
    kw = N_KV_HEADS * HEAD_DIM
    ts = tm // MID_SUBTILES
    even_lane = (lax.broadcasted_iota(jnp.int32, (ts, HEAD_DIM), 1) & 1) == 0
    ones_pad = (lax.broadcasted_iota(jnp.int32, (V_PAD_ROWS, ts), 0) == 0).astype(v_ref.dtype)
    for sub in range(MID_SUBTILES):
        rows = slice(sub * ts, (sub + 1) * ts)
        x1 = x_ref[rows, :] + jnp.dot(y[rows], wob_ref[...], preferred_element_type=F32)
        x1_ref[rows, :] = x1
        h = _rms_norm(x1, g_ref[...]).astype(BF16)
        p = jnp.dot(h, wib_ref[...], preferred_element_type=F32)
        cos = cos_ref[rows, :]
        sin = sin_ref[rows, :]
        for hh in range(N_HEADS):
            qh = _norm_rope(p[:, hh * HEAD_DIM:(hh + 1) * HEAD_DIM], qg_ref[...], cos, sin, even_lane)
            qt = (qh * q_scale).T.astype(q_ref.dtype)
            q_ref[0, hh, :, rows] = qt
            qf = qt.astype(F32)
            qn_ref[0, hh, :, rows] = jnp.sqrt(jnp.sum(qf * qf, axis=0, keepdims=True))
        for hh in range(N_KV_HEADS):
            kh = _norm_rope(p[:, qw + hh * HEAD_DIM:qw + (hh + 1) * HEAD_DIM], kg_ref[...], cos, sin, even_lane)
            k_ref[0, hh, rows, :] = kh.astype(k_ref.dtype)
            vt = p[:, qw + kw + hh * HEAD_DIM:qw + kw + (hh + 1) * HEAD_DIM].T.astype(v_ref.dtype)
            v_ref[0, hh, :, rows] = jnp.concatenate([vt, ones_pad], axis=0)
        sz_ref[rows, :] = _silu(p[:, qw + 2 * kw:]).astype(sz_ref.dtype)


def _mid(x2, cx, ga, f2, gb, conv_w, w_out, g_odd, w_in, q_gain, k_gain, cos, sin, bsz, seq, tm, q_scale):
    n, d = x2.shape
    nt = seq // tm
    halo = 8
    tpb = tm // halo
    last_blk = n // halo - 1
    row = lambda b, i: (b * nt + i, 0)
    fixed = lambda b, i: (0, 0)
    prev = lambda b, i: (jnp.maximum((b * nt + i) * tpb - 1, 0), 0)
    nxt = lambda b, i: (jnp.minimum((b * nt + i + 1) * tpb, last_blk), 0)
    pos = lambda b, i: (i, 0)
    head = lambda b, i: (b, 0, i, 0)
    head_t = lambda b, i: (b, 0, 0, i)
    half = pl.BlockSpec((tm, A_WIDTH), row)
    return pl.pallas_call(
        functools.partial(_mid_kernel, q_scale=q_scale),
        grid=(bsz, nt),
        in_specs=[pl.BlockSpec((tm, d), row),
                  half,
                  pl.BlockSpec((halo, A_WIDTH), prev),
                  pl.BlockSpec((halo, A_WIDTH), nxt),
                  half, half, half,
                  pl.BlockSpec(conv_w.shape, fixed),
                  _resident(w_out.shape),
                  pl.BlockSpec((1, d), fixed),
                  _resident(w_in.shape),
                  pl.BlockSpec((1, HEAD_DIM), fixed),
                  pl.BlockSpec((1, HEAD_DIM), fixed),
                  pl.BlockSpec((tm, HEAD_DIM), pos),
                  pl.BlockSpec((tm, HEAD_DIM), pos)],
        out_specs=[pl.BlockSpec((tm, d), row),
                   pl.BlockSpec((1, N_HEADS, HEAD_DIM, tm), head_t),
                   pl.BlockSpec((1, N_HEADS, 1, tm), head_t),
                   pl.BlockSpec((1, N_KV_HEADS, tm, HEAD_DIM), head),
                   pl.BlockSpec((1, N_KV_HEADS, HEAD_DIM + V_PAD_ROWS, tm), head_t),
                   pl.BlockSpec((tm, N_HEADS * HEAD_DIM), row)],
        out_shape=[jax.ShapeDtypeStruct((n, d), F32),
                   jax.ShapeDtypeStruct((bsz, N_HEADS, HEAD_DIM, seq), BF16),
                   jax.ShapeDtypeStruct((bsz, N_HEADS, 1, seq), F32),
                   jax.ShapeDtypeStruct((bsz, N_KV_HEADS, seq, HEAD_DIM), BF16),
                   jax.ShapeDtypeStruct((bsz, N_KV_HEADS, HEAD_DIM + V_PAD_ROWS, seq), BF16),
                   jax.ShapeDtypeStruct((n, N_HEADS * HEAD_DIM), BF16)],
        scratch_shapes=[pltpu.VMEM(w_out.shape, BF16), pltpu.VMEM(w_in.shape, BF16)],
        compiler_params=_params(2),
        name="even_out_odd_in",
    )(x2, cx, cx, cx, ga, f2, gb, conv_w, w_out, g_odd, w_in, q_gain, k_gain, cos, sin)


def _attn_kernel(q_ref, qn_ref, k_ref, v_ref, o_ref, kmax_ref, acc_ref, *, tk):
    _, grp, hd, tq = q_ref.shape
    seq = k_ref.shape[2]
    n_chunks = seq // tk
    v_rows = v_ref.shape[2]

    @pl.when(pl.program_id(2) == 0)
    def _key_norm_bound():
        def body(j, best):
            kc = k_ref[0, 0, pl.ds(pl.multiple_of(j * tk, tk), tk), :].astype(F32)
            return jnp.maximum(best, jnp.sum(kc * kc, axis=1, keepdims=True))
        best = lax.fori_loop(0, n_chunks, body, jnp.zeros((tk, 1), F32))
        kmax_ref[0] = jnp.sqrt(jnp.max(best))

    shift = [qn_ref[0, g] * (kmax_ref[0] * SHIFT_MARGIN) for g in range(grp)]
    span = functools.reduce(jnp.maximum, [jnp.max(c) for c in shift])

    def chunk(j):
        start = pl.multiple_of(j * tk, tk)
        return k_ref[0, 0, pl.ds(start, tk), :], v_ref[0, 0, :, pl.ds(start, tk)]

    bounded = span <= SCORE_SPAN_LIMIT

    @pl.when(bounded)
    def _bounded_shift():
        n_parts = max(1, grp * tq // ATTN_QUERY_LANES)
        hp = grp // n_parts
        q_parts = [jnp.concatenate([q_ref[0, g] for g in range(s * hp, (s + 1) * hp)], axis=1)
                   for s in range(n_parts)]
        shift_parts = [jnp.concatenate(shift[s * hp:(s + 1) * hp], axis=1) for s in range(n_parts)]

        def body(j, accs):
            kc, vc = chunk(j)
            out = []
            for s in range(n_parts):
                sc = jnp.dot(kc, q_parts[s], preferred_element_type=F32)
                p = jnp.exp2(sc - shift_parts[s]).astype(BF16)
                out.append(accs[s] + jnp.dot(vc, p, preferred_element_type=F32))
            return tuple(out)
        init = tuple(jnp.zeros((v_rows, hp * tq), F32) for _ in range(n_parts))
        accs = lax.fori_loop(0, n_chunks, body, init, unroll=ATTN_UNROLL)
        for g in range(grp):
            acc_ref[g] = accs[g // hp][:, (g % hp) * tq:(g % hp + 1) * tq]

    @pl.when(jnp.logical_not(bounded))
    def _online_max():
        for g in range(grp):
            def body(j, carry):
                m, acc = carry
                kc, vc = chunk(j)
                s = jnp.dot(kc, q_ref[0, g], preferred_element_type=F32)
                m_new = jnp.maximum(m, jnp.max(s, axis=0, keepdims=True))
                p = jnp.exp2(s - m_new).astype(BF16)
                acc = jnp.exp2(m - m_new) * acc + jnp.dot(vc, p, preferred_element_type=F32)
                return m_new, acc
            init = (jnp.full((1, tq), -jnp.inf, F32), jnp.zeros((v_rows, tq), F32))
            _, acc = lax.fori_loop(0, n_chunks, body, init)
            acc_ref[g] = acc

    for g in range(grp):
        acc = acc_ref[g]
        o_t = acc[:hd] / acc[hd:hd + 1]
        o_ref[0, :, g * hd:(g + 1) * hd] = o_t.T.astype(o_ref.dtype)


def _attention(qt, qn, k, vt, tq, tk):
    bsz, nh, hd, seq = qt.shape
    nkv = k.shape[1]
    grp = nh // nkv
    v_rows = vt.shape[2]
    return pl.pallas_call(
        functools.partial(_attn_kernel, tk=tk),
        grid=(bsz, nkv, seq // tq),
        in_specs=[pl.BlockSpec((1, grp, hd, tq), lambda b, h, i: (b, h, 0, i)),
                  pl.BlockSpec((1, grp, 1, tq), lambda b, h, i: (b, h, 0, i)),
                  pl.BlockSpec((1, 1, seq, hd), lambda b, h, i: (b, h, 0, 0)),
                  pl.BlockSpec((1, 1, v_rows, seq), lambda b, h, i: (b, h, 0, 0))],
        out_specs=pl.BlockSpec((1, tq, grp * hd), lambda b, h, i: (b, i, h)),
        out_shape=jax.ShapeDtypeStruct((bsz, seq, nh * hd), BF16),
        scratch_shapes=[pltpu.SMEM((1,), F32), pltpu.VMEM((grp, v_rows, tq), F32)],
        compiler_params=_params(3),
        name="gqa_attention",
    )(qt, qn, k, vt)


def _odd_out_kernel(x1_ref, o_ref, sz_ref, w_ref, g_ref, out_ref, wb_ref):
    _cast_once(pl.program_id(0) == 0, w_ref, wb_ref)
    y = (o_ref[...].astype(F32) * sz_ref[...].astype(F32)).astype(BF16)
    x2 = x1_ref[...] + jnp.dot(y, wb_ref[...], preferred_element_type=F32)
    out_ref[...] = _rms_norm(x2, g_ref[...])


def _odd_out(x1, o2, sz, w, g, tm):
    n, d = x1.shape
    row = lambda i: (i, 0)
    fixed = lambda i: (0, 0)
    return pl.pallas_call(
        _odd_out_kernel,
        grid=(n // tm,),
        in_specs=[pl.BlockSpec((tm, d), row),
                  pl.BlockSpec((tm, o2.shape[1]), row),
                  pl.BlockSpec((tm, sz.shape[1]), row),
                  _resident(w.shape),
                  pl.BlockSpec((1, d), fixed)],
        out_specs=pl.BlockSpec((tm, d), row),
        out_shape=jax.ShapeDtypeStruct((n, d), F32),
        scratch_shapes=[pltpu.VMEM(w.shape, BF16)],
        compiler_params=_params(1),
        name="odd_out",
    )(x1, o2, sz, w, g)


def kernel(x, norm_even, w_in_even, conv_w, w_out_even, norm_odd, w_in_odd, q_gain, k_gain, w_out_odd, final_norm):
    bsz, seq, d = x.shape
    assert norm_even.shape[0] == 1 and norm_odd.shape[0] == 1, "one even and one odd layer"
    assert seq % FFT_FAST == 0 and seq % GRID_W == 0
    n = bsz * seq
    slow = seq // FFT_FAST
    tm = min(512, seq)
    x2 = x.reshape(n, d)

    cx, ga, bu, gb = _even_in(x2, norm_even[0][None, :], w_in_even[0], min(1024, seq))
    w1, t2, wc = _dft_tables(seq)
    y = _dft1(w1, bu.reshape(bsz, slow, FFT_FAST, B_WIDTH), nb=DFT_ROWS_PER_STEP)
    scale = 1.0 / math.sqrt(seq * B_GROUP_DIM)
    f = _dft2(y.reshape(bsz, 2, seq, B_WIDTH), t2, wc, cc=min(DFT_ROWS_PER_STEP, slow), scale=scale)
    f2 = f.reshape(n, B_WIDTH)

    cos, sin = _rope_tables(seq)
    q_scale = HEAD_DIM ** -0.5 * math.log2(math.e)
    x1, qt, qn, k, vt, sz = _mid(x2, cx, ga, f2, gb, conv_w[0], w_out_even[0],
                                 norm_odd[0][None, :], w_in_odd[0],
                                 q_gain[0][None, :], k_gain[0][None, :], cos, sin, bsz, seq, tm, q_scale)

    o = _attention(qt, qn, k, vt, tq=min(512, seq), tk=min(512, seq))
    out = _odd_out(x1, o.reshape(n, N_HEADS * HEAD_DIM), sz, w_out_odd[0], final_norm[None, :], min(1024, seq))
    return out.reshape(bsz, seq, d)
```

```python
import functools
import math

import numpy as np
import jax
import jax.numpy as jnp
from jax import lax
from jax.experimental import pallas as pl
from jax.experimental.pallas import tpu as pltpu

F32 = jnp.float32
BF16 = jnp.bfloat16

EPS = 1e-6
GRID_W = 64
ROPE_THETA = 10000.0
CONV_WIDTH = 3
HEAD_DIM = 128
N_HEADS = 8
N_KV_HEADS = 2
KV_GROUP = N_HEADS // N_KV_HEADS
A_WIDTH = 512
B_WIDTH = 512
B_GROUPS = 4
B_GROUP_DIM = 128
FFT_FAST = 128
DFT_ROWS_PER_STEP = 16
V_PAD_ROWS = 16
SCORE_SPAN_LIMIT = 60.0
SHIFT_MARGIN = 1.01
MID_SUBTILES = 2

VMEM_LIMIT_BYTES = 56 * 1024 * 1024


def _silu(z):
    return z / (1.0 + jnp.exp(-z))


def _rms_norm(x, g):
    return x * lax.rsqrt(jnp.mean(x * x, axis=-1, keepdims=True) + EPS) * g


def _resident(shape):
    zeros = (0,) * len(shape)
    return pl.BlockSpec(shape, lambda *_: zeros, pipeline_mode=pl.Buffered(1))


def _cast_once(first_step, src_ref, dst_ref):
    @pl.when(first_step)
    def _():
        dst_ref[...] = src_ref[...].astype(dst_ref.dtype)


def _params(n_grid_dims):
    return pltpu.CompilerParams(
        dimension_semantics=("arbitrary",) * n_grid_dims,
        vmem_limit_bytes=VMEM_LIMIT_BYTES)


def _even_in_kernel(x_ref, g_ref, w_ref, cx_ref, ga_ref, bu_ref, gb_ref, wb_ref):
    _cast_once(pl.program_id(0) == 0, w_ref, wb_ref)
    h = _rms_norm(x_ref[...], g_ref[...]).astype(BF16)
    p = jnp.dot(h, wb_ref[...], preferred_element_type=F32)
    a_x = p[:, 0 * A_WIDTH:1 * A_WIDTH]
    a_b = p[:, 1 * A_WIDTH:2 * A_WIDTH]
    a_c = p[:, 2 * A_WIDTH:3 * A_WIDTH]
    a_z = p[:, 3 * A_WIDTH:4 * A_WIDTH]
    b_u = p[:, 4 * A_WIDTH:4 * A_WIDTH + B_WIDTH]
    b_z = p[:, 4 * A_WIDTH + B_WIDTH:]
    cx_ref[...] = (a_c * a_x).astype(cx_ref.dtype)
    ga_ref[...] = (a_b * _silu(a_z)).astype(ga_ref.dtype)
    bu_ref[...] = b_u.astype(bu_ref.dtype)
    gb_ref[...] = _silu(b_z).astype(gb_ref.dtype)


def _even_in(x2, g, w, tm):
    n, d = x2.shape
    e_in = w.shape[1]
    out = jax.ShapeDtypeStruct((n, A_WIDTH), BF16)
    row = lambda i: (i, 0)
    fixed = lambda i: (0, 0)
    return pl.pallas_call(
        _even_in_kernel,
        grid=(n // tm,),
        in_specs=[pl.BlockSpec((tm, d), row),
                  pl.BlockSpec((1, d), fixed),
                  _resident((d, e_in))],
        out_specs=[pl.BlockSpec((tm, A_WIDTH), row)] * 4,
        out_shape=[out] * 4,
        scratch_shapes=[pltpu.VMEM((d, e_in), BF16)],
        compiler_params=_params(1),
        name="even_in",
    )(x2, g, w)


def _dft_tables(seq):
    slow = seq // FFT_FAST
    a = np.arange(slow)
    th = 2.0 * np.pi * np.outer(a, a) / slow
    w1 = np.concatenate([np.cos(th), -np.sin(th)], axis=0)
    b = np.arange(FFT_FAST)
    k = a[:, None, None] + slow * b[None, :, None]
    th2 = 2.0 * np.pi * (k * b[None, None, :]) / seq
    cs, sn = np.cos(th2), np.sin(th2)
    t2 = np.concatenate([np.concatenate([cs, sn], axis=2),
                         np.concatenate([-sn, cs], axis=2)], axis=1)
    ch = np.arange(B_GROUP_DIM)
    thc = 2.0 * np.pi * np.outer(ch, ch) / B_GROUP_DIM
    wc = np.concatenate([np.cos(thc), np.sin(thc)], axis=0)
    return tuple(jnp.asarray(t, dtype=F32) for t in (w1, t2, wc))


def _dft1_kernel(w_ref, u_ref, y_ref):
    _, slow, nb, width = u_ref.shape
    ut = jnp.transpose(u_ref[0], (1, 0, 2))
    x = jnp.concatenate([ut[b] for b in range(nb)], axis=1)
    y = jnp.dot(w_ref[...].astype(BF16), x, preferred_element_type=F32)
    yb = y.astype(y_ref.dtype)
    for part in range(2):
        rows = slice(part * slow, (part + 1) * slow)
        by_b = jnp.stack([yb[rows, b * width:(b + 1) * width] for b in range(nb)], axis=0)
        y_ref[0, part] = jnp.transpose(by_b, (1, 0, 2))


def _dft1(w1, u4, nb):
    bsz, slow, fast, width = u4.shape
    return pl.pallas_call(
        _dft1_kernel,
        grid=(bsz, fast // nb),
        in_specs=[pl.BlockSpec((2 * slow, slow), lambda b, j: (0, 0)),
                  pl.BlockSpec((1, slow, nb, width), lambda b, j: (b, 0, j, 0))],
        out_specs=pl.BlockSpec((1, 2, slow, nb, width), lambda b, j: (b, 0, 0, j, 0)),
        out_shape=jax.ShapeDtypeStruct((bsz, 2, slow, fast, width), BF16),
        compiler_params=_params(2),
        name="dft_stage1",
    )(w1, u4)


def _dft2_kernel(y_ref, t_ref, wc_ref, f_ref, *, cc, scale):
    gd = B_GROUP_DIM
    by_c = []
    for ci in range(cc):
        rows = slice(ci * FFT_FAST, (ci + 1) * FFT_FAST)
        ys = jnp.concatenate([y_ref[0, 0, rows, :], y_ref[0, 1, rows, :]], axis=0)
        p = jnp.dot(t_ref[ci].astype(BF16), ys, preferred_element_type=F32).astype(BF16)
        lhs = jnp.concatenate(
            [jnp.concatenate([p[:FFT_FAST, g * gd:(g + 1) * gd], p[FFT_FAST:, g * gd:(g + 1) * gd]], axis=1)
             for g in range(B_GROUPS)], axis=0)
        f = jnp.dot(lhs, wc_ref[...].astype(BF16), preferred_element_type=F32) * scale
        by_c.append(jnp.concatenate([f[g * FFT_FAST:(g + 1) * FFT_FAST] for g in range(B_GROUPS)],
                                    axis=1).astype(f_ref.dtype))
    f_ref[0] = jnp.transpose(jnp.stack(by_c, axis=0), (1, 0, 2))


def _dft2(y4, t2, wc, cc, scale):
    bsz, _, seq, width = y4.shape
    slow = seq // FFT_FAST
    return pl.pallas_call(
        functools.partial(_dft2_kernel, cc=cc, scale=scale),
        grid=(slow // cc, bsz),
        in_specs=[pl.BlockSpec((1, 2, cc * FFT_FAST, width), lambda j, b: (b, 0, j, 0)),
                  pl.BlockSpec((cc, 2 * FFT_FAST, 2 * FFT_FAST), lambda j, b: (j, 0, 0)),
                  pl.BlockSpec(wc.shape, lambda j, b: (0, 0))],
        out_specs=pl.BlockSpec((1, FFT_FAST, cc, width), lambda j, b: (b, 0, j, 0)),
        out_shape=jax.ShapeDtypeStruct((bsz, FFT_FAST, slow, width), BF16),
        compiler_params=_params(2),
        name="dft_stage2",
    )(y4, t2, wc)


def _rope_tables(seq):
    t = np.arange(seq)
    row = (t // GRID_W).astype(np.float64)
    col = (t % GRID_W).astype(np.float64)
    n_pair = HEAD_DIM // 4
    inv = ROPE_THETA ** (-np.arange(n_pair, dtype=np.float64) / n_pair)
    ang = np.concatenate([row[:, None] * inv, col[:, None] * inv], axis=-1)
    cos = np.repeat(np.cos(ang), 2, axis=1)
    sin = np.repeat(np.sin(ang), 2, axis=1)
    sign = np.tile(np.array([-1.0, 1.0]), HEAD_DIM // 2)
    return jnp.asarray(cos, dtype=F32), jnp.asarray(sin * sign, dtype=F32)


def _norm_rope(xh, gain, cos, sin_signed, even_lane):
    xn = _rms_norm(xh, gain)
    partner = jnp.where(even_lane, pltpu.roll(xn, HEAD_DIM - 1, axis=1), pltpu.roll(xn, 1, axis=1))
    return xn * cos + partner * sin_signed


def _mid_kernel(x_ref, cx_ref, cxp_ref, cxn_ref, ga_ref, f_ref, gb_ref, cw_ref, wo_ref,
                g_ref, wi_ref, qg_ref, kg_ref, cos_ref, sin_ref,
                x1_ref, q_ref, qn_ref, k_ref, v_ref, z_ref, wob_ref, wib_ref, *, q_scale):
    first_step = (pl.program_id(0) == 0) & (pl.program_id(1) == 0)
    _cast_once(first_step, wo_ref, wob_ref)
    _cast_once(first_step, wi_ref, wib_ref)
    i = pl.program_id(1)
    n_i = pl.num_programs(1)
    tm = x_ref.shape[0]
    cx = cx_ref[...].astype(F32)
    prev_row = jnp.where(i > 0, cxp_ref[7:8, :].astype(F32), 0.0)
    next_row = jnp.where(i < n_i - 1, cxn_ref[0:1, :].astype(F32), 0.0)
    r = lax.broadcasted_iota(jnp.int32, cx.shape, 0)
    up = jnp.where(r == 0, prev_row, pltpu.roll(cx, 1, axis=0))
    dn = jnp.where(r == tm - 1, next_row, pltpu.roll(cx, tm - 1, axis=0))
    conv = up * cw_ref[0:1, :] + cx * cw_ref[1:2, :] + dn * cw_ref[2:3, :]
    y_a = ga_ref[...].astype(F32) * conv
    y_b = f_ref[...].astype(F32) * gb_ref[...].astype(F32)
    y = jnp.concatenate([y_a, y_b], axis=1).astype(BF16)
    qw = N_HEADS * HEAD_DIM
    kw = N_KV_HEADS * HEAD_DIM
    ts = tm // MID_SUBTILES
    even_lane = (lax.broadcasted_iota(jnp.int32, (ts, HEAD_DIM), 1) & 1) == 0
    ones_pad = (lax.broadcasted_iota(jnp.int32, (V_PAD_ROWS, ts), 0) == 0).astype(v_ref.dtype)
    for sub in range(MID_SUBTILES):
        rows = slice(sub * ts, (sub + 1) * ts)
        x1 = x_ref[rows, :] + jnp.dot(y[rows], wob_ref[...], preferred_element_type=F32)
        x1_ref[rows, :] = x1
        h = _rms_norm(x1, g_ref[...]).astype(BF16)
        p = jnp.dot(h, wib_ref[...], preferred_element_type=F32)
        cos = cos_ref[rows, :]
        sin = sin_ref[rows, :]
        for hh in range(N_HEADS):
            qh = _norm_rope(p[:, hh * HEAD_DIM:(hh + 1) * HEAD_DIM], qg_ref[...], cos, sin, even_lane)
            qt = (qh * q_scale).T.astype(q_ref.dtype)
            q_ref[0, hh, :, rows] = qt
            qf = qt.astype(F32)
            qn_ref[0, hh, :, rows] = jnp.sqrt(jnp.sum(qf * qf, axis=0, keepdims=True))
        for hh in range(N_KV_HEADS):
            kh = _norm_rope(p[:, qw + hh * HEAD_DIM:qw + (hh + 1) * HEAD_DIM], kg_ref[...], cos, sin, even_lane)
            k_ref[0, hh, rows, :] = kh.astype(k_ref.dtype)
            vt = p[:, qw + kw + hh * HEAD_DIM:qw + kw + (hh + 1) * HEAD_DIM].T.astype(v_ref.dtype)
            v_ref[0, hh, :, rows] = jnp.concatenate([vt, ones_pad], axis=0)
        z_ref[rows, :] = p[:, qw + 2 * kw:].astype(z_ref.dtype)


def _mid(x2, cx, ga, f2, gb, conv_w, w_out, g_odd, w_in, q_gain, k_gain, cos, sin, bsz, seq, tm, q_scale):
    n, d = x2.shape
    nt = seq // tm
    halo = 8
    tpb = tm // halo
    last_blk = n // halo - 1
    row = lambda b, i: (b * nt + i, 0)
    fixed = lambda b, i: (0, 0)
    prev = lambda b, i: (jnp.maximum((b * nt + i) * tpb - 1, 0), 0)
    nxt = lambda b, i: (jnp.minimum((b * nt + i + 1) * tpb, last_blk), 0)
    pos = lambda b, i: (i, 0)
    head = lambda b, i: (b, 0, i, 0)
    head_t = lambda b, i: (b, 0, 0, i)
    half = pl.BlockSpec((tm, A_WIDTH), row)
    return pl.pallas_call(
        functools.partial(_mid_kernel, q_scale=q_scale),
        grid=(bsz, nt),
        in_specs=[pl.BlockSpec((tm, d), row),
                  half,
                  pl.BlockSpec((halo, A_WIDTH), prev),
                  pl.BlockSpec((halo, A_WIDTH), nxt),
                  half, half, half,
                  pl.BlockSpec(conv_w.shape, fixed),
                  _resident(w_out.shape),
                  pl.BlockSpec((1, d), fixed),
                  _resident(w_in.shape),
                  pl.BlockSpec((1, HEAD_DIM), fixed),
                  pl.BlockSpec((1, HEAD_DIM), fixed),
                  pl.BlockSpec((tm, HEAD_DIM), pos),
                  pl.BlockSpec((tm, HEAD_DIM), pos)],
        out_specs=[pl.BlockSpec((tm, d), row),
                   pl.BlockSpec((1, N_HEADS, HEAD_DIM, tm), head_t),
                   pl.BlockSpec((1, N_HEADS, 1, tm), head_t),
                   pl.BlockSpec((1, N_KV_HEADS, tm, HEAD_DIM), head),
                   pl.BlockSpec((1, N_KV_HEADS, HEAD_DIM + V_PAD_ROWS, tm), head_t),
                   pl.BlockSpec((tm, N_HEADS * HEAD_DIM), row)],
        out_shape=[jax.ShapeDtypeStruct((n, d), F32),
                   jax.ShapeDtypeStruct((bsz, N_HEADS, HEAD_DIM, seq), BF16),
                   jax.ShapeDtypeStruct((bsz, N_HEADS, 1, seq), F32),
                   jax.ShapeDtypeStruct((bsz, N_KV_HEADS, seq, HEAD_DIM), BF16),
                   jax.ShapeDtypeStruct((bsz, N_KV_HEADS, HEAD_DIM + V_PAD_ROWS, seq), BF16),
                   jax.ShapeDtypeStruct((n, N_HEADS * HEAD_DIM), BF16)],
        scratch_shapes=[pltpu.VMEM(w_out.shape, BF16), pltpu.VMEM(w_in.shape, BF16)],
        compiler_params=_params(2),
        name="even_out_odd_in",
    )(x2, cx, cx, cx, ga, f2, gb, conv_w, w_out, g_odd, w_in, q_gain, k_gain, cos, sin)


def _attn_kernel(q_ref, qn_ref, qn_all_ref, k_ref, v_ref, o_ref, bound_ref, acc_ref, *, tk):
    _, grp, hd, tq = q_ref.shape
    seq = k_ref.shape[2]
    n_chunks = seq // tk
    v_rows = v_ref.shape[2]

    @pl.when(pl.program_id(2) == 0)
    def _norm_bounds():
        def body(j, best):
            kc = k_ref[0, 0, pl.ds(pl.multiple_of(j * tk, tk), tk), :].astype(F32)
            return jnp.maximum(best, jnp.sum(kc * kc, axis=1, keepdims=True))
        best = lax.fori_loop(0, n_chunks, body, jnp.zeros((tk, 1), F32))
        key_bound = jnp.sqrt(jnp.max(best)) * SHIFT_MARGIN
        bound_ref[0] = key_bound
        bound_ref[1] = jnp.max(qn_all_ref[0]) * key_bound

    def chunk(j):
        start = pl.multiple_of(j * tk, tk)
        return k_ref[0, 0, pl.ds(start, tk), :], v_ref[0, 0, :, pl.ds(start, tk)]

    bounded = bound_ref[1] <= SCORE_SPAN_LIMIT

    @pl.when(bounded)
    def _bounded_shift():
        q_all = jnp.concatenate([q_ref[0, g] for g in range(grp)], axis=1)
        shift = jnp.concatenate([qn_ref[0, g] for g in range(grp)], axis=1) * bound_ref[0]

        def body(j, acc):
            kc, vc = chunk(j)
            s = jnp.dot(kc, q_all, preferred_element_type=F32)
            p = jnp.exp2(s - shift).astype(BF16)
            return acc + jnp.dot(vc, p, preferred_element_type=F32)
        acc = lax.fori_loop(0, n_chunks, body, jnp.zeros((v_rows, grp * tq), F32), unroll=True)
        for g in range(grp):
            acc_ref[g] = acc[:, g * tq:(g + 1) * tq]

    @pl.when(jnp.logical_not(bounded))
    def _online_max():
        for g in range(grp):
            def body(j, carry):
                m, acc = carry
                kc, vc = chunk(j)
                s = jnp.dot(kc, q_ref[0, g], preferred_element_type=F32)
                m_new = jnp.maximum(m, jnp.max(s, axis=0, keepdims=True))
                p = jnp.exp2(s - m_new).astype(BF16)
                acc = jnp.exp2(m - m_new) * acc + jnp.dot(vc, p, preferred_element_type=F32)
                return m_new, acc
            init = (jnp.full((1, tq), -jnp.inf, F32), jnp.zeros((v_rows, tq), F32))
            _, acc = lax.fori_loop(0, n_chunks, body, init)
            acc_ref[g] = acc

    for g in range(grp):
        acc = acc_ref[g]
        o_t = acc[:hd] / acc[hd:hd + 1]
        o_ref[0, :, g * hd:(g + 1) * hd] = o_t.T.astype(o_ref.dtype)


def _attention(qt, qn, k, vt, tq, tk):
    bsz, nh, hd, seq = qt.shape
    nkv = k.shape[1]
    grp = nh // nkv
    v_rows = vt.shape[2]
    return pl.pallas_call(
        functools.partial(_attn_kernel, tk=tk),
        grid=(bsz, nkv, seq // tq),
        in_specs=[pl.BlockSpec((1, grp, hd, tq), lambda b, h, i: (b, h, 0, i)),
                  pl.BlockSpec((1, grp, 1, tq), lambda b, h, i: (b, h, 0, i)),
                  pl.BlockSpec((1, grp, 1, seq), lambda b, h, i: (b, h, 0, 0)),
                  pl.BlockSpec((1, 1, seq, hd), lambda b, h, i: (b, h, 0, 0)),
                  pl.BlockSpec((1, 1, v_rows, seq), lambda b, h, i: (b, h, 0, 0))],
        out_specs=pl.BlockSpec((1, tq, grp * hd), lambda b, h, i: (b, i, h)),
        out_shape=jax.ShapeDtypeStruct((bsz, seq, nh * hd), BF16),
        scratch_shapes=[pltpu.SMEM((2,), F32), pltpu.VMEM((grp, v_rows, tq), F32)],
        compiler_params=_params(3),
        name="gqa_attention",
    )(qt, qn, qn, k, vt)


def _odd_out_kernel(x1_ref, o_ref, z_ref, w_ref, g_ref, out_ref, wb_ref):
    _cast_once(pl.program_id(0) == 0, w_ref, wb_ref)
    y = (o_ref[...].astype(F32) * _silu(z_ref[...].astype(F32))).astype(BF16)
    x2 = x1_ref[...] + jnp.dot(y, wb_ref[...], preferred_element_type=F32)
    out_ref[...] = _rms_norm(x2, g_ref[...])


def _odd_out(x1, o2, z, w, g, tm):
    n, d = x1.shape
    row = lambda i: (i, 0)
    fixed = lambda i: (0, 0)
    return pl.pallas_call(
        _odd_out_kernel,
        grid=(n // tm,),
        in_specs=[pl.BlockSpec((tm, d), row),
                  pl.BlockSpec((tm, o2.shape[1]), row),
                  pl.BlockSpec((tm, z.shape[1]), row),
                  _resident(w.shape),
                  pl.BlockSpec((1, d), fixed)],
        out_specs=pl.BlockSpec((tm, d), row),
        out_shape=jax.ShapeDtypeStruct((n, d), F32),
        scratch_shapes=[pltpu.VMEM(w.shape, BF16)],
        compiler_params=_params(1),
        name="odd_out",
    )(x1, o2, z, w, g)


def kernel(x, norm_even, w_in_even, conv_w, w_out_even, norm_odd, w_in_odd, q_gain, k_gain, w_out_odd, final_norm):
    bsz, seq, d = x.shape
    assert norm_even.shape[0] == 1 and norm_odd.shape[0] == 1, "one even and one odd layer"
    assert seq % FFT_FAST == 0 and seq % GRID_W == 0
    n = bsz * seq
    slow = seq // FFT_FAST
    tm = min(512, seq)
    x2 = x.reshape(n, d)

    cx, ga, bu, gb = _even_in(x2, norm_even[0][None, :], w_in_even[0], min(1024, seq))
    w1, t2, wc = _dft_tables(seq)
    y = _dft1(w1, bu.reshape(bsz, slow, FFT_FAST, B_WIDTH), nb=DFT_ROWS_PER_STEP)
    scale = 1.0 / math.sqrt(seq * B_GROUP_DIM)
    f = _dft2(y.reshape(bsz, 2, seq, B_WIDTH), t2, wc, cc=min(DFT_ROWS_PER_STEP, slow), scale=scale)
    f2 = f.reshape(n, B_WIDTH)

    cos, sin = _rope_tables(seq)
    q_scale = HEAD_DIM ** -0.5 * math.log2(math.e)
    x1, qt, qn, k, vt, z = _mid(x2, cx, ga, f2, gb, conv_w[0], w_out_even[0],
                                 norm_odd[0][None, :], w_in_odd[0],
                                 q_gain[0][None, :], k_gain[0][None, :], cos, sin, bsz, seq, tm, q_scale)

    o = _attention(qt, qn, k, vt, tq=min(512, seq), tk=min(512, seq))
    out = _odd_out(x1, o.reshape(n, N_HEADS * HEAD_DIM), z, w_out_odd[0], final_norm[None, :], min(1024, seq))
    return out.reshape(bsz, seq, d)
```

```python
import functools
import math
from typing import NamedTuple

import numpy as np
import jax
import jax.numpy as jnp
from jax import lax
from jax.experimental import pallas as pl
from jax.experimental.pallas import tpu as pltpu

F32 = jnp.float32
BF16 = jnp.bfloat16

EPS = 1e-6
GRID_W = 64
ROPE_THETA = 10000.0
CONV_WIDTH = 3
HEAD_DIM = 128
N_HEADS = 8
N_KV_HEADS = 2
KV_GROUP = N_HEADS // N_KV_HEADS
A_WIDTH = 512
B_WIDTH = 512
B_GROUPS = 4
B_GROUP_DIM = 128
FFT_FAST = 128
DFT_ROWS_PER_STEP = 16
V_PAD_ROWS = 16
SCORE_SPAN_LIMIT = 60.0
SHIFT_MARGIN = 1.01
MID_SUBTILES = 2

VMEM_LIMIT_BYTES = 56 * 1024 * 1024


def _silu(z):
    return z / (1.0 + jnp.exp(-z))


def _rms_norm(x, g):
    return x * lax.rsqrt(jnp.mean(x * x, axis=-1, keepdims=True) + EPS) * g


def _resident(shape):
    zeros = (0,) * len(shape)
    return pl.BlockSpec(shape, lambda *_: zeros, pipeline_mode=pl.Buffered(1))


def _cast_once(first_step, src_ref, dst_ref):
    @pl.when(first_step)
    def _():
        dst_ref[...] = src_ref[...].astype(dst_ref.dtype)


def _params(n_grid_dims):
    return pltpu.CompilerParams(
        dimension_semantics=("arbitrary",) * n_grid_dims,
        vmem_limit_bytes=VMEM_LIMIT_BYTES)


def _even_in_kernel(x_ref, g_ref, w_ref, cx_ref, ga_ref, bu_ref, gb_ref, wb_ref):
    _cast_once(pl.program_id(0) == 0, w_ref, wb_ref)
    h = _rms_norm(x_ref[...], g_ref[...]).astype(BF16)
    p = jnp.dot(h, wb_ref[...], preferred_element_type=F32)
    a_x = p[:, 0 * A_WIDTH:1 * A_WIDTH]
    a_b = p[:, 1 * A_WIDTH:2 * A_WIDTH]
    a_c = p[:, 2 * A_WIDTH:3 * A_WIDTH]
    a_z = p[:, 3 * A_WIDTH:4 * A_WIDTH]
    b_u = p[:, 4 * A_WIDTH:4 * A_WIDTH + B_WIDTH]
    b_z = p[:, 4 * A_WIDTH + B_WIDTH:]
    cx_ref[...] = (a_c * a_x).astype(cx_ref.dtype)
    ga_ref[...] = (a_b * _silu(a_z)).astype(ga_ref.dtype)
    bu_ref[...] = b_u.astype(bu_ref.dtype)
    gb_ref[...] = _silu(b_z).astype(gb_ref.dtype)


def _even_in(x2, g, w, tm):
    n, d = x2.shape
    e_in = w.shape[1]
    out = jax.ShapeDtypeStruct((n, A_WIDTH), BF16)
    row = lambda i: (i, 0)
    fixed = lambda i: (0, 0)
    return pl.pallas_call(
        _even_in_kernel,
        grid=(n // tm,),
        in_specs=[pl.BlockSpec((tm, d), row),
                  pl.BlockSpec((1, d), fixed),
                  _resident((d, e_in))],
        out_specs=[pl.BlockSpec((tm, A_WIDTH), row)] * 4,
        out_shape=[out] * 4,
        scratch_shapes=[pltpu.VMEM((d, e_in), BF16)],
        compiler_params=_params(1),
        name="even_in",
    )(x2, g, w)


def _dft_tables(seq):
    slow = seq // FFT_FAST
    a = np.arange(slow)
    th = 2.0 * np.pi * np.outer(a, a) / slow
    w1 = np.concatenate([np.cos(th), -np.sin(th)], axis=0)
    b = np.arange(FFT_FAST)
    k = a[:, None, None] + slow * b[None, :, None]
    th2 = 2.0 * np.pi * (k * b[None, None, :]) / seq
    cs, sn = np.cos(th2), np.sin(th2)
    t2 = np.concatenate([np.concatenate([cs, sn], axis=2),
                         np.concatenate([-sn, cs], axis=2)], axis=1)
    ch = np.arange(B_GROUP_DIM)
    thc = 2.0 * np.pi * np.outer(ch, ch) / B_GROUP_DIM
    wc = np.concatenate([np.cos(thc), np.sin(thc)], axis=0)
    return tuple(jnp.asarray(t, dtype=F32) for t in (w1, t2, wc))


def _dft1_kernel(w_ref, u_ref, y_ref):
    _, slow, nb, width = u_ref.shape
    ut = jnp.transpose(u_ref[0], (1, 0, 2))
    x = jnp.concatenate([ut[b] for b in range(nb)], axis=1)
    y = jnp.dot(w_ref[...].astype(BF16), x, preferred_element_type=F32)
    yb = y.astype(y_ref.dtype)
    for part in range(2):
        rows = slice(part * slow, (part + 1) * slow)
        by_b = jnp.stack([yb[rows, b * width:(b + 1) * width] for b in range(nb)], axis=0)
        y_ref[0, part] = jnp.transpose(by_b, (1, 0, 2))


def _dft1(w1, u4, nb):
    bsz, slow, fast, width = u4.shape
    return pl.pallas_call(
        _dft1_kernel,
        grid=(bsz, fast // nb),
        in_specs=[pl.BlockSpec((2 * slow, slow), lambda b, j: (0, 0)),
                  pl.BlockSpec((1, slow, nb, width), lambda b, j: (b, 0, j, 0))],
        out_specs=pl.BlockSpec((1, 2, slow, nb, width), lambda b, j: (b, 0, 0, j, 0)),
        out_shape=jax.ShapeDtypeStruct((bsz, 2, slow, fast, width), BF16),
        compiler_params=_params(2),
        name="dft_stage1",
    )(w1, u4)


def _dft2_kernel(y_ref, t_ref, wc_ref, f_ref, *, cc, scale):
    gd = B_GROUP_DIM
    by_c = []
    for ci in range(cc):
        rows = slice(ci * FFT_FAST, (ci + 1) * FFT_FAST)
        ys = jnp.concatenate([y_ref[0, 0, rows, :], y_ref[0, 1, rows, :]], axis=0)
        p = jnp.dot(t_ref[ci].astype(BF16), ys, preferred_element_type=F32).astype(BF16)
        lhs = jnp.concatenate(
            [jnp.concatenate([p[:FFT_FAST, g * gd:(g + 1) * gd], p[FFT_FAST:, g * gd:(g + 1) * gd]], axis=1)
             for g in range(B_GROUPS)], axis=0)
        f = jnp.dot(lhs, wc_ref[...].astype(BF16), preferred_element_type=F32) * scale
        by_c.append(jnp.concatenate([f[g * FFT_FAST:(g + 1) * FFT_FAST] for g in range(B_GROUPS)],
                                    axis=1).astype(f_ref.dtype))
    f_ref[0] = jnp.transpose(jnp.stack(by_c, axis=0), (1, 0, 2))


def _dft2(y4, t2, wc, cc, scale):
    bsz, _, seq, width = y4.shape
    slow = seq // FFT_FAST
    return pl.pallas_call(
        functools.partial(_dft2_kernel, cc=cc, scale=scale),
        grid=(slow // cc, bsz),
        in_specs=[pl.BlockSpec((1, 2, cc * FFT_FAST, width), lambda j, b: (b, 0, j, 0)),
                  pl.BlockSpec((cc, 2 * FFT_FAST, 2 * FFT_FAST), lambda j, b: (j, 0, 0)),
                  pl.BlockSpec(wc.shape, lambda j, b: (0, 0))],
        out_specs=pl.BlockSpec((1, FFT_FAST, cc, width), lambda j, b: (b, 0, j, 0)),
        out_shape=jax.ShapeDtypeStruct((bsz, FFT_FAST, slow, width), BF16),
        compiler_params=_params(2),
        name="dft_stage2",
    )(y4, t2, wc)


def _rope_tables(seq):
    t = np.arange(seq)
    row = (t // GRID_W).astype(np.float64)
    col = (t % GRID_W).astype(np.float64)
    n_pair = HEAD_DIM // 4
    inv = ROPE_THETA ** (-np.arange(n_pair, dtype=np.float64) / n_pair)
    ang = np.concatenate([row[:, None] * inv, col[:, None] * inv], axis=-1)
    cos = np.repeat(np.cos(ang), 2, axis=1)
    sin = np.repeat(np.sin(ang), 2, axis=1)
    sign = np.tile(np.array([-1.0, 1.0]), HEAD_DIM // 2)
    return jnp.asarray(cos, dtype=F32), jnp.asarray(sin * sign, dtype=F32)


def _norm_rope(xh, gain, cos, sin_signed, even_lane):
    xn = _rms_norm(xh, gain)
    partner = jnp.where(even_lane, pltpu.roll(xn, HEAD_DIM - 1, axis=1), pltpu.roll(xn, 1, axis=1))
    return xn * cos + partner * sin_signed


def _mid_kernel(x_ref, cx_ref, cxp_ref, cxn_ref, ga_ref, f_ref, gb_ref, cw_ref, wo_ref,
                g_ref, wi_ref, qg_ref, kg_ref, cos_ref, sin_ref,
                x1_ref, q_ref, qn_ref, k_ref, v_ref, z_ref, wob_ref, wib_ref, *, q_scale):
    first_step = (pl.program_id(0) == 0) & (pl.program_id(1) == 0)
    _cast_once(first_step, wo_ref, wob_ref)
    _cast_once(first_step, wi_ref, wib_ref)
    i = pl.program_id(1)
    n_i = pl.num_programs(1)
    tm = x_ref.shape[0]
    cx = cx_ref[...].astype(F32)
    prev_row = jnp.where(i > 0, cxp_ref[7:8, :].astype(F32), 0.0)
    next_row = jnp.where(i < n_i - 1, cxn_ref[0:1, :].astype(F32), 0.0)
    r = lax.broadcasted_iota(jnp.int32, cx.shape, 0)
    up = jnp.where(r == 0, prev_row, pltpu.roll(cx, 1, axis=0))
    dn = jnp.where(r == tm - 1, next_row, pltpu.roll(cx, tm - 1, axis=0))
    conv = up * cw_ref[0:1, :] + cx * cw_ref[1:2, :] + dn * cw_ref[2:3, :]
    y_a = ga_ref[...].astype(F32) * conv
    y_b = f_ref[...].astype(F32) * gb_ref[...].astype(F32)
    y = jnp.concatenate([y_a, y_b], axis=1).astype(BF16)
    qw = N_HEADS * HEAD_DIM
    kw = N_KV_HEADS * HEAD_DIM
    ts = tm // MID_SUBTILES
    even_lane = (lax.broadcasted_iota(jnp.int32, (ts, HEAD_DIM), 1) & 1) == 0
    ones_pad = (lax.broadcasted_iota(jnp.int32, (V_PAD_ROWS, ts), 0) == 0).astype(v_ref.dtype)
    for sub in range(MID_SUBTILES):
        rows = slice(sub * ts, (sub + 1) * ts)
        x1 = x_ref[rows, :] + jnp.dot(y[rows], wob_ref[...], preferred_element_type=F32)
        x1_ref[rows, :] = x1
        h = _rms_norm(x1, g_ref[...]).astype(BF16)
        p = jnp.dot(h, wib_ref[...], preferred_element_type=F32)
        cos = cos_ref[rows, :]
        sin = sin_ref[rows, :]
        for hh in range(N_HEADS):
            qh = _norm_rope(p[:, hh * HEAD_DIM:(hh + 1) * HEAD_DIM], qg_ref[...], cos, sin, even_lane)
            qt = (qh * q_scale).T.astype(q_ref.dtype)
            q_ref[0, hh, :, rows] = qt
            qf = qt.astype(F32)
            qn_ref[0, hh, :, rows] = jnp.sqrt(jnp.sum(qf * qf, axis=0, keepdims=True))
        for hh in range(N_KV_HEADS):
            kh = _norm_rope(p[:, qw + hh * HEAD_DIM:qw + (hh + 1) * HEAD_DIM], kg_ref[...], cos, sin, even_lane)
            k_ref[0, hh, rows, :] = kh.astype(k_ref.dtype)
            vt = p[:, qw + kw + hh * HEAD_DIM:qw + kw + (hh + 1) * HEAD_DIM].T.astype(v_ref.dtype)
            v_ref[0, hh, :, rows] = jnp.concatenate([vt, ones_pad], axis=0)
        z_ref[rows, :] = p[:, qw + 2 * kw:].astype(z_ref.dtype)


def _mid(x2, cx, ga, f2, gb, conv_w, w_out, g_odd, w_in, q_gain, k_gain, cos, sin, bsz, seq, tm, q_scale):
    n, d = x2.shape
    nt = seq // tm
    halo = 8
    tpb = tm // halo
    last_blk = n // halo - 1
    row = lambda b, i: (b * nt + i, 0)
    fixed = lambda b, i: (0, 0)
    prev = lambda b, i: (jnp.maximum((b * nt + i) * tpb - 1, 0), 0)
    nxt = lambda b, i: (jnp.minimum((b * nt + i + 1) * tpb, last_blk), 0)
    pos = lambda b, i: (i, 0)
    head = lambda b, i: (b, 0, i, 0)
    head_t = lambda b, i: (b, 0, 0, i)
    half = pl.BlockSpec((tm, A_WIDTH), row)
    return pl.pallas_call(
        functools.partial(_mid_kernel, q_scale=q_scale),
        grid=(bsz, nt),
        in_specs=[pl.BlockSpec((tm, d), row),
                  half,
                  pl.BlockSpec((halo, A_WIDTH), prev),
                  pl.BlockSpec((halo, A_WIDTH), nxt),
                  half, half, half,
                  pl.BlockSpec(conv_w.shape, fixed),
                  _resident(w_out.shape),
                  pl.BlockSpec((1, d), fixed),
                  _resident(w_in.shape),
                  pl.BlockSpec((1, HEAD_DIM), fixed),
                  pl.BlockSpec((1, HEAD_DIM), fixed),
                  pl.BlockSpec((tm, HEAD_DIM), pos),
                  pl.BlockSpec((tm, HEAD_DIM), pos)],
        out_specs=[pl.BlockSpec((tm, d), row),
                   pl.BlockSpec((1, N_HEADS, HEAD_DIM, tm), head_t),
                   pl.BlockSpec((1, N_HEADS, 1, tm), head_t),
                   pl.BlockSpec((1, N_KV_HEADS, tm, HEAD_DIM), head),
                   pl.BlockSpec((1, N_KV_HEADS, HEAD_DIM + V_PAD_ROWS, tm), head_t),
                   pl.BlockSpec((tm, N_HEADS * HEAD_DIM), row)],
        out_shape=[jax.ShapeDtypeStruct((n, d), F32),
                   jax.ShapeDtypeStruct((bsz, N_HEADS, HEAD_DIM, seq), BF16),
                   jax.ShapeDtypeStruct((bsz, N_HEADS, 1, seq), F32),
                   jax.ShapeDtypeStruct((bsz, N_KV_HEADS, seq, HEAD_DIM), BF16),
                   jax.ShapeDtypeStruct((bsz, N_KV_HEADS, HEAD_DIM + V_PAD_ROWS, seq), BF16),
                   jax.ShapeDtypeStruct((n, N_HEADS * HEAD_DIM), BF16)],
        scratch_shapes=[pltpu.VMEM(w_out.shape, BF16), pltpu.VMEM(w_in.shape, BF16)],
        compiler_params=_params(2),
        name="even_out_odd_in",
    )(x2, cx, cx, cx, ga, f2, gb, conv_w, w_out, g_odd, w_in, q_gain, k_gain, cos, sin)


def _attn_kernel(q_ref, qn_ref, qn_all_ref, k_ref, v_ref, o_ref, bound_ref, acc_ref, *, tk):
    _, grp, hd, tq = q_ref.shape
    seq = k_ref.shape[2]
    n_chunks = seq // tk
    v_rows = v_ref.shape[2]

    @pl.when(pl.program_id(2) == 0)
    def _norm_bounds():
        def body(j, best):
            kc = k_ref[0, 0, pl.ds(pl.multiple_of(j * tk, tk), tk), :].astype(F32)
            return jnp.maximum(best, jnp.sum(kc * kc, axis=1, keepdims=True))
        best = lax.fori_loop(0, n_chunks, body, jnp.zeros((tk, 1), F32))
        key_bound = jnp.sqrt(jnp.max(best)) * SHIFT_MARGIN
        bound_ref[0] = key_bound
        bound_ref[1] = jnp.max(qn_all_ref[0]) * key_bound

    def chunk(j):
        start = pl.multiple_of(j * tk, tk)
        return k_ref[0, 0, pl.ds(start, tk), :], v_ref[0, 0, :, pl.ds(start, tk)]

    bounded = bound_ref[1] <= SCORE_SPAN_LIMIT

    @pl.when(bounded)
    def _bounded_shift():
        q_all = jnp.concatenate([q_ref[0, g] for g in range(grp)], axis=1)
        shift = jnp.concatenate([qn_ref[0, g] for g in range(grp)], axis=1) * bound_ref[0]

        def body(j, acc):
            kc, vc = chunk(j)
            s = jnp.dot(kc, q_all, preferred_element_type=F32)
            p = jnp.exp2(s - shift).astype(BF16)
            return acc + jnp.dot(vc, p, preferred_element_type=F32)
        acc = lax.fori_loop(0, n_chunks, body, jnp.zeros((v_rows, grp * tq), F32), unroll=True)
        for g in range(grp):
            acc_ref[g] = acc[:, g * tq:(g + 1) * tq]

    @pl.when(jnp.logical_not(bounded))
    def _online_max():
        for g in range(grp):
            def body(j, carry):
                m, acc = carry
                kc, vc = chunk(j)
                s = jnp.dot(kc, q_ref[0, g], preferred_element_type=F32)
                m_new = jnp.maximum(m, jnp.max(s, axis=0, keepdims=True))
                p = jnp.exp2(s - m_new).astype(BF16)
                acc = jnp.exp2(m - m_new) * acc + jnp.dot(vc, p, preferred_element_type=F32)
                return m_new, acc
            init = (jnp.full((1, tq), -jnp.inf, F32), jnp.zeros((v_rows, tq), F32))
            _, acc = lax.fori_loop(0, n_chunks, body, init)
            acc_ref[g] = acc

    for g in range(grp):
        acc = acc_ref[g]
        o_t = acc[:hd] / acc[hd:hd + 1]
        o_ref[0, :, g * hd:(g + 1) * hd] = o_t.T.astype(o_ref.dtype)


def _attention(qt, qn, k, vt, tq, tk):
    bsz, nh, hd, seq = qt.shape
    nkv = k.shape[1]
    grp = nh // nkv
    v_rows = vt.shape[2]
    return pl.pallas_call(
        functools.partial(_attn_kernel, tk=tk),
        grid=(bsz, nkv, seq // tq),
        in_specs=[pl.BlockSpec((1, grp, hd, tq), lambda b, h, i: (b, h, 0, i)),
                  pl.BlockSpec((1, grp, 1, tq), lambda b, h, i: (b, h, 0, i)),
                  pl.BlockSpec((1, grp, 1, seq), lambda b, h, i: (b, h, 0, 0)),
                  pl.BlockSpec((1, 1, seq, hd), lambda b, h, i: (b, h, 0, 0)),
                  pl.BlockSpec((1, 1, v_rows, seq), lambda b, h, i: (b, h, 0, 0))],
        out_specs=pl.BlockSpec((1, tq, grp * hd), lambda b, h, i: (b, i, h)),
        out_shape=jax.ShapeDtypeStruct((bsz, seq, nh * hd), BF16),
        scratch_shapes=[pltpu.SMEM((2,), F32), pltpu.VMEM((grp, v_rows, tq), F32)],
        compiler_params=_params(3),
        name="gqa_attention",
    )(qt, qn, qn, k, vt)


def _odd_out_kernel(x1_ref, o_ref, z_ref, w_ref, g_ref, out_ref, wb_ref):
    _cast_once(pl.program_id(0) == 0, w_ref, wb_ref)
    y = (o_ref[...].astype(F32) * _silu(z_ref[...].astype(F32))).astype(BF16)
    x2 = x1_ref[...] + jnp.dot(y, wb_ref[...], preferred_element_type=F32)
    out_ref[...] = _rms_norm(x2, g_ref[...])


def _odd_out(x1, o2, z, w, g, tm):
    n, d = x1.shape
    row = lambda i: (i, 0)
    fixed = lambda i: (0, 0)
    return pl.pallas_call(
        _odd_out_kernel,
        grid=(n // tm,),
        in_specs=[pl.BlockSpec((tm, d), row),
                  pl.BlockSpec((tm, o2.shape[1]), row),
                  pl.BlockSpec((tm, z.shape[1]), row),
                  _resident(w.shape),
                  pl.BlockSpec((1, d), fixed)],
        out_specs=pl.BlockSpec((tm, d), row),
        out_shape=jax.ShapeDtypeStruct((n, d), F32),
        scratch_shapes=[pltpu.VMEM(w.shape, BF16)],
        compiler_params=_params(1),
        name="odd_out",
    )(x1, o2, z, w, g)


class _Tiles(NamedTuple):
    proj_rows: int
    mid_rows: int
    dft_rows: int
    queries: int
    keys: int


def _tiles(seq):
    assert seq % FFT_FAST == 0 and seq % GRID_W == 0, seq
    t = _Tiles(proj_rows=min(1024, seq), mid_rows=min(512, seq), dft_rows=min(DFT_ROWS_PER_STEP, seq // FFT_FAST),
               queries=min(512, seq), keys=min(512, seq))
    assert all(seq % v == 0 for v in (t.proj_rows, t.mid_rows, t.queries, t.keys)), (seq, t)
    assert (seq // FFT_FAST) % t.dft_rows == 0 and FFT_FAST % DFT_ROWS_PER_STEP == 0, (seq, t)
    return t


def kernel(x, norm_even, w_in_even, conv_w, w_out_even, norm_odd, w_in_odd, q_gain, k_gain, w_out_odd, final_norm):
    bsz, seq, d = x.shape
    assert norm_even.shape[0] == 1 and norm_odd.shape[0] == 1, "one even and one odd layer"
    n = bsz * seq
    slow = seq // FFT_FAST
    t = _tiles(seq)
    x2 = x.reshape(n, d)

    cx, ga, bu, gb = _even_in(x2, norm_even[0][None, :], w_in_even[0], t.proj_rows)
    w1, t2, wc = _dft_tables(seq)
    y = _dft1(w1, bu.reshape(bsz, slow, FFT_FAST, B_WIDTH), nb=DFT_ROWS_PER_STEP)
    scale = 1.0 / math.sqrt(seq * B_GROUP_DIM)
    f = _dft2(y.reshape(bsz, 2, seq, B_WIDTH), t2, wc, cc=t.dft_rows, scale=scale)
    f2 = f.reshape(n, B_WIDTH)

    cos, sin = _rope_tables(seq)
    q_scale = HEAD_DIM ** -0.5 * math.log2(math.e)
    x1, qt, qn, k, vt, z = _mid(x2, cx, ga, f2, gb, conv_w[0], w_out_even[0],
                                 norm_odd[0][None, :], w_in_odd[0],
                                 q_gain[0][None, :], k_gain[0][None, :], cos, sin, bsz, seq, t.mid_rows, q_scale)

    o = _attention(qt, qn, k, vt, tq=t.queries, tk=t.keys)
    out = _odd_out(x1, o.reshape(n, N_HEADS * HEAD_DIM), z, w_out_odd[0], final_norm[None, :], t.proj_rows)
    return out.reshape(bsz, seq, d)
```

```python
import functools
import math
from typing import NamedTuple

import numpy as np
import jax
import jax.numpy as jnp
from jax import lax
from jax.experimental import pallas as pl
from jax.experimental.pallas import tpu as pltpu

F32 = jnp.float32
BF16 = jnp.bfloat16

EPS = 1e-6
GRID_W = 64
ROPE_THETA = 10000.0
CONV_WIDTH = 3
HEAD_DIM = 128
N_HEADS = 8
N_KV_HEADS = 2
KV_GROUP = N_HEADS // N_KV_HEADS
A_WIDTH = 512
B_WIDTH = 512
B_GROUPS = 4
B_GROUP_DIM = 128
FFT_FAST = 128
DFT_ROWS_PER_STEP = 16
V_PAD_ROWS = 16
SCORE_SPAN_LIMIT = 60.0
SHIFT_MARGIN = 1.01
MID_SUBTILES = 2

VMEM_LIMIT_BYTES = 56 * 1024 * 1024


def _silu(z):
    return z / (1.0 + jnp.exp(-z))


def _rms_norm(x, g):
    return x * lax.rsqrt(jnp.mean(x * x, axis=-1, keepdims=True) + EPS) * g


def _resident(shape):
    zeros = (0,) * len(shape)
    return pl.BlockSpec(shape, lambda *_: zeros, pipeline_mode=pl.Buffered(1))


def _cast_once(first_step, src_ref, dst_ref):
    @pl.when(first_step)
    def _():
        dst_ref[...] = src_ref[...].astype(dst_ref.dtype)


def _params(n_grid_dims):
    return pltpu.CompilerParams(
        dimension_semantics=("arbitrary",) * n_grid_dims,
        vmem_limit_bytes=VMEM_LIMIT_BYTES)


def _even_in_kernel(x_ref, g_ref, w_ref, cx_ref, ga_ref, bu_ref, gb_ref, wb_ref):
    _cast_once(pl.program_id(0) == 0, w_ref, wb_ref)
    h = _rms_norm(x_ref[...], g_ref[...]).astype(BF16)
    p = jnp.dot(h, wb_ref[...], preferred_element_type=F32)
    a_x = p[:, 0 * A_WIDTH:1 * A_WIDTH]
    a_b = p[:, 1 * A_WIDTH:2 * A_WIDTH]
    a_c = p[:, 2 * A_WIDTH:3 * A_WIDTH]
    a_z = p[:, 3 * A_WIDTH:4 * A_WIDTH]
    b_u = p[:, 4 * A_WIDTH:4 * A_WIDTH + B_WIDTH]
    b_z = p[:, 4 * A_WIDTH + B_WIDTH:]
    cx_ref[...] = (a_c * a_x).astype(cx_ref.dtype)
    ga_ref[...] = (a_b * _silu(a_z)).astype(ga_ref.dtype)
    bu_ref[...] = b_u.astype(bu_ref.dtype)
    gb_ref[...] = _silu(b_z).astype(gb_ref.dtype)


def _even_in(x2, g, w, tm):
    n, d = x2.shape
    e_in = w.shape[1]
    out = jax.ShapeDtypeStruct((n, A_WIDTH), BF16)
    row = lambda i: (i, 0)
    fixed = lambda i: (0, 0)
    return pl.pallas_call(
        _even_in_kernel,
        grid=(n // tm,),
        in_specs=[pl.BlockSpec((tm, d), row),
                  pl.BlockSpec((1, d), fixed),
                  _resident((d, e_in))],
        out_specs=[pl.BlockSpec((tm, A_WIDTH), row)] * 4,
        out_shape=[out] * 4,
        scratch_shapes=[pltpu.VMEM((d, e_in), BF16)],
        compiler_params=_params(1),
        name="even_in",
    )(x2, g, w)


def _dft_tables(seq):
    slow = seq // FFT_FAST
    a = np.arange(slow)
    th = 2.0 * np.pi * np.outer(a, a) / slow
    w1 = np.concatenate([np.cos(th), -np.sin(th)], axis=0)
    b = np.arange(FFT_FAST)
    k = a[:, None, None] + slow * b[None, :, None]
    th2 = 2.0 * np.pi * (k * b[None, None, :]) / seq
    cs, sn = np.cos(th2), np.sin(th2)
    t2 = np.concatenate([np.concatenate([cs, sn], axis=2),
                         np.concatenate([-sn, cs], axis=2)], axis=1)
    ch = np.arange(B_GROUP_DIM)
    thc = 2.0 * np.pi * np.outer(ch, ch) / B_GROUP_DIM
    wc = np.concatenate([np.cos(thc), np.sin(thc)], axis=0)
    return tuple(jnp.asarray(t, dtype=F32) for t in (w1, t2, wc))


def _dft1_kernel(w_ref, u_ref, y_ref):
    _, slow, nb, width = u_ref.shape
    ut = jnp.transpose(u_ref[0], (1, 0, 2))
    x = jnp.concatenate([ut[b] for b in range(nb)], axis=1)
    y = jnp.dot(w_ref[...].astype(BF16), x, preferred_element_type=F32)
    yb = y.astype(y_ref.dtype)
    for part in range(2):
        rows = slice(part * slow, (part + 1) * slow)
        by_b = jnp.stack([yb[rows, b * width:(b + 1) * width] for b in range(nb)], axis=0)
        y_ref[0, part] = jnp.transpose(by_b, (1, 0, 2))


def _dft1(w1, u4, nb):
    bsz, slow, fast, width = u4.shape
    return pl.pallas_call(
        _dft1_kernel,
        grid=(bsz, fast // nb),
        in_specs=[pl.BlockSpec((2 * slow, slow), lambda b, j: (0, 0)),
                  pl.BlockSpec((1, slow, nb, width), lambda b, j: (b, 0, j, 0))],
        out_specs=pl.BlockSpec((1, 2, slow, nb, width), lambda b, j: (b, 0, 0, j, 0)),
        out_shape=jax.ShapeDtypeStruct((bsz, 2, slow, fast, width), BF16),
        compiler_params=_params(2),
        name="dft_stage1",
    )(w1, u4)


def _dft2_kernel(y_ref, t_ref, wc_ref, f_ref, *, cc, scale):
    gd = B_GROUP_DIM
    by_c = []
    for ci in range(cc):
        rows = slice(ci * FFT_FAST, (ci + 1) * FFT_FAST)
        ys = jnp.concatenate([y_ref[0, 0, rows, :], y_ref[0, 1, rows, :]], axis=0)
        p = jnp.dot(t_ref[ci].astype(BF16), ys, preferred_element_type=F32).astype(BF16)
        lhs = jnp.concatenate(
            [jnp.concatenate([p[:FFT_FAST, g * gd:(g + 1) * gd], p[FFT_FAST:, g * gd:(g + 1) * gd]], axis=1)
             for g in range(B_GROUPS)], axis=0)
        f = jnp.dot(lhs, wc_ref[...].astype(BF16), preferred_element_type=F32) * scale
        by_c.append(jnp.concatenate([f[g * FFT_FAST:(g + 1) * FFT_FAST] for g in range(B_GROUPS)],
                                    axis=1).astype(f_ref.dtype))
    f_ref[0] = jnp.transpose(jnp.stack(by_c, axis=0), (1, 0, 2))


def _dft2(y4, t2, wc, cc, scale):
    bsz, _, seq, width = y4.shape
    slow = seq // FFT_FAST
    return pl.pallas_call(
        functools.partial(_dft2_kernel, cc=cc, scale=scale),
        grid=(slow // cc, bsz),
        in_specs=[pl.BlockSpec((1, 2, cc * FFT_FAST, width), lambda j, b: (b, 0, j, 0)),
                  pl.BlockSpec((cc, 2 * FFT_FAST, 2 * FFT_FAST), lambda j, b: (j, 0, 0)),
                  pl.BlockSpec(wc.shape, lambda j, b: (0, 0))],
        out_specs=pl.BlockSpec((1, FFT_FAST, cc, width), lambda j, b: (b, 0, j, 0)),
        out_shape=jax.ShapeDtypeStruct((bsz, FFT_FAST, slow, width), BF16),
        compiler_params=_params(2),
        name="dft_stage2",
    )(y4, t2, wc)


def _rope_tables(seq):
    t = np.arange(seq)
    row = (t // GRID_W).astype(np.float64)
    col = (t % GRID_W).astype(np.float64)
    n_pair = HEAD_DIM // 4
    inv = ROPE_THETA ** (-np.arange(n_pair, dtype=np.float64) / n_pair)
    ang = np.concatenate([row[:, None] * inv, col[:, None] * inv], axis=-1)
    cos = np.repeat(np.cos(ang), 2, axis=1)
    sin = np.repeat(np.sin(ang), 2, axis=1)
    sign = np.tile(np.array([-1.0, 1.0]), HEAD_DIM // 2)
    return jnp.asarray(cos, dtype=F32), jnp.asarray(sin * sign, dtype=F32)


def _norm_rope(xh, gain, cos, sin_signed, even_lane):
    xn = _rms_norm(xh, gain)
    partner = jnp.where(even_lane, pltpu.roll(xn, HEAD_DIM - 1, axis=1), pltpu.roll(xn, 1, axis=1))
    return xn * cos + partner * sin_signed


def _mid_kernel(x_ref, cx_ref, cxp_ref, cxn_ref, ga_ref, f_ref, gb_ref, cw_ref, wo_ref,
                g_ref, wi_ref, qg_ref, kg_ref, cos_ref, sin_ref,
                x1_ref, q_ref, qn_ref, k_ref, v_ref, z_ref, wob_ref, wib_ref, *, q_scale):
    first_step = (pl.program_id(0) == 0) & (pl.program_id(1) == 0)
    _cast_once(first_step, wo_ref, wob_ref)
    _cast_once(first_step, wi_ref, wib_ref)
    i = pl.program_id(1)
    n_i = pl.num_programs(1)
    tm = x_ref.shape[0]
    cx = cx_ref[...].astype(F32)
    prev_row = jnp.where(i > 0, cxp_ref[7:8, :].astype(F32), 0.0)
    next_row = jnp.where(i < n_i - 1, cxn_ref[0:1, :].astype(F32), 0.0)
    r = lax.broadcasted_iota(jnp.int32, cx.shape, 0)
    up = jnp.where(r == 0, prev_row, pltpu.roll(cx, 1, axis=0))
    dn = jnp.where(r == tm - 1, next_row, pltpu.roll(cx, tm - 1, axis=0))
    conv = up * cw_ref[0:1, :] + cx * cw_ref[1:2, :] + dn * cw_ref[2:3, :]
    y_a = ga_ref[...].astype(F32) * conv
    y_b = f_ref[...].astype(F32) * gb_ref[...].astype(F32)
    y = jnp.concatenate([y_a, y_b], axis=1).astype(BF16)
    qw = N_HEADS * HEAD_DIM
    kw = N_KV_HEADS * HEAD_DIM
    ts = tm // MID_SUBTILES
    even_lane = (lax.broadcasted_iota(jnp.int32, (ts, HEAD_DIM), 1) & 1) == 0
    ones_pad = (lax.broadcasted_iota(jnp.int32, (V_PAD_ROWS, ts), 0) == 0).astype(v_ref.dtype)
    for sub in range(MID_SUBTILES):
        rows = slice(sub * ts, (sub + 1) * ts)
        x1 = x_ref[rows, :] + jnp.dot(y[rows], wob_ref[...], preferred_element_type=F32)
        x1_ref[rows, :] = x1
        h = _rms_norm(x1, g_ref[...]).astype(BF16)
        p = jnp.dot(h, wib_ref[...], preferred_element_type=F32)
        cos = cos_ref[rows, :]
        sin = sin_ref[rows, :]
        for hh in range(N_HEADS):
            qh = _norm_rope(p[:, hh * HEAD_DIM:(hh + 1) * HEAD_DIM], qg_ref[...], cos, sin, even_lane)
            qt = (qh * q_scale).T.astype(q_ref.dtype)
            q_ref[0, hh, :, rows] = qt
            qf = qt.astype(F32)
            qn_ref[0, hh, :, rows] = jnp.sqrt(jnp.sum(qf * qf, axis=0, keepdims=True))
        for hh in range(N_KV_HEADS):
            kh = _norm_rope(p[:, qw + hh * HEAD_DIM:qw + (hh + 1) * HEAD_DIM], kg_ref[...], cos, sin, even_lane)
            k_ref[0, hh, rows, :] = kh.astype(k_ref.dtype)
            vt = p[:, qw + kw + hh * HEAD_DIM:qw + kw + (hh + 1) * HEAD_DIM].T.astype(v_ref.dtype)
            v_ref[0, hh, :, rows] = jnp.concatenate([vt, ones_pad], axis=0)
        z_ref[rows, :] = p[:, qw + 2 * kw:].astype(z_ref.dtype)


def _mid(x2, cx, ga, f2, gb, conv_w, w_out, g_odd, w_in, q_gain, k_gain, cos, sin, bsz, seq, tm, q_scale):
    n, d = x2.shape
    nt = seq // tm
    halo = 8
    tpb = tm // halo
    last_blk = n // halo - 1
    row = lambda b, i: (b * nt + i, 0)
    fixed = lambda b, i: (0, 0)
    prev = lambda b, i: (jnp.maximum((b * nt + i) * tpb - 1, 0), 0)
    nxt = lambda b, i: (jnp.minimum((b * nt + i + 1) * tpb, last_blk), 0)
    pos = lambda b, i: (i, 0)
    head = lambda b, i: (b, 0, i, 0)
    head_t = lambda b, i: (b, 0, 0, i)
    half = pl.BlockSpec((tm, A_WIDTH), row)
    return pl.pallas_call(
        functools.partial(_mid_kernel, q_scale=q_scale),
        grid=(bsz, nt),
        in_specs=[pl.BlockSpec((tm, d), row),
                  half,
                  pl.BlockSpec((halo, A_WIDTH), prev),
                  pl.BlockSpec((halo, A_WIDTH), nxt),
                  half, half, half,
                  pl.BlockSpec(conv_w.shape, fixed),
                  _resident(w_out.shape),
                  pl.BlockSpec((1, d), fixed),
                  _resident(w_in.shape),
                  pl.BlockSpec((1, HEAD_DIM), fixed),
                  pl.BlockSpec((1, HEAD_DIM), fixed),
                  pl.BlockSpec((tm, HEAD_DIM), pos),
                  pl.BlockSpec((tm, HEAD_DIM), pos)],
        out_specs=[pl.BlockSpec((tm, d), row),
                   pl.BlockSpec((1, N_HEADS, HEAD_DIM, tm), head_t),
                   pl.BlockSpec((1, N_HEADS, 1, tm), head_t),
                   pl.BlockSpec((1, N_KV_HEADS, tm, HEAD_DIM), head),
                   pl.BlockSpec((1, N_KV_HEADS, HEAD_DIM + V_PAD_ROWS, tm), head_t),
                   pl.BlockSpec((tm, N_HEADS * HEAD_DIM), row)],
        out_shape=[jax.ShapeDtypeStruct((n, d), F32),
                   jax.ShapeDtypeStruct((bsz, N_HEADS, HEAD_DIM, seq), BF16),
                   jax.ShapeDtypeStruct((bsz, N_HEADS, 1, seq), F32),
                   jax.ShapeDtypeStruct((bsz, N_KV_HEADS, seq, HEAD_DIM), BF16),
                   jax.ShapeDtypeStruct((bsz, N_KV_HEADS, HEAD_DIM + V_PAD_ROWS, seq), BF16),
                   jax.ShapeDtypeStruct((n, N_HEADS * HEAD_DIM), BF16)],
        scratch_shapes=[pltpu.VMEM(w_out.shape, BF16), pltpu.VMEM(w_in.shape, BF16)],
        compiler_params=_params(2),
        name="even_out_odd_in",
    )(x2, cx, cx, cx, ga, f2, gb, conv_w, w_out, g_odd, w_in, q_gain, k_gain, cos, sin)


def _attn_kernel(q_ref, qn_ref, k_ref, v_ref, o_ref, acc_ref, *, tq, tk):
    _, grp, hd, seq = q_ref.shape
    n_chunks = seq // tk
    v_rows = v_ref.shape[2]

    def key_norm(j, best):
        kc = k_ref[0, 0, pl.ds(pl.multiple_of(j * tk, tk), tk), :].astype(F32)
        return jnp.maximum(best, jnp.sum(kc * kc, axis=1, keepdims=True))
    key_bound = jnp.sqrt(jnp.max(lax.fori_loop(0, n_chunks, key_norm, jnp.zeros((tk, 1), F32)))) * SHIFT_MARGIN
    bounded = jnp.max(qn_ref[0]) * key_bound <= SCORE_SPAN_LIMIT

    def chunk(j):
        start = pl.multiple_of(j * tk, tk)
        return k_ref[0, 0, pl.ds(start, tk), :], v_ref[0, 0, :, pl.ds(start, tk)]

    def query_tile(i, carry):
        cols = pl.ds(pl.multiple_of(i * tq, tq), tq)

        @pl.when(bounded)
        def _bounded_shift():
            q_all = jnp.concatenate([q_ref[0, g, :, cols] for g in range(grp)], axis=1)
            shift = jnp.concatenate([qn_ref[0, g, :, cols] for g in range(grp)], axis=1) * key_bound

            def body(j, acc):
                kc, vc = chunk(j)
                s = jnp.dot(kc, q_all, preferred_element_type=F32)
                p = jnp.exp2(s - shift).astype(BF16)
                return acc + jnp.dot(vc, p, preferred_element_type=F32)
            acc = lax.fori_loop(0, n_chunks, body, jnp.zeros((v_rows, grp * tq), F32), unroll=True)
            for g in range(grp):
                acc_ref[g] = acc[:, g * tq:(g + 1) * tq]

        @pl.when(jnp.logical_not(bounded))
        def _online_max():
            for g in range(grp):
                q_g = q_ref[0, g, :, cols]

                def body(j, carry):
                    m, acc = carry
                    kc, vc = chunk(j)
                    s = jnp.dot(kc, q_g, preferred_element_type=F32)
                    m_new = jnp.maximum(m, jnp.max(s, axis=0, keepdims=True))
                    p = jnp.exp2(s - m_new).astype(BF16)
                    acc = jnp.exp2(m - m_new) * acc + jnp.dot(vc, p, preferred_element_type=F32)
                    return m_new, acc
                init = (jnp.full((1, tq), -jnp.inf, F32), jnp.zeros((v_rows, tq), F32))
                _, acc = lax.fori_loop(0, n_chunks, body, init)
                acc_ref[g] = acc

        for g in range(grp):
            acc = acc_ref[g]
            o_t = acc[:hd] / acc[hd:hd + 1]
            o_ref[0, cols, g * hd:(g + 1) * hd] = o_t.T.astype(o_ref.dtype)
        return carry

    lax.fori_loop(0, seq // tq, query_tile, 0)


def _attention(qt, qn, k, vt, tq, tk):
    bsz, nh, hd, seq = qt.shape
    nkv = k.shape[1]
    grp = nh // nkv
    v_rows = vt.shape[2]
    return pl.pallas_call(
        functools.partial(_attn_kernel, tq=tq, tk=tk),
        grid=(bsz, nkv),
        in_specs=[pl.BlockSpec((1, grp, hd, seq), lambda b, h: (b, h, 0, 0)),
                  pl.BlockSpec((1, grp, 1, seq), lambda b, h: (b, h, 0, 0)),
                  pl.BlockSpec((1, 1, seq, hd), lambda b, h: (b, h, 0, 0)),
                  pl.BlockSpec((1, 1, v_rows, seq), lambda b, h: (b, h, 0, 0))],
        out_specs=pl.BlockSpec((1, seq, grp * hd), lambda b, h: (b, 0, h)),
        out_shape=jax.ShapeDtypeStruct((bsz, seq, nh * hd), BF16),
        scratch_shapes=[pltpu.VMEM((grp, v_rows, tq), F32)],
        compiler_params=_params(2),
        name="gqa_attention",
    )(qt, qn, k, vt)


def _odd_out_kernel(x1_ref, o_ref, z_ref, w_ref, g_ref, out_ref, wb_ref):
    _cast_once(pl.program_id(0) == 0, w_ref, wb_ref)
    y = (o_ref[...].astype(F32) * _silu(z_ref[...].astype(F32))).astype(BF16)
    x2 = x1_ref[...] + jnp.dot(y, wb_ref[...], preferred_element_type=F32)
    out_ref[...] = _rms_norm(x2, g_ref[...])


def _odd_out(x1, o2, z, w, g, tm):
    n, d = x1.shape
    row = lambda i: (i, 0)
    fixed = lambda i: (0, 0)
    return pl.pallas_call(
        _odd_out_kernel,
        grid=(n // tm,),
        in_specs=[pl.BlockSpec((tm, d), row),
                  pl.BlockSpec((tm, o2.shape[1]), row),
                  pl.BlockSpec((tm, z.shape[1]), row),
                  _resident(w.shape),
                  pl.BlockSpec((1, d), fixed)],
        out_specs=pl.BlockSpec((tm, d), row),
        out_shape=jax.ShapeDtypeStruct((n, d), F32),
        scratch_shapes=[pltpu.VMEM(w.shape, BF16)],
        compiler_params=_params(1),
        name="odd_out",
    )(x1, o2, z, w, g)


class _Tiles(NamedTuple):
    proj_rows: int
    mid_rows: int
    dft_rows: int
    queries: int
    keys: int


def _tiles(seq):
    assert seq % FFT_FAST == 0 and seq % GRID_W == 0, seq
    t = _Tiles(proj_rows=min(1024, seq), mid_rows=min(512, seq), dft_rows=min(DFT_ROWS_PER_STEP, seq // FFT_FAST),
               queries=min(512, seq), keys=min(512, seq))
    assert all(seq % v == 0 for v in (t.proj_rows, t.mid_rows, t.queries, t.keys)), (seq, t)
    assert (seq // FFT_FAST) % t.dft_rows == 0 and FFT_FAST % DFT_ROWS_PER_STEP == 0, (seq, t)
    return t


def kernel(x, norm_even, w_in_even, conv_w, w_out_even, norm_odd, w_in_odd, q_gain, k_gain, w_out_odd, final_norm):
    bsz, seq, d = x.shape
    assert norm_even.shape[0] == 1 and norm_odd.shape[0] == 1, "one even and one odd layer"
    n = bsz * seq
    slow = seq // FFT_FAST
    t = _tiles(seq)
    x2 = x.reshape(n, d)

    cx, ga, bu, gb = _even_in(x2, norm_even[0][None, :], w_in_even[0], t.proj_rows)
    w1, t2, wc = _dft_tables(seq)
    y = _dft1(w1, bu.reshape(bsz, slow, FFT_FAST, B_WIDTH), nb=DFT_ROWS_PER_STEP)
    scale = 1.0 / math.sqrt(seq * B_GROUP_DIM)
    f = _dft2(y.reshape(bsz, 2, seq, B_WIDTH), t2, wc, cc=t.dft_rows, scale=scale)
    f2 = f.reshape(n, B_WIDTH)

    cos, sin = _rope_tables(seq)
    q_scale = HEAD_DIM ** -0.5 * math.log2(math.e)
    x1, qt, qn, k, vt, z = _mid(x2, cx, ga, f2, gb, conv_w[0], w_out_even[0],
                                 norm_odd[0][None, :], w_in_odd[0],
                                 q_gain[0][None, :], k_gain[0][None, :], cos, sin, bsz, seq, t.mid_rows, q_scale)

    o = _attention(qt, qn, k, vt, tq=t.queries, tk=t.keys)
    out = _odd_out(x1, o.reshape(n, N_HEADS * HEAD_DIM), z, w_out_odd[0], final_norm[None, :], t.proj_rows)
    return out.reshape(bsz, seq, d)
```

```python
import functools
import math
from typing import NamedTuple

import numpy as np
import jax
import jax.numpy as jnp
from jax import lax
from jax.experimental import pallas as pl
from jax.experimental.pallas import tpu as pltpu

F32 = jnp.float32
BF16 = jnp.bfloat16

EPS = 1e-6
GRID_W = 64
ROPE_THETA = 10000.0
CONV_WIDTH = 3
HEAD_DIM = 128
N_HEADS = 8
N_KV_HEADS = 2
KV_GROUP = N_HEADS // N_KV_HEADS
A_WIDTH = 512
B_WIDTH = 512
B_GROUPS = 4
B_GROUP_DIM = 128
FFT_FAST = 128
DFT_ROWS_PER_STEP = 16
V_PAD_ROWS = 16
SCORE_SPAN_LIMIT = 60.0
SHIFT_MARGIN = 1.01
MID_SUBTILES = 2

VMEM_LIMIT_BYTES = 56 * 1024 * 1024


def _silu(z):
    return z / (1.0 + jnp.exp(-z))


def _rms_norm(x, g):
    return x * lax.rsqrt(jnp.mean(x * x, axis=-1, keepdims=True) + EPS) * g


def _resident(shape):
    zeros = (0,) * len(shape)
    return pl.BlockSpec(shape, lambda *_: zeros, pipeline_mode=pl.Buffered(1))


def _cast_once(first_step, src_ref, dst_ref):
    @pl.when(first_step)
    def _():
        dst_ref[...] = src_ref[...].astype(dst_ref.dtype)


def _params(n_grid_dims):
    return pltpu.CompilerParams(
        dimension_semantics=("arbitrary",) * n_grid_dims,
        vmem_limit_bytes=VMEM_LIMIT_BYTES)


def _even_in_kernel(x_ref, g_ref, w_ref, cx_ref, ga_ref, bu_ref, gb_ref, wb_ref):
    _cast_once(pl.program_id(0) == 0, w_ref, wb_ref)
    h = _rms_norm(x_ref[...], g_ref[...]).astype(BF16)
    p = jnp.dot(h, wb_ref[...], preferred_element_type=F32)
    a_x = p[:, 0 * A_WIDTH:1 * A_WIDTH]
    a_b = p[:, 1 * A_WIDTH:2 * A_WIDTH]
    a_c = p[:, 2 * A_WIDTH:3 * A_WIDTH]
    a_z = p[:, 3 * A_WIDTH:4 * A_WIDTH]
    b_u = p[:, 4 * A_WIDTH:4 * A_WIDTH + B_WIDTH]
    b_z = p[:, 4 * A_WIDTH + B_WIDTH:]
    cx_ref[...] = (a_c * a_x).astype(cx_ref.dtype)
    ga_ref[...] = (a_b * _silu(a_z)).astype(ga_ref.dtype)
    bu_ref[...] = b_u.astype(bu_ref.dtype)
    gb_ref[...] = _silu(b_z).astype(gb_ref.dtype)


def _even_in(x2, g, w, tm):
    n, d = x2.shape
    e_in = w.shape[1]
    out = jax.ShapeDtypeStruct((n, A_WIDTH), BF16)
    row = lambda i: (i, 0)
    fixed = lambda i: (0, 0)
    return pl.pallas_call(
        _even_in_kernel,
        grid=(n // tm,),
        in_specs=[pl.BlockSpec((tm, d), row),
                  pl.BlockSpec((1, d), fixed),
                  _resident((d, e_in))],
        out_specs=[pl.BlockSpec((tm, A_WIDTH), row)] * 4,
        out_shape=[out] * 4,
        scratch_shapes=[pltpu.VMEM((d, e_in), BF16)],
        compiler_params=_params(1),
        name="even_in",
    )(x2, g, w)


def _dft_tables(seq):
    slow = seq // FFT_FAST
    a = np.arange(slow)
    th = 2.0 * np.pi * np.outer(a, a) / slow
    w1 = np.concatenate([np.cos(th), -np.sin(th)], axis=0)
    b = np.arange(FFT_FAST)
    k = a[:, None, None] + slow * b[None, :, None]
    th2 = 2.0 * np.pi * (k * b[None, None, :]) / seq
    cs, sn = np.cos(th2), np.sin(th2)
    t2 = np.concatenate([np.concatenate([cs, sn], axis=2),
                         np.concatenate([-sn, cs], axis=2)], axis=1)
    ch = np.arange(B_GROUP_DIM)
    thc = 2.0 * np.pi * np.outer(ch, ch) / B_GROUP_DIM
    wc = np.concatenate([np.cos(thc), np.sin(thc)], axis=0)
    return tuple(jnp.asarray(t, dtype=F32) for t in (w1, t2, wc))


def _dft_kernel(w1_ref, u_ref, t_ref, wc_ref, f_ref, y_ref, *, nb, cc, scale):
    _, slow, fast, width = u_ref.shape
    gd = B_GROUP_DIM

    @pl.when(pl.program_id(1) == 0)
    def _stage1():
        w1 = w1_ref[...].astype(BF16)

        def b_block(i, carry):
            cols = pl.ds(pl.multiple_of(i * nb, nb), nb)
            ut = jnp.transpose(u_ref[0, :, cols, :], (1, 0, 2))
            x = jnp.concatenate([ut[b] for b in range(nb)], axis=1)
            yb = jnp.dot(w1, x, preferred_element_type=F32).astype(y_ref.dtype)
            for part in range(2):
                rows = slice(part * slow, (part + 1) * slow)
                by_b = jnp.stack([yb[rows, b * width:(b + 1) * width] for b in range(nb)], axis=0)
                y_ref[part, :, cols, :] = jnp.transpose(by_b, (1, 0, 2))
            return carry
        lax.fori_loop(0, fast // nb, b_block, 0)

    wc = wc_ref[...].astype(BF16)
    c0 = pl.program_id(1) * cc
    by_c = []
    for ci in range(cc):
        ys = jnp.concatenate([y_ref[0, c0 + ci], y_ref[1, c0 + ci]], axis=0)
        p = jnp.dot(t_ref[ci].astype(BF16), ys, preferred_element_type=F32).astype(BF16)
        lhs = jnp.concatenate(
            [jnp.concatenate([p[:fast, g * gd:(g + 1) * gd], p[fast:, g * gd:(g + 1) * gd]], axis=1)
             for g in range(B_GROUPS)], axis=0)
        f = jnp.dot(lhs, wc, preferred_element_type=F32) * scale
        by_c.append(jnp.concatenate([f[g * fast:(g + 1) * fast] for g in range(B_GROUPS)],
                                    axis=1).astype(f_ref.dtype))
    f_ref[0] = jnp.transpose(jnp.stack(by_c, axis=0), (1, 0, 2))


def _fourier_mix(w1, t2, wc, u4, nb, cc, scale):
    bsz, slow, fast, width = u4.shape
    return pl.pallas_call(
        functools.partial(_dft_kernel, nb=nb, cc=cc, scale=scale),
        grid=(bsz, slow // cc),
        in_specs=[pl.BlockSpec(w1.shape, lambda b, j: (0, 0)),
                  pl.BlockSpec((1, slow, fast, width), lambda b, j: (b, 0, 0, 0)),
                  pl.BlockSpec((cc, 2 * fast, 2 * fast), lambda b, j: (j, 0, 0)),
                  pl.BlockSpec(wc.shape, lambda b, j: (0, 0))],
        out_specs=pl.BlockSpec((1, fast, cc, width), lambda b, j: (b, 0, j, 0)),
        out_shape=jax.ShapeDtypeStruct((bsz, fast, slow, width), BF16),
        scratch_shapes=[pltpu.VMEM((2, slow, fast, width), BF16)],
        compiler_params=_params(2),
        name="fourier_mix",
    )(w1, u4, t2, wc)


def _rope_tables(seq):
    t = np.arange(seq)
    row = (t // GRID_W).astype(np.float64)
    col = (t % GRID_W).astype(np.float64)
    n_pair = HEAD_DIM // 4
    inv = ROPE_THETA ** (-np.arange(n_pair, dtype=np.float64) / n_pair)
    ang = np.concatenate([row[:, None] * inv, col[:, None] * inv], axis=-1)
    cos = np.repeat(np.cos(ang), 2, axis=1)
    sin = np.repeat(np.sin(ang), 2, axis=1)
    sign = np.tile(np.array([-1.0, 1.0]), HEAD_DIM // 2)
    return jnp.asarray(cos, dtype=F32), jnp.asarray(sin * sign, dtype=F32)


def _norm_rope(xh, gain, cos, sin_signed, even_lane):
    xn = _rms_norm(xh, gain)
    partner = jnp.where(even_lane, pltpu.roll(xn, HEAD_DIM - 1, axis=1), pltpu.roll(xn, 1, axis=1))
    return xn * cos + partner * sin_signed


def _mid_kernel(x_ref, cx_ref, cxp_ref, cxn_ref, ga_ref, f_ref, gb_ref, cw_ref, wo_ref,
                g_ref, wi_ref, qg_ref, kg_ref, cos_ref, sin_ref,
                x1_ref, q_ref, qn_ref, k_ref, v_ref, z_ref, wob_ref, wib_ref, *, q_scale):
    first_step = (pl.program_id(0) == 0) & (pl.program_id(1) == 0)
    _cast_once(first_step, wo_ref, wob_ref)
    _cast_once(first_step, wi_ref, wib_ref)
    i = pl.program_id(1)
    n_i = pl.num_programs(1)
    tm = x_ref.shape[0]
    cx = cx_ref[...].astype(F32)
    prev_row = jnp.where(i > 0, cxp_ref[7:8, :].astype(F32), 0.0)
    next_row = jnp.where(i < n_i - 1, cxn_ref[0:1, :].astype(F32), 0.0)
    r = lax.broadcasted_iota(jnp.int32, cx.shape, 0)
    up = jnp.where(r == 0, prev_row, pltpu.roll(cx, 1, axis=0))
    dn = jnp.where(r == tm - 1, next_row, pltpu.roll(cx, tm - 1, axis=0))
    conv = up * cw_ref[0:1, :] + cx * cw_ref[1:2, :] + dn * cw_ref[2:3, :]
    y_a = ga_ref[...].astype(F32) * conv
    y_b = f_ref[...].astype(F32) * gb_ref[...].astype(F32)
    y = jnp.concatenate([y_a, y_b], axis=1).astype(BF16)
    qw = N_HEADS * HEAD_DIM
    kw = N_KV_HEADS * HEAD_DIM
    ts = tm // MID_SUBTILES
    even_lane = (lax.broadcasted_iota(jnp.int32, (ts, HEAD_DIM), 1) & 1) == 0
    ones_pad = (lax.broadcasted_iota(jnp.int32, (V_PAD_ROWS, ts), 0) == 0).astype(v_ref.dtype)
    for sub in range(MID_SUBTILES):
        rows = slice(sub * ts, (sub + 1) * ts)
        x1 = x_ref[rows, :] + jnp.dot(y[rows], wob_ref[...], preferred_element_type=F32)
        x1_ref[rows, :] = x1
        h = _rms_norm(x1, g_ref[...]).astype(BF16)
        p = jnp.dot(h, wib_ref[...], preferred_element_type=F32)
        cos = cos_ref[rows, :]
        sin = sin_ref[rows, :]
        for hh in range(N_HEADS):
            qh = _norm_rope(p[:, hh * HEAD_DIM:(hh + 1) * HEAD_DIM], qg_ref[...], cos, sin, even_lane)
            qt = (qh * q_scale).T.astype(q_ref.dtype)
            q_ref[0, hh, :, rows] = qt
            qf = qt.astype(F32)
            qn_ref[0, hh, :, rows] = jnp.sqrt(jnp.sum(qf * qf, axis=0, keepdims=True))
        for hh in range(N_KV_HEADS):
            kh = _norm_rope(p[:, qw + hh * HEAD_DIM:qw + (hh + 1) * HEAD_DIM], kg_ref[...], cos, sin, even_lane)
            k_ref[0, hh, rows, :] = kh.astype(k_ref.dtype)
            vt = p[:, qw + kw + hh * HEAD_DIM:qw + kw + (hh + 1) * HEAD_DIM].T.astype(v_ref.dtype)
            v_ref[0, hh, :, rows] = jnp.concatenate([vt, ones_pad], axis=0)
        z_ref[rows, :] = p[:, qw + 2 * kw:].astype(z_ref.dtype)


def _mid(x2, cx, ga, f2, gb, conv_w, w_out, g_odd, w_in, q_gain, k_gain, cos, sin, bsz, seq, tm, q_scale):
    n, d = x2.shape
    nt = seq // tm
    halo = 8
    tpb = tm // halo
    last_blk = n // halo - 1
    row = lambda b, i: (b * nt + i, 0)
    fixed = lambda b, i: (0, 0)
    prev = lambda b, i: (jnp.maximum((b * nt + i) * tpb - 1, 0), 0)
    nxt = lambda b, i: (jnp.minimum((b * nt + i + 1) * tpb, last_blk), 0)
    pos = lambda b, i: (i, 0)
    head = lambda b, i: (b, 0, i, 0)
    head_t = lambda b, i: (b, 0, 0, i)
    half = pl.BlockSpec((tm, A_WIDTH), row)
    return pl.pallas_call(
        functools.partial(_mid_kernel, q_scale=q_scale),
        grid=(bsz, nt),
        in_specs=[pl.BlockSpec((tm, d), row),
                  half,
                  pl.BlockSpec((halo, A_WIDTH), prev),
                  pl.BlockSpec((halo, A_WIDTH), nxt),
                  half, half, half,
                  pl.BlockSpec(conv_w.shape, fixed),
                  _resident(w_out.shape),
                  pl.BlockSpec((1, d), fixed),
                  _resident(w_in.shape),
                  pl.BlockSpec((1, HEAD_DIM), fixed),
                  pl.BlockSpec((1, HEAD_DIM), fixed),
                  pl.BlockSpec((tm, HEAD_DIM), pos),
                  pl.BlockSpec((tm, HEAD_DIM), pos)],
        out_specs=[pl.BlockSpec((tm, d), row),
                   pl.BlockSpec((1, N_HEADS, HEAD_DIM, tm), head_t),
                   pl.BlockSpec((1, N_HEADS, 1, tm), head_t),
                   pl.BlockSpec((1, N_KV_HEADS, tm, HEAD_DIM), head),
                   pl.BlockSpec((1, N_KV_HEADS, HEAD_DIM + V_PAD_ROWS, tm), head_t),
                   pl.BlockSpec((tm, N_HEADS * HEAD_DIM), row)],
        out_shape=[jax.ShapeDtypeStruct((n, d), F32),
                   jax.ShapeDtypeStruct((bsz, N_HEADS, HEAD_DIM, seq), BF16),
                   jax.ShapeDtypeStruct((bsz, N_HEADS, 1, seq), F32),
                   jax.ShapeDtypeStruct((bsz, N_KV_HEADS, seq, HEAD_DIM), BF16),
                   jax.ShapeDtypeStruct((bsz, N_KV_HEADS, HEAD_DIM + V_PAD_ROWS, seq), BF16),
                   jax.ShapeDtypeStruct((n, N_HEADS * HEAD_DIM), BF16)],
        scratch_shapes=[pltpu.VMEM(w_out.shape, BF16), pltpu.VMEM(w_in.shape, BF16)],
        compiler_params=_params(2),
        name="even_out_odd_in",
    )(x2, cx, cx, cx, ga, f2, gb, conv_w, w_out, g_odd, w_in, q_gain, k_gain, cos, sin)


def _attn_kernel(q_ref, qn_ref, qn_all_ref, k_ref, v_ref, o_ref, bound_ref, acc_ref, *, tk):
    _, grp, hd, tq = q_ref.shape
    seq = k_ref.shape[2]
    n_chunks = seq // tk
    v_rows = v_ref.shape[2]

    @pl.when(pl.program_id(2) == 0)
    def _norm_bounds():
        def body(j, best):
            kc = k_ref[0, 0, pl.ds(pl.multiple_of(j * tk, tk), tk), :].astype(F32)
            return jnp.maximum(best, jnp.sum(kc * kc, axis=1, keepdims=True))
        best = lax.fori_loop(0, n_chunks, body, jnp.zeros((tk, 1), F32))
        key_bound = jnp.sqrt(jnp.max(best)) * SHIFT_MARGIN
        bound_ref[0] = key_bound
        bound_ref[1] = jnp.max(qn_all_ref[0]) * key_bound

    def chunk(j):
        start = pl.multiple_of(j * tk, tk)
        return k_ref[0, 0, pl.ds(start, tk), :], v_ref[0, 0, :, pl.ds(start, tk)]

    bounded = bound_ref[1] <= SCORE_SPAN_LIMIT

    @pl.when(bounded)
    def _bounded_shift():
        q_all = jnp.concatenate([q_ref[0, g] for g in range(grp)], axis=1)
        shift = jnp.concatenate([qn_ref[0, g] for g in range(grp)], axis=1) * bound_ref[0]

        def body(j, acc):
            kc, vc = chunk(j)
            s = jnp.dot(kc, q_all, preferred_element_type=F32)
            p = jnp.exp2(s - shift).astype(BF16)
            return acc + jnp.dot(vc, p, preferred_element_type=F32)
        acc = lax.fori_loop(0, n_chunks, body, jnp.zeros((v_rows, grp * tq), F32), unroll=True)
        for g in range(grp):
            acc_ref[g] = acc[:, g * tq:(g + 1) * tq]

    @pl.when(jnp.logical_not(bounded))
    def _online_max():
        for g in range(grp):
            def body(j, carry):
                m, acc = carry
                kc, vc = chunk(j)
                s = jnp.dot(kc, q_ref[0, g], preferred_element_type=F32)
                m_new = jnp.maximum(m, jnp.max(s, axis=0, keepdims=True))
                p = jnp.exp2(s - m_new).astype(BF16)
                acc = jnp.exp2(m - m_new) * acc + jnp.dot(vc, p, preferred_element_type=F32)
                return m_new, acc
            init = (jnp.full((1, tq), -jnp.inf, F32), jnp.zeros((v_rows, tq), F32))
            _, acc = lax.fori_loop(0, n_chunks, body, init)
            acc_ref[g] = acc

    for g in range(grp):
        acc = acc_ref[g]
        o_t = acc[:hd] / acc[hd:hd + 1]
        o_ref[0, :, g * hd:(g + 1) * hd] = o_t.T.astype(o_ref.dtype)


def _attention(qt, qn, k, vt, tq, tk):
    bsz, nh, hd, seq = qt.shape
    nkv = k.shape[1]
    grp = nh // nkv
    v_rows = vt.shape[2]
    return pl.pallas_call(
        functools.partial(_attn_kernel, tk=tk),
        grid=(bsz, nkv, seq // tq),
        in_specs=[pl.BlockSpec((1, grp, hd, tq), lambda b, h, i: (b, h, 0, i)),
                  pl.BlockSpec((1, grp, 1, tq), lambda b, h, i: (b, h, 0, i)),
                  pl.BlockSpec((1, grp, 1, seq), lambda b, h, i: (b, h, 0, 0)),
                  pl.BlockSpec((1, 1, seq, hd), lambda b, h, i: (b, h, 0, 0)),
                  pl.BlockSpec((1, 1, v_rows, seq), lambda b, h, i: (b, h, 0, 0))],
        out_specs=pl.BlockSpec((1, tq, grp * hd), lambda b, h, i: (b, i, h)),
        out_shape=jax.ShapeDtypeStruct((bsz, seq, nh * hd), BF16),
        scratch_shapes=[pltpu.SMEM((2,), F32), pltpu.VMEM((grp, v_rows, tq), F32)],
        compiler_params=_params(3),
        name="gqa_attention",
    )(qt, qn, qn, k, vt)


def _odd_out_kernel(x1_ref, o_ref, z_ref, w_ref, g_ref, out_ref, wb_ref):
    _cast_once(pl.program_id(0) == 0, w_ref, wb_ref)
    y = (o_ref[...].astype(F32) * _silu(z_ref[...].astype(F32))).astype(BF16)
    x2 = x1_ref[...] + jnp.dot(y, wb_ref[...], preferred_element_type=F32)
    out_ref[...] = _rms_norm(x2, g_ref[...])


def _odd_out(x1, o2, z, w, g, tm):
    n, d = x1.shape
    row = lambda i: (i, 0)
    fixed = lambda i: (0, 0)
    return pl.pallas_call(
        _odd_out_kernel,
        grid=(n // tm,),
        in_specs=[pl.BlockSpec((tm, d), row),
                  pl.BlockSpec((tm, o2.shape[1]), row),
                  pl.BlockSpec((tm, z.shape[1]), row),
                  _resident(w.shape),
                  pl.BlockSpec((1, d), fixed)],
        out_specs=pl.BlockSpec((tm, d), row),
        out_shape=jax.ShapeDtypeStruct((n, d), F32),
        scratch_shapes=[pltpu.VMEM(w.shape, BF16)],
        compiler_params=_params(1),
        name="odd_out",
    )(x1, o2, z, w, g)


class _Tiles(NamedTuple):
    proj_rows: int
    mid_rows: int
    dft_rows: int
    queries: int
    keys: int


def _tiles(seq):
    assert seq % FFT_FAST == 0 and seq % GRID_W == 0, seq
    t = _Tiles(proj_rows=min(1024, seq), mid_rows=min(512, seq), dft_rows=min(DFT_ROWS_PER_STEP, seq // FFT_FAST),
               queries=min(512, seq), keys=min(512, seq))
    assert all(seq % v == 0 for v in (t.proj_rows, t.mid_rows, t.queries, t.keys)), (seq, t)
    assert (seq // FFT_FAST) % t.dft_rows == 0 and FFT_FAST % DFT_ROWS_PER_STEP == 0, (seq, t)
    return t


def kernel(x, norm_even, w_in_even, conv_w, w_out_even, norm_odd, w_in_odd, q_gain, k_gain, w_out_odd, final_norm):
    bsz, seq, d = x.shape
    assert norm_even.shape[0] == 1 and norm_odd.shape[0] == 1, "one even and one odd layer"
    n = bsz * seq
    slow = seq // FFT_FAST
    t = _tiles(seq)
    x2 = x.reshape(n, d)

    cx, ga, bu, gb = _even_in(x2, norm_even[0][None, :], w_in_even[0], t.proj_rows)
    w1, t2, wc = _dft_tables(seq)
    scale = 1.0 / math.sqrt(seq * B_GROUP_DIM)
    f = _fourier_mix(w1, t2, wc, bu.reshape(bsz, slow, FFT_FAST, B_WIDTH),
                     nb=DFT_ROWS_PER_STEP, cc=t.dft_rows, scale=scale)
    f2 = f.reshape(n, B_WIDTH)

    cos, sin = _rope_tables(seq)
    q_scale = HEAD_DIM ** -0.5 * math.log2(math.e)
    x1, qt, qn, k, vt, z = _mid(x2, cx, ga, f2, gb, conv_w[0], w_out_even[0],
                                 norm_odd[0][None, :], w_in_odd[0],
                                 q_gain[0][None, :], k_gain[0][None, :], cos, sin, bsz, seq, t.mid_rows, q_scale)

    o = _attention(qt, qn, k, vt, tq=t.queries, tk=t.keys)
    out = _odd_out(x1, o.reshape(n, N_HEADS * HEAD_DIM), z, w_out_odd[0], final_norm[None, :], t.proj_rows)
    return out.reshape(bsz, seq, d)
```

```python
import functools
import math
from typing import NamedTuple

import numpy as np
import jax
import jax.numpy as jnp
from jax import lax
from jax.experimental import pallas as pl
from jax.experimental.pallas import tpu as pltpu

F32 = jnp.float32
BF16 = jnp.bfloat16

EPS = 1e-6
GRID_W = 64
ROPE_THETA = 10000.0
CONV_WIDTH = 3
HEAD_DIM = 128
N_HEADS = 8
N_KV_HEADS = 2
KV_GROUP = N_HEADS // N_KV_HEADS
A_WIDTH = 512
B_WIDTH = 512
B_GROUPS = 4
B_GROUP_DIM = 128
FFT_FAST = 128
DFT_ROWS_PER_STEP = 16
V_PAD_ROWS = 16
SCORE_SPAN_LIMIT = 60.0
SHIFT_MARGIN = 1.01
MID_SUBTILES = 2

VMEM_LIMIT_BYTES = 56 * 1024 * 1024


def _silu(z):
    return z / (1.0 + jnp.exp(-z))


def _rms_norm(x, g):
    return x * lax.rsqrt(jnp.mean(x * x, axis=-1, keepdims=True) + EPS) * g


def _resident(shape):
    zeros = (0,) * len(shape)
    return pl.BlockSpec(shape, lambda *_: zeros, pipeline_mode=pl.Buffered(1))


def _cast_once(first_step, src_ref, dst_ref):
    @pl.when(first_step)
    def _():
        dst_ref[...] = src_ref[...].astype(dst_ref.dtype)


def _params(n_grid_dims):
    return pltpu.CompilerParams(
        dimension_semantics=("arbitrary",) * n_grid_dims,
        vmem_limit_bytes=VMEM_LIMIT_BYTES)


def _even_in_kernel(x_ref, g_ref, w_ref, cx_ref, ga_ref, bu_ref, gb_ref, wb_ref):
    _cast_once(pl.program_id(0) == 0, w_ref, wb_ref)
    h = _rms_norm(x_ref[...], g_ref[...]).astype(BF16)
    p = jnp.dot(h, wb_ref[...], preferred_element_type=F32)
    a_x = p[:, 0 * A_WIDTH:1 * A_WIDTH]
    a_b = p[:, 1 * A_WIDTH:2 * A_WIDTH]
    a_c = p[:, 2 * A_WIDTH:3 * A_WIDTH]
    a_z = p[:, 3 * A_WIDTH:4 * A_WIDTH]
    b_u = p[:, 4 * A_WIDTH:4 * A_WIDTH + B_WIDTH]
    b_z = p[:, 4 * A_WIDTH + B_WIDTH:]
    cx_ref[...] = (a_c * a_x).astype(cx_ref.dtype)
    ga_ref[...] = (a_b * _silu(a_z)).astype(ga_ref.dtype)
    bu_ref[...] = b_u.astype(bu_ref.dtype)
    gb_ref[...] = _silu(b_z).astype(gb_ref.dtype)


def _even_in(x2, g, w, tm):
    n, d = x2.shape
    e_in = w.shape[1]
    out = jax.ShapeDtypeStruct((n, A_WIDTH), BF16)
    row = lambda i: (i, 0)
    fixed = lambda i: (0, 0)
    return pl.pallas_call(
        _even_in_kernel,
        grid=(n // tm,),
        in_specs=[pl.BlockSpec((tm, d), row),
                  pl.BlockSpec((1, d), fixed),
                  _resident((d, e_in))],
        out_specs=[pl.BlockSpec((tm, A_WIDTH), row)] * 4,
        out_shape=[out] * 4,
        scratch_shapes=[pltpu.VMEM((d, e_in), BF16)],
        compiler_params=_params(1),
        name="even_in",
    )(x2, g, w)


def _dft_tables(seq):
    slow = seq // FFT_FAST
    a = np.arange(slow)
    th = 2.0 * np.pi * np.outer(a, a) / slow
    w1 = np.concatenate([np.cos(th), -np.sin(th)], axis=0)
    b = np.arange(FFT_FAST)
    k = a[:, None, None] + slow * b[None, :, None]
    th2 = 2.0 * np.pi * (k * b[None, None, :]) / seq
    cs, sn = np.cos(th2), np.sin(th2)
    t2 = np.concatenate([np.concatenate([cs, sn], axis=2),
                         np.concatenate([-sn, cs], axis=2)], axis=1)
    ch = np.arange(B_GROUP_DIM)
    thc = 2.0 * np.pi * np.outer(ch, ch) / B_GROUP_DIM
    wc = np.concatenate([np.cos(thc), np.sin(thc)], axis=0)
    return tuple(jnp.asarray(t, dtype=F32) for t in (w1, t2, wc))


def _dft_kernel(w1_ref, u_ref, t_ref, wc_ref, f_ref, y_ref, *, nb, cc, scale):
    _, slow, fast, width = u_ref.shape
    gd = B_GROUP_DIM

    @pl.when(pl.program_id(1) == 0)
    def _stage1():
        w1 = w1_ref[...].astype(BF16)

        def b_block(i, carry):
            cols = pl.ds(pl.multiple_of(i * nb, nb), nb)
            ut = jnp.transpose(u_ref[0, :, cols, :], (1, 0, 2))
            x = jnp.concatenate([ut[b] for b in range(nb)], axis=1)
            yb = jnp.dot(w1, x, preferred_element_type=F32).astype(y_ref.dtype)
            for part in range(2):
                rows = slice(part * slow, (part + 1) * slow)
                by_b = jnp.stack([yb[rows, b * width:(b + 1) * width] for b in range(nb)], axis=0)
                y_ref[part, :, cols, :] = jnp.transpose(by_b, (1, 0, 2))
            return carry
        lax.fori_loop(0, fast // nb, b_block, 0)

    wc = wc_ref[...].astype(BF16)
    c0 = pl.program_id(1) * cc
    by_c = []
    for ci in range(cc):
        ys = jnp.concatenate([y_ref[0, c0 + ci], y_ref[1, c0 + ci]], axis=0)
        p = jnp.dot(t_ref[ci].astype(BF16), ys, preferred_element_type=F32).astype(BF16)
        lhs = jnp.concatenate(
            [jnp.concatenate([p[:fast, g * gd:(g + 1) * gd], p[fast:, g * gd:(g + 1) * gd]], axis=1)
             for g in range(B_GROUPS)], axis=0)
        f = jnp.dot(lhs, wc, preferred_element_type=F32) * scale
        by_c.append(jnp.concatenate([f[g * fast:(g + 1) * fast] for g in range(B_GROUPS)],
                                    axis=1).astype(f_ref.dtype))
    f_ref[0] = jnp.transpose(jnp.stack(by_c, axis=0), (1, 0, 2))


def _fourier_mix(w1, t2, wc, u4, nb, cc, scale):
    bsz, slow, fast, width = u4.shape
    return pl.pallas_call(
        functools.partial(_dft_kernel, nb=nb, cc=cc, scale=scale),
        grid=(bsz, slow // cc),
        in_specs=[pl.BlockSpec(w1.shape, lambda b, j: (0, 0)),
                  pl.BlockSpec((1, slow, fast, width), lambda b, j: (b, 0, 0, 0)),
                  pl.BlockSpec((cc, 2 * fast, 2 * fast), lambda b, j: (j, 0, 0)),
                  pl.BlockSpec(wc.shape, lambda b, j: (0, 0))],
        out_specs=pl.BlockSpec((1, fast, cc, width), lambda b, j: (b, 0, j, 0)),
        out_shape=jax.ShapeDtypeStruct((bsz, fast, slow, width), BF16),
        scratch_shapes=[pltpu.VMEM((2, slow, fast, width), BF16)],
        compiler_params=_params(2),
        name="fourier_mix",
    )(w1, u4, t2, wc)


def _rope_tables(seq):
    t = np.arange(seq)
    row = (t // GRID_W).astype(np.float64)
    col = (t % GRID_W).astype(np.float64)
    n_pair = HEAD_DIM // 4
    inv = ROPE_THETA ** (-np.arange(n_pair, dtype=np.float64) / n_pair)
    ang = np.concatenate([row[:, None] * inv, col[:, None] * inv], axis=-1)
    cos = np.repeat(np.cos(ang), 2, axis=1)
    sin = np.repeat(np.sin(ang), 2, axis=1)
    sign = np.tile(np.array([-1.0, 1.0]), HEAD_DIM // 2)
    return jnp.asarray(cos, dtype=F32), jnp.asarray(sin * sign, dtype=F32)


def _norm_rope(xh, gain, cos, sin_signed, even_lane):
    xn = _rms_norm(xh, gain)
    partner = jnp.where(even_lane, pltpu.roll(xn, HEAD_DIM - 1, axis=1), pltpu.roll(xn, 1, axis=1))
    return xn * cos + partner * sin_signed


def _mid_kernel(x_ref, cx_ref, cxp_ref, cxn_ref, ga_ref, f_ref, gb_ref, cw_ref, wo_ref,
                g_ref, wi_ref, qg_ref, kg_ref, cos_ref, sin_ref,
                x1_ref, q_ref, k_ref, v_ref, z_ref, wob_ref, wib_ref, *, q_scale):
    first_step = (pl.program_id(0) == 0) & (pl.program_id(1) == 0)
    _cast_once(first_step, wo_ref, wob_ref)
    _cast_once(first_step, wi_ref, wib_ref)
    i = pl.program_id(1)
    n_i = pl.num_programs(1)
    tm = x_ref.shape[0]
    cx = cx_ref[...].astype(F32)
    prev_row = jnp.where(i > 0, cxp_ref[7:8, :].astype(F32), 0.0)
    next_row = jnp.where(i < n_i - 1, cxn_ref[0:1, :].astype(F32), 0.0)
    r = lax.broadcasted_iota(jnp.int32, cx.shape, 0)
    up = jnp.where(r == 0, prev_row, pltpu.roll(cx, 1, axis=0))
    dn = jnp.where(r == tm - 1, next_row, pltpu.roll(cx, tm - 1, axis=0))
    conv = up * cw_ref[0:1, :] + cx * cw_ref[1:2, :] + dn * cw_ref[2:3, :]
    y_a = ga_ref[...].astype(F32) * conv
    y_b = f_ref[...].astype(F32) * gb_ref[...].astype(F32)
    y = jnp.concatenate([y_a, y_b], axis=1).astype(BF16)
    qw = N_HEADS * HEAD_DIM
    kw = N_KV_HEADS * HEAD_DIM
    ts = tm // MID_SUBTILES
    even_lane = (lax.broadcasted_iota(jnp.int32, (ts, HEAD_DIM), 1) & 1) == 0
    ones_pad = (lax.broadcasted_iota(jnp.int32, (V_PAD_ROWS, ts), 0) == 0).astype(v_ref.dtype)
    for sub in range(MID_SUBTILES):
        rows = slice(sub * ts, (sub + 1) * ts)
        x1 = x_ref[rows, :] + jnp.dot(y[rows], wob_ref[...], preferred_element_type=F32)
        x1_ref[rows, :] = x1
        h = _rms_norm(x1, g_ref[...]).astype(BF16)
        p = jnp.dot(h, wib_ref[...], preferred_element_type=F32)
        cos = cos_ref[rows, :]
        sin = sin_ref[rows, :]
        for hh in range(N_HEADS):
            qh = _norm_rope(p[:, hh * HEAD_DIM:(hh + 1) * HEAD_DIM], qg_ref[...], cos, sin, even_lane)
            q_ref[0, hh, :, rows] = (qh * q_scale).T.astype(q_ref.dtype)
        for hh in range(N_KV_HEADS):
            kh = _norm_rope(p[:, qw + hh * HEAD_DIM:qw + (hh + 1) * HEAD_DIM], kg_ref[...], cos, sin, even_lane)
            k_ref[0, hh, rows, :] = kh.astype(k_ref.dtype)
            vt = p[:, qw + kw + hh * HEAD_DIM:qw + kw + (hh + 1) * HEAD_DIM].T.astype(v_ref.dtype)
            v_ref[0, hh, :, rows] = jnp.concatenate([vt, ones_pad], axis=0)
        z_ref[rows, :] = p[:, qw + 2 * kw:].astype(z_ref.dtype)


def _mid(x2, cx, ga, f2, gb, conv_w, w_out, g_odd, w_in, q_gain, k_gain, cos, sin, bsz, seq, tm, q_scale):
    n, d = x2.shape
    nt = seq // tm
    halo = 8
    tpb = tm // halo
    last_blk = n // halo - 1
    row = lambda b, i: (b * nt + i, 0)
    fixed = lambda b, i: (0, 0)
    prev = lambda b, i: (jnp.maximum((b * nt + i) * tpb - 1, 0), 0)
    nxt = lambda b, i: (jnp.minimum((b * nt + i + 1) * tpb, last_blk), 0)
    pos = lambda b, i: (i, 0)
    head = lambda b, i: (b, 0, i, 0)
    head_t = lambda b, i: (b, 0, 0, i)
    half = pl.BlockSpec((tm, A_WIDTH), row)
    return pl.pallas_call(
        functools.partial(_mid_kernel, q_scale=q_scale),
        grid=(bsz, nt),
        in_specs=[pl.BlockSpec((tm, d), row),
                  half,
                  pl.BlockSpec((halo, A_WIDTH), prev),
                  pl.BlockSpec((halo, A_WIDTH), nxt),
                  half, half, half,
                  pl.BlockSpec(conv_w.shape, fixed),
                  _resident(w_out.shape),
                  pl.BlockSpec((1, d), fixed),
                  _resident(w_in.shape),
                  pl.BlockSpec((1, HEAD_DIM), fixed),
                  pl.BlockSpec((1, HEAD_DIM), fixed),
                  pl.BlockSpec((tm, HEAD_DIM), pos),
                  pl.BlockSpec((tm, HEAD_DIM), pos)],
        out_specs=[pl.BlockSpec((tm, d), row),
                   pl.BlockSpec((1, N_HEADS, HEAD_DIM, tm), head_t),
                   pl.BlockSpec((1, N_KV_HEADS, tm, HEAD_DIM), head),
                   pl.BlockSpec((1, N_KV_HEADS, HEAD_DIM + V_PAD_ROWS, tm), head_t),
                   pl.BlockSpec((tm, N_HEADS * HEAD_DIM), row)],
        out_shape=[jax.ShapeDtypeStruct((n, d), F32),
                   jax.ShapeDtypeStruct((bsz, N_HEADS, HEAD_DIM, seq), BF16),
                   jax.ShapeDtypeStruct((bsz, N_KV_HEADS, seq, HEAD_DIM), BF16),
                   jax.ShapeDtypeStruct((bsz, N_KV_HEADS, HEAD_DIM + V_PAD_ROWS, seq), BF16),
                   jax.ShapeDtypeStruct((n, N_HEADS * HEAD_DIM), BF16)],
        scratch_shapes=[pltpu.VMEM(w_out.shape, BF16), pltpu.VMEM(w_in.shape, BF16)],
        compiler_params=_params(2),
        name="even_out_odd_in",
    )(x2, cx, cx, cx, ga, f2, gb, conv_w, w_out, g_odd, w_in, q_gain, k_gain, cos, sin)


def _attn_kernel(q_ref, k_ref, v_ref, qg_ref, kg_ref, o_ref, acc_ref, *, tk, q_scale):
    _, grp, hd, tq = q_ref.shape
    seq = k_ref.shape[2]
    n_chunks = seq // tk
    v_rows = v_ref.shape[2]
    shift = (hd * q_scale * SHIFT_MARGIN) * jnp.max(jnp.abs(qg_ref[...])) * jnp.max(jnp.abs(kg_ref[...]))

    def chunk(j):
        start = pl.multiple_of(j * tk, tk)
        return k_ref[0, 0, pl.ds(start, tk), :], v_ref[0, 0, :, pl.ds(start, tk)]

    bounded = shift <= SCORE_SPAN_LIMIT

    @pl.when(bounded)
    def _bounded_shift():
        q_all = jnp.concatenate([q_ref[0, g] for g in range(grp)], axis=1)

        def body(j, acc):
            kc, vc = chunk(j)
            s = jnp.dot(kc, q_all, preferred_element_type=F32)
            p = jnp.exp2(s - shift).astype(BF16)
            return acc + jnp.dot(vc, p, preferred_element_type=F32)
        acc = lax.fori_loop(0, n_chunks, body, jnp.zeros((v_rows, grp * tq), F32), unroll=True)
        for g in range(grp):
            acc_ref[g] = acc[:, g * tq:(g + 1) * tq]

    @pl.when(jnp.logical_not(bounded))
    def _online_max():
        for g in range(grp):
            def body(j, carry):
                m, acc = carry
                kc, vc = chunk(j)
                s = jnp.dot(kc, q_ref[0, g], preferred_element_type=F32)
                m_new = jnp.maximum(m, jnp.max(s, axis=0, keepdims=True))
                p = jnp.exp2(s - m_new).astype(BF16)
                acc = jnp.exp2(m - m_new) * acc + jnp.dot(vc, p, preferred_element_type=F32)
                return m_new, acc
            init = (jnp.full((1, tq), -jnp.inf, F32), jnp.zeros((v_rows, tq), F32))
            _, acc = lax.fori_loop(0, n_chunks, body, init)
            acc_ref[g] = acc

    for g in range(grp):
        acc = acc_ref[g]
        o_t = acc[:hd] / acc[hd:hd + 1]
        o_ref[0, :, g * hd:(g + 1) * hd] = o_t.T.astype(o_ref.dtype)


def _attention(qt, k, vt, q_gain, k_gain, tq, tk, q_scale):
    bsz, nh, hd, seq = qt.shape
    nkv = k.shape[1]
    grp = nh // nkv
    v_rows = vt.shape[2]
    return pl.pallas_call(
        functools.partial(_attn_kernel, tk=tk, q_scale=q_scale),
        grid=(bsz, nkv, seq // tq),
        in_specs=[pl.BlockSpec((1, grp, hd, tq), lambda b, h, i: (b, h, 0, i)),
                  pl.BlockSpec((1, 1, seq, hd), lambda b, h, i: (b, h, 0, 0)),
                  pl.BlockSpec((1, 1, v_rows, seq), lambda b, h, i: (b, h, 0, 0)),
                  pl.BlockSpec((1, hd), lambda b, h, i: (0, 0)),
                  pl.BlockSpec((1, hd), lambda b, h, i: (0, 0))],
        out_specs=pl.BlockSpec((1, tq, grp * hd), lambda b, h, i: (b, i, h)),
        out_shape=jax.ShapeDtypeStruct((bsz, seq, nh * hd), BF16),
        scratch_shapes=[pltpu.VMEM((grp, v_rows, tq), F32)],
        compiler_params=_params(3),
        name="gqa_attention",
    )(qt, k, vt, q_gain, k_gain)


def _odd_out_kernel(x1_ref, o_ref, z_ref, w_ref, g_ref, out_ref, wb_ref):
    _cast_once(pl.program_id(0) == 0, w_ref, wb_ref)
    y = (o_ref[...].astype(F32) * _silu(z_ref[...].astype(F32))).astype(BF16)
    x2 = x1_ref[...] + jnp.dot(y, wb_ref[...], preferred_element_type=F32)
    out_ref[...] = _rms_norm(x2, g_ref[...])


def _odd_out(x1, o2, z, w, g, tm):
    n, d = x1.shape
    row = lambda i: (i, 0)
    fixed = lambda i: (0, 0)
    return pl.pallas_call(
        _odd_out_kernel,
        grid=(n // tm,),
        in_specs=[pl.BlockSpec((tm, d), row),
                  pl.BlockSpec((tm, o2.shape[1]), row),
                  pl.BlockSpec((tm, z.shape[1]), row),
                  _resident(w.shape),
                  pl.BlockSpec((1, d), fixed)],
        out_specs=pl.BlockSpec((tm, d), row),
        out_shape=jax.ShapeDtypeStruct((n, d), F32),
        scratch_shapes=[pltpu.VMEM(w.shape, BF16)],
        compiler_params=_params(1),
        name="odd_out",
    )(x1, o2, z, w, g)


class _Tiles(NamedTuple):
    proj_rows: int
    mid_rows: int
    dft_rows: int
    queries: int
    keys: int


def _tiles(seq):
    assert seq % FFT_FAST == 0 and seq % GRID_W == 0, seq
    t = _Tiles(proj_rows=min(1024, seq), mid_rows=min(512, seq), dft_rows=min(DFT_ROWS_PER_STEP, seq // FFT_FAST),
               queries=min(512, seq), keys=min(512, seq))
    assert all(seq % v == 0 for v in (t.proj_rows, t.mid_rows, t.queries, t.keys)), (seq, t)
    assert (seq // FFT_FAST) % t.dft_rows == 0 and FFT_FAST % DFT_ROWS_PER_STEP == 0, (seq, t)
    return t


def kernel(x, norm_even, w_in_even, conv_w, w_out_even, norm_odd, w_in_odd, q_gain, k_gain, w_out_odd, final_norm):
    bsz, seq, d = x.shape
    assert norm_even.shape[0] == 1 and norm_odd.shape[0] == 1, "one even and one odd layer"
    n = bsz * seq
    slow = seq // FFT_FAST
    t = _tiles(seq)
    x2 = x.reshape(n, d)

    cx, ga, bu, gb = _even_in(x2, norm_even[0][None, :], w_in_even[0], t.proj_rows)
    w1, t2, wc = _dft_tables(seq)
    scale = 1.0 / math.sqrt(seq * B_GROUP_DIM)
    f = _fourier_mix(w1, t2, wc, bu.reshape(bsz, slow, FFT_FAST, B_WIDTH),
                     nb=DFT_ROWS_PER_STEP, cc=t.dft_rows, scale=scale)
    f2 = f.reshape(n, B_WIDTH)

    cos, sin = _rope_tables(seq)
    q_scale = HEAD_DIM ** -0.5 * math.log2(math.e)
    x1, qt, k, vt, z = _mid(x2, cx, ga, f2, gb, conv_w[0], w_out_even[0],
                                 norm_odd[0][None, :], w_in_odd[0],
                                 q_gain[0][None, :], k_gain[0][None, :], cos, sin, bsz, seq, t.mid_rows, q_scale)

    o = _attention(qt, k, vt, q_gain[0][None, :], k_gain[0][None, :], tq=t.queries, tk=t.keys, q_scale=q_scale)
    out = _odd_out(x1, o.reshape(n, N_HEADS * HEAD_DIM), z, w_out_odd[0], final_norm[None, :], t.proj_rows)
    return out.reshape(bsz, seq, d)
```

```python
import functools
import math
from typing import NamedTuple

import numpy as np
import jax
import jax.numpy as jnp
from jax import lax
from jax.experimental import pallas as pl
from jax.experimental.pallas import tpu as pltpu

F32 = jnp.float32
BF16 = jnp.bfloat16

EPS = 1e-6
GRID_W = 64
ROPE_THETA = 10000.0
CONV_WIDTH = 3
HEAD_DIM = 128
N_HEADS = 8
N_KV_HEADS = 2
KV_GROUP = N_HEADS // N_KV_HEADS
A_WIDTH = 512
B_WIDTH = 512
B_GROUPS = 4
B_GROUP_DIM = 128
FFT_FAST = 128
DFT_ROWS_PER_STEP = 16
V_PAD_ROWS = 16
SCORE_SPAN_LIMIT = 60.0
SHIFT_MARGIN = 1.01
MID_SUBTILES = 2

VMEM_LIMIT_BYTES = 56 * 1024 * 1024


def _silu(z):
    return z / (1.0 + jnp.exp(-z))


def _rms_norm(x, g):
    return x * lax.rsqrt(jnp.mean(x * x, axis=-1, keepdims=True) + EPS) * g


def _resident(shape):
    zeros = (0,) * len(shape)
    return pl.BlockSpec(shape, lambda *_: zeros, pipeline_mode=pl.Buffered(1))


def _cast_once(first_step, src_ref, dst_ref):
    @pl.when(first_step)
    def _():
        dst_ref[...] = src_ref[...].astype(dst_ref.dtype)


def _params(n_grid_dims):
    return pltpu.CompilerParams(
        dimension_semantics=("arbitrary",) * n_grid_dims,
        vmem_limit_bytes=VMEM_LIMIT_BYTES)


def _even_in_kernel(x_ref, g_ref, w_ref, cx_ref, ga_ref, bu_ref, gb_ref, wb_ref):
    _cast_once(pl.program_id(0) == 0, w_ref, wb_ref)
    h = _rms_norm(x_ref[...], g_ref[...]).astype(BF16)
    p = jnp.dot(h, wb_ref[...], preferred_element_type=F32)
    a_x = p[:, 0 * A_WIDTH:1 * A_WIDTH]
    a_b = p[:, 1 * A_WIDTH:2 * A_WIDTH]
    a_c = p[:, 2 * A_WIDTH:3 * A_WIDTH]
    a_z = p[:, 3 * A_WIDTH:4 * A_WIDTH]
    b_u = p[:, 4 * A_WIDTH:4 * A_WIDTH + B_WIDTH]
    b_z = p[:, 4 * A_WIDTH + B_WIDTH:]
    cx_ref[...] = (a_c * a_x).astype(cx_ref.dtype)
    ga_ref[...] = (a_b * _silu(a_z)).astype(ga_ref.dtype)
    bu_ref[...] = b_u.astype(bu_ref.dtype)
    gb_ref[...] = _silu(b_z).astype(gb_ref.dtype)


def _even_in(x2, g, w, tm):
    n, d = x2.shape
    e_in = w.shape[1]
    out = jax.ShapeDtypeStruct((n, A_WIDTH), BF16)
    row = lambda i: (i, 0)
    fixed = lambda i: (0, 0)
    return pl.pallas_call(
        _even_in_kernel,
        grid=(n // tm,),
        in_specs=[pl.BlockSpec((tm, d), row),
                  pl.BlockSpec((1, d), fixed),
                  _resident((d, e_in))],
        out_specs=[pl.BlockSpec((tm, A_WIDTH), row)] * 4,
        out_shape=[out] * 4,
        scratch_shapes=[pltpu.VMEM((d, e_in), BF16)],
        compiler_params=_params(1),
        name="even_in",
    )(x2, g, w)


def _dft_tables(seq):
    slow = seq // FFT_FAST
    a = np.arange(slow)
    th = 2.0 * np.pi * np.outer(a, a) / slow
    w1 = np.concatenate([np.cos(th), -np.sin(th)], axis=0)
    b = np.arange(FFT_FAST)
    k = a[:, None, None] + slow * b[None, :, None]
    th2 = 2.0 * np.pi * (k * b[None, None, :]) / seq
    cs, sn = np.cos(th2), np.sin(th2)
    t2 = np.concatenate([np.concatenate([cs, sn], axis=2),
                         np.concatenate([-sn, cs], axis=2)], axis=1)
    ch = np.arange(B_GROUP_DIM)
    thc = 2.0 * np.pi * np.outer(ch, ch) / B_GROUP_DIM
    wc = np.concatenate([np.cos(thc), np.sin(thc)], axis=0)
    return tuple(jnp.asarray(t, dtype=F32) for t in (w1, t2, wc))


def _dft_kernel(w1_ref, u_ref, t_ref, wc_ref, f_ref, y_ref, *, nb, cc, scale):
    _, slow, fast, width = u_ref.shape
    gd = B_GROUP_DIM

    @pl.when(pl.program_id(1) == 0)
    def _stage1():
        w1 = w1_ref[...].astype(BF16)

        def b_block(i, carry):
            cols = pl.ds(pl.multiple_of(i * nb, nb), nb)
            ut = jnp.transpose(u_ref[0, :, cols, :], (1, 0, 2))
            x = jnp.concatenate([ut[b] for b in range(nb)], axis=1)
            yb = jnp.dot(w1, x, preferred_element_type=F32).astype(y_ref.dtype)
            for part in range(2):
                rows = slice(part * slow, (part + 1) * slow)
                by_b = jnp.stack([yb[rows, b * width:(b + 1) * width] for b in range(nb)], axis=0)
                y_ref[part, :, cols, :] = jnp.transpose(by_b, (1, 0, 2))
            return carry
        lax.fori_loop(0, fast // nb, b_block, 0)

    wc = wc_ref[...].astype(BF16)
    c0 = pl.program_id(1) * cc
    by_c = []
    for ci in range(cc):
        ys = jnp.concatenate([y_ref[0, c0 + ci], y_ref[1, c0 + ci]], axis=0)
        p = jnp.dot(t_ref[ci].astype(BF16), ys, preferred_element_type=F32).astype(BF16)
        lhs = jnp.concatenate(
            [jnp.concatenate([p[:fast, g * gd:(g + 1) * gd], p[fast:, g * gd:(g + 1) * gd]], axis=1)
             for g in range(B_GROUPS)], axis=0)
        f = jnp.dot(lhs, wc, preferred_element_type=F32) * scale
        by_c.append(jnp.concatenate([f[g * fast:(g + 1) * fast] for g in range(B_GROUPS)],
                                    axis=1).astype(f_ref.dtype))
    f_ref[0] = jnp.transpose(jnp.stack(by_c, axis=0), (1, 0, 2))


def _fourier_mix(w1, t2, wc, u4, nb, cc, scale):
    bsz, slow, fast, width = u4.shape
    return pl.pallas_call(
        functools.partial(_dft_kernel, nb=nb, cc=cc, scale=scale),
        grid=(bsz, slow // cc),
        in_specs=[pl.BlockSpec(w1.shape, lambda b, j: (0, 0)),
                  pl.BlockSpec((1, slow, fast, width), lambda b, j: (b, 0, 0, 0)),
                  pl.BlockSpec((cc, 2 * fast, 2 * fast), lambda b, j: (j, 0, 0)),
                  pl.BlockSpec(wc.shape, lambda b, j: (0, 0))],
        out_specs=pl.BlockSpec((1, fast, cc, width), lambda b, j: (b, 0, j, 0)),
        out_shape=jax.ShapeDtypeStruct((bsz, fast, slow, width), BF16),
        scratch_shapes=[pltpu.VMEM((2, slow, fast, width), BF16)],
        compiler_params=_params(2),
        name="fourier_mix",
    )(w1, u4, t2, wc)


def _rope_tables(seq):
    t = np.arange(seq)
    row = (t // GRID_W).astype(np.float64)
    col = (t % GRID_W).astype(np.float64)
    n_pair = HEAD_DIM // 4
    inv = ROPE_THETA ** (-np.arange(n_pair, dtype=np.float64) / n_pair)
    ang = np.concatenate([row[:, None] * inv, col[:, None] * inv], axis=-1)
    cos = np.repeat(np.cos(ang), 2, axis=1)
    sin = np.repeat(np.sin(ang), 2, axis=1)
    sign = np.tile(np.array([-1.0, 1.0]), HEAD_DIM // 2)
    return jnp.asarray(cos, dtype=F32), jnp.asarray(sin * sign, dtype=F32)


def _norm_rope(xh, gain, cos, sin_signed, even_lane):
    xn = _rms_norm(xh, gain)
    partner = jnp.where(even_lane, pltpu.roll(xn, HEAD_DIM - 1, axis=1), pltpu.roll(xn, 1, axis=1))
    return xn * cos + partner * sin_signed


def _mid_kernel(x_ref, cx_ref, cxp_ref, cxn_ref, ga_ref, f_ref, gb_ref, cw_ref, wo_ref,
                g_ref, wi_ref, qg_ref, kg_ref, cos_ref, sin_ref,
                x1_ref, q_ref, k_ref, v_ref, z_ref, wob_ref, wib_ref, *, q_scale):
    first_step = (pl.program_id(0) == 0) & (pl.program_id(1) == 0)
    _cast_once(first_step, wo_ref, wob_ref)
    _cast_once(first_step, wi_ref, wib_ref)
    i = pl.program_id(1)
    n_i = pl.num_programs(1)
    tm = x_ref.shape[0]
    cx = cx_ref[...].astype(F32)
    prev_row = jnp.where(i > 0, cxp_ref[7:8, :].astype(F32), 0.0)
    next_row = jnp.where(i < n_i - 1, cxn_ref[0:1, :].astype(F32), 0.0)
    r = lax.broadcasted_iota(jnp.int32, cx.shape, 0)
    up = jnp.where(r == 0, prev_row, pltpu.roll(cx, 1, axis=0))
    dn = jnp.where(r == tm - 1, next_row, pltpu.roll(cx, tm - 1, axis=0))
    conv = up * cw_ref[0:1, :] + cx * cw_ref[1:2, :] + dn * cw_ref[2:3, :]
    y_a = ga_ref[...].astype(F32) * conv
    y_b = f_ref[...].astype(F32) * gb_ref[...].astype(F32)
    y = jnp.concatenate([y_a, y_b], axis=1).astype(BF16)
    qw = N_HEADS * HEAD_DIM
    kw = N_KV_HEADS * HEAD_DIM
    ts = tm // MID_SUBTILES
    even_lane = (lax.broadcasted_iota(jnp.int32, (ts, HEAD_DIM), 1) & 1) == 0
    ones_pad = (lax.broadcasted_iota(jnp.int32, (V_PAD_ROWS, ts), 0) == 0).astype(v_ref.dtype)
    q_gain = qg_ref[...] * q_scale
    for sub in range(MID_SUBTILES):
        rows = slice(sub * ts, (sub + 1) * ts)
        x1 = x_ref[rows, :] + jnp.dot(y[rows], wob_ref[...], preferred_element_type=F32)
        x1_ref[rows, :] = x1
        h = _rms_norm(x1, g_ref[...]).astype(BF16)
        p = jnp.dot(h, wib_ref[...], preferred_element_type=F32)
        cos = cos_ref[rows, :]
        sin = sin_ref[rows, :]
        for hh in range(N_HEADS):
            qh = _norm_rope(p[:, hh * HEAD_DIM:(hh + 1) * HEAD_DIM], q_gain, cos, sin, even_lane)
            q_ref[0, hh, :, rows] = qh.T.astype(q_ref.dtype)
        for hh in range(N_KV_HEADS):
            kh = _norm_rope(p[:, qw + hh * HEAD_DIM:qw + (hh + 1) * HEAD_DIM], kg_ref[...], cos, sin, even_lane)
            k_ref[0, hh, rows, :] = kh.astype(k_ref.dtype)
            vt = p[:, qw + kw + hh * HEAD_DIM:qw + kw + (hh + 1) * HEAD_DIM].T.astype(v_ref.dtype)
            v_ref[0, hh, :, rows] = jnp.concatenate([vt, ones_pad], axis=0)
        z_ref[rows, :] = p[:, qw + 2 * kw:].astype(z_ref.dtype)


def _mid(x2, cx, ga, f2, gb, conv_w, w_out, g_odd, w_in, q_gain, k_gain, cos, sin, bsz, seq, tm, q_scale):
    n, d = x2.shape
    nt = seq // tm
    halo = 8
    tpb = tm // halo
    last_blk = n // halo - 1
    row = lambda b, i: (b * nt + i, 0)
    fixed = lambda b, i: (0, 0)
    prev = lambda b, i: (jnp.maximum((b * nt + i) * tpb - 1, 0), 0)
    nxt = lambda b, i: (jnp.minimum((b * nt + i + 1) * tpb, last_blk), 0)
    pos = lambda b, i: (i, 0)
    head = lambda b, i: (b, 0, i, 0)
    head_t = lambda b, i: (b, 0, 0, i)
    half = pl.BlockSpec((tm, A_WIDTH), row)
    return pl.pallas_call(
        functools.partial(_mid_kernel, q_scale=q_scale),
        grid=(bsz, nt),
        in_specs=[pl.BlockSpec((tm, d), row),
                  half,
                  pl.BlockSpec((halo, A_WIDTH), prev),
                  pl.BlockSpec((halo, A_WIDTH), nxt),
                  half, half, half,
                  pl.BlockSpec(conv_w.shape, fixed),
                  _resident(w_out.shape),
                  pl.BlockSpec((1, d), fixed),
                  _resident(w_in.shape),
                  pl.BlockSpec((1, HEAD_DIM), fixed),
                  pl.BlockSpec((1, HEAD_DIM), fixed),
                  pl.BlockSpec((tm, HEAD_DIM), pos),
                  pl.BlockSpec((tm, HEAD_DIM), pos)],
        out_specs=[pl.BlockSpec((tm, d), row),
                   pl.BlockSpec((1, N_HEADS, HEAD_DIM, tm), head_t),
                   pl.BlockSpec((1, N_KV_HEADS, tm, HEAD_DIM), head),
                   pl.BlockSpec((1, N_KV_HEADS, HEAD_DIM + V_PAD_ROWS, tm), head_t),
                   pl.BlockSpec((tm, N_HEADS * HEAD_DIM), row)],
        out_shape=[jax.ShapeDtypeStruct((n, d), F32),
                   jax.ShapeDtypeStruct((bsz, N_HEADS, HEAD_DIM, seq), BF16),
                   jax.ShapeDtypeStruct((bsz, N_KV_HEADS, seq, HEAD_DIM), BF16),
                   jax.ShapeDtypeStruct((bsz, N_KV_HEADS, HEAD_DIM + V_PAD_ROWS, seq), BF16),
                   jax.ShapeDtypeStruct((n, N_HEADS * HEAD_DIM), BF16)],
        scratch_shapes=[pltpu.VMEM(w_out.shape, BF16), pltpu.VMEM(w_in.shape, BF16)],
        compiler_params=_params(2),
        name="even_out_odd_in",
    )(x2, cx, cx, cx, ga, f2, gb, conv_w, w_out, g_odd, w_in, q_gain, k_gain, cos, sin)


def _attn_kernel(q_ref, k_ref, v_ref, qg_ref, kg_ref, o_ref, acc_ref, *, tk, q_scale):
    _, grp, hd, tq = q_ref.shape
    seq = k_ref.shape[2]
    n_chunks = seq // tk
    v_rows = v_ref.shape[2]
    shift = (hd * q_scale * SHIFT_MARGIN) * jnp.max(jnp.abs(qg_ref[...])) * jnp.max(jnp.abs(kg_ref[...]))

    def chunk(j):
        start = pl.multiple_of(j * tk, tk)
        return k_ref[0, 0, pl.ds(start, tk), :], v_ref[0, 0, :, pl.ds(start, tk)]

    bounded = shift <= SCORE_SPAN_LIMIT

    @pl.when(bounded)
    def _bounded_shift():
        q_all = jnp.concatenate([q_ref[0, g] for g in range(grp)], axis=1)

        def body(j, acc):
            kc, vc = chunk(j)
            s = jnp.dot(kc, q_all, preferred_element_type=F32)
            p = jnp.exp2(s - shift).astype(BF16)
            return acc + jnp.dot(vc, p, preferred_element_type=F32)
        acc = lax.fori_loop(0, n_chunks, body, jnp.zeros((v_rows, grp * tq), F32), unroll=True)
        for g in range(grp):
            acc_ref[g] = acc[:, g * tq:(g + 1) * tq]

    @pl.when(jnp.logical_not(bounded))
    def _online_max():
        for g in range(grp):
            def body(j, carry):
                m, acc = carry
                kc, vc = chunk(j)
                s = jnp.dot(kc, q_ref[0, g], preferred_element_type=F32)
                m_new = jnp.maximum(m, jnp.max(s, axis=0, keepdims=True))
                p = jnp.exp2(s - m_new).astype(BF16)
                acc = jnp.exp2(m - m_new) * acc + jnp.dot(vc, p, preferred_element_type=F32)
                return m_new, acc
            init = (jnp.full((1, tq), -jnp.inf, F32), jnp.zeros((v_rows, tq), F32))
            _, acc = lax.fori_loop(0, n_chunks, body, init)
            acc_ref[g] = acc

    for g in range(grp):
        acc = acc_ref[g]
        o_t = acc[:hd] / acc[hd:hd + 1]
        o_ref[0, :, g * hd:(g + 1) * hd] = o_t.T.astype(o_ref.dtype)


def _attention(qt, k, vt, q_gain, k_gain, tq, tk, q_scale):
    bsz, nh, hd, seq = qt.shape
    nkv = k.shape[1]
    grp = nh // nkv
    v_rows = vt.shape[2]
    return pl.pallas_call(
        functools.partial(_attn_kernel, tk=tk, q_scale=q_scale),
        grid=(bsz, nkv, seq // tq),
        in_specs=[pl.BlockSpec((1, grp, hd, tq), lambda b, h, i: (b, h, 0, i)),
                  pl.BlockSpec((1, 1, seq, hd), lambda b, h, i: (b, h, 0, 0)),
                  pl.BlockSpec((1, 1, v_rows, seq), lambda b, h, i: (b, h, 0, 0)),
                  pl.BlockSpec((1, hd), lambda b, h, i: (0, 0)),
                  pl.BlockSpec((1, hd), lambda b, h, i: (0, 0))],
        out_specs=pl.BlockSpec((1, tq, grp * hd), lambda b, h, i: (b, i, h)),
        out_shape=jax.ShapeDtypeStruct((bsz, seq, nh * hd), BF16),
        scratch_shapes=[pltpu.VMEM((grp, v_rows, tq), F32)],
        compiler_params=_params(3),
        name="gqa_attention",
    )(qt, k, vt, q_gain, k_gain)


def _odd_out_kernel(x1_ref, o_ref, z_ref, w_ref, g_ref, out_ref, wb_ref):
    _cast_once(pl.program_id(0) == 0, w_ref, wb_ref)
    y = (o_ref[...].astype(F32) * _silu(z_ref[...].astype(F32))).astype(BF16)
    x2 = x1_ref[...] + jnp.dot(y, wb_ref[...], preferred_element_type=F32)
    out_ref[...] = _rms_norm(x2, g_ref[...])


def _odd_out(x1, o2, z, w, g, tm):
    n, d = x1.shape
    row = lambda i: (i, 0)
    fixed = lambda i: (0, 0)
    return pl.pallas_call(
        _odd_out_kernel,
        grid=(n // tm,),
        in_specs=[pl.BlockSpec((tm, d), row),
                  pl.BlockSpec((tm, o2.shape[1]), row),
                  pl.BlockSpec((tm, z.shape[1]), row),
                  _resident(w.shape),
                  pl.BlockSpec((1, d), fixed)],
        out_specs=pl.BlockSpec((tm, d), row),
        out_shape=jax.ShapeDtypeStruct((n, d), F32),
        scratch_shapes=[pltpu.VMEM(w.shape, BF16)],
        compiler_params=_params(1),
        name="odd_out",
    )(x1, o2, z, w, g)


class _Tiles(NamedTuple):
    proj_rows: int
    mid_rows: int
    dft_rows: int
    queries: int
    keys: int


def _tiles(seq):
    assert seq % FFT_FAST == 0 and seq % GRID_W == 0, seq
    t = _Tiles(proj_rows=min(1024, seq), mid_rows=min(512, seq), dft_rows=min(DFT_ROWS_PER_STEP, seq // FFT_FAST),
               queries=min(512, seq), keys=min(512, seq))
    assert all(seq % v == 0 for v in (t.proj_rows, t.mid_rows, t.queries, t.keys)), (seq, t)
    assert (seq // FFT_FAST) % t.dft_rows == 0 and FFT_FAST % DFT_ROWS_PER_STEP == 0, (seq, t)
    return t


def kernel(x, norm_even, w_in_even, conv_w, w_out_even, norm_odd, w_in_odd, q_gain, k_gain, w_out_odd, final_norm):
    bsz, seq, d = x.shape
    assert norm_even.shape[0] == 1 and norm_odd.shape[0] == 1, "one even and one odd layer"
    n = bsz * seq
    slow = seq // FFT_FAST
    t = _tiles(seq)
    x2 = x.reshape(n, d)

    cx, ga, bu, gb = _even_in(x2, norm_even[0][None, :], w_in_even[0], t.proj_rows)
    w1, t2, wc = _dft_tables(seq)
    scale = 1.0 / math.sqrt(seq * B_GROUP_DIM)
    f = _fourier_mix(w1, t2, wc, bu.reshape(bsz, slow, FFT_FAST, B_WIDTH),
                     nb=DFT_ROWS_PER_STEP, cc=t.dft_rows, scale=scale)
    f2 = f.reshape(n, B_WIDTH)

    cos, sin = _rope_tables(seq)
    q_scale = HEAD_DIM ** -0.5 * math.log2(math.e)
    x1, qt, k, vt, z = _mid(x2, cx, ga, f2, gb, conv_w[0], w_out_even[0],
                                 norm_odd[0][None, :], w_in_odd[0],
                                 q_gain[0][None, :], k_gain[0][None, :], cos, sin, bsz, seq, t.mid_rows, q_scale)

    o = _attention(qt, k, vt, q_gain[0][None, :], k_gain[0][None, :], tq=t.queries, tk=t.keys, q_scale=q_scale)
    out = _odd_out(x1, o.reshape(n, N_HEADS * HEAD_DIM), z, w_out_odd[0], final_norm[None, :], t.proj_rows)
    return out.reshape(bsz, seq, d)
```

```python
import functools
import math
from typing import NamedTuple

import numpy as np
import jax
import jax.numpy as jnp
from jax import lax
from jax.experimental import pallas as pl
from jax.experimental.pallas import tpu as pltpu

F32 = jnp.float32
BF16 = jnp.bfloat16

EPS = 1e-6
GRID_W = 64
ROPE_THETA = 10000.0
CONV_WIDTH = 3
HEAD_DIM = 128
N_HEADS = 8
N_KV_HEADS = 2
A_WIDTH = 512
B_WIDTH = 512
B_GROUPS = 4
B_GROUP_DIM = 128
FFT_FAST = 128
DFT_ROWS_PER_STEP = 16
V_PAD_ROWS = 16
SCORE_SPAN_LIMIT = 60.0
SHIFT_MARGIN = 1.01
MID_SUBTILES = 2

VMEM_LIMIT_BYTES = 56 * 1024 * 1024


def _silu(z):
    return z / (1.0 + jnp.exp(-z))


def _rms_norm(x, g):
    return x * lax.rsqrt(jnp.mean(x * x, axis=-1, keepdims=True) + EPS) * g


def _resident(shape):
    zeros = (0,) * len(shape)
    return pl.BlockSpec(shape, lambda *_: zeros, pipeline_mode=pl.Buffered(1))


def _cast_once(first_step, src_ref, dst_ref):
    @pl.when(first_step)
    def _():
        dst_ref[...] = src_ref[...].astype(dst_ref.dtype)


def _params(n_grid_dims):
    return pltpu.CompilerParams(
        dimension_semantics=("arbitrary",) * n_grid_dims,
        vmem_limit_bytes=VMEM_LIMIT_BYTES)


def _even_in_kernel(x_ref, g_ref, w_ref, cx_ref, ga_ref, bu_ref, gb_ref, wb_ref):
    _cast_once(pl.program_id(0) == 0, w_ref, wb_ref)
    h = _rms_norm(x_ref[...], g_ref[...]).astype(BF16)
    p = jnp.dot(h, wb_ref[...], preferred_element_type=F32)
    a_x = p[:, 0 * A_WIDTH:1 * A_WIDTH]
    a_b = p[:, 1 * A_WIDTH:2 * A_WIDTH]
    a_c = p[:, 2 * A_WIDTH:3 * A_WIDTH]
    a_z = p[:, 3 * A_WIDTH:4 * A_WIDTH]
    b_u = p[:, 4 * A_WIDTH:4 * A_WIDTH + B_WIDTH]
    b_z = p[:, 4 * A_WIDTH + B_WIDTH:]
    cx_ref[...] = (a_c * a_x).astype(cx_ref.dtype)
    ga_ref[...] = (a_b * _silu(a_z)).astype(ga_ref.dtype)
    bu_ref[...] = b_u.astype(bu_ref.dtype)
    gb_ref[...] = _silu(b_z).astype(gb_ref.dtype)


def _even_in(x2, g, w, tm):
    n, d = x2.shape
    e_in = w.shape[1]
    out = jax.ShapeDtypeStruct((n, A_WIDTH), BF16)
    row = lambda i: (i, 0)
    fixed = lambda i: (0, 0)
    return pl.pallas_call(
        _even_in_kernel,
        grid=(n // tm,),
        in_specs=[pl.BlockSpec((tm, d), row),
                  pl.BlockSpec((1, d), fixed),
                  _resident((d, e_in))],
        out_specs=[pl.BlockSpec((tm, A_WIDTH), row)] * 4,
        out_shape=[out] * 4,
        scratch_shapes=[pltpu.VMEM((d, e_in), BF16)],
        compiler_params=_params(1),
        name="even_in",
    )(x2, g, w)


def _dft_tables(seq):
    slow = seq // FFT_FAST
    a = np.arange(slow)
    th = 2.0 * np.pi * np.outer(a, a) / slow
    w1 = np.concatenate([np.cos(th), -np.sin(th)], axis=0)
    b = np.arange(FFT_FAST)
    k = a[:, None, None] + slow * b[None, :, None]
    th2 = 2.0 * np.pi * (k * b[None, None, :]) / seq
    cs, sn = np.cos(th2), np.sin(th2)
    t2 = np.concatenate([np.concatenate([cs, sn], axis=2),
                         np.concatenate([-sn, cs], axis=2)], axis=1)
    ch = np.arange(B_GROUP_DIM)
    thc = 2.0 * np.pi * np.outer(ch, ch) / B_GROUP_DIM
    wc = np.concatenate([np.cos(thc), np.sin(thc)], axis=0)
    return tuple(jnp.asarray(t, dtype=F32) for t in (w1, t2, wc))


def _dft_kernel(w1_ref, u_ref, t_ref, wc_ref, f_ref, y_ref, *, nb, cc, scale):
    _, slow, fast, width = u_ref.shape
    gd = B_GROUP_DIM

    @pl.when(pl.program_id(1) == 0)
    def _stage1():
        w1 = w1_ref[...].astype(BF16)

        def b_block(i, carry):
            cols = pl.ds(pl.multiple_of(i * nb, nb), nb)
            ut = jnp.transpose(u_ref[0, :, cols, :], (1, 0, 2))
            x = jnp.concatenate([ut[b] for b in range(nb)], axis=1)
            yb = jnp.dot(w1, x, preferred_element_type=F32).astype(y_ref.dtype)
            for part in range(2):
                rows = slice(part * slow, (part + 1) * slow)
                by_b = jnp.stack([yb[rows, b * width:(b + 1) * width] for b in range(nb)], axis=0)
                y_ref[part, :, cols, :] = jnp.transpose(by_b, (1, 0, 2))
            return carry
        lax.fori_loop(0, fast // nb, b_block, 0)

    wc = wc_ref[...].astype(BF16)
    c0 = pl.program_id(1) * cc
    by_c = []
    for ci in range(cc):
        ys = jnp.concatenate([y_ref[0, c0 + ci], y_ref[1, c0 + ci]], axis=0)
        p = jnp.dot(t_ref[ci].astype(BF16), ys, preferred_element_type=F32).astype(BF16)
        lhs = jnp.concatenate(
            [jnp.concatenate([p[:fast, g * gd:(g + 1) * gd], p[fast:, g * gd:(g + 1) * gd]], axis=1)
             for g in range(B_GROUPS)], axis=0)
        f = jnp.dot(lhs, wc, preferred_element_type=F32) * scale
        by_c.append(jnp.concatenate([f[g * fast:(g + 1) * fast] for g in range(B_GROUPS)],
                                    axis=1).astype(f_ref.dtype))
    f_ref[0] = jnp.transpose(jnp.stack(by_c, axis=0), (1, 0, 2))


def _fourier_mix(w1, t2, wc, u4, nb, cc, scale):
    bsz, slow, fast, width = u4.shape
    return pl.pallas_call(
        functools.partial(_dft_kernel, nb=nb, cc=cc, scale=scale),
        grid=(bsz, slow // cc),
        in_specs=[pl.BlockSpec(w1.shape, lambda b, j: (0, 0)),
                  pl.BlockSpec((1, slow, fast, width), lambda b, j: (b, 0, 0, 0)),
                  pl.BlockSpec((cc, 2 * fast, 2 * fast), lambda b, j: (j, 0, 0)),
                  pl.BlockSpec(wc.shape, lambda b, j: (0, 0))],
        out_specs=pl.BlockSpec((1, fast, cc, width), lambda b, j: (b, 0, j, 0)),
        out_shape=jax.ShapeDtypeStruct((bsz, fast, slow, width), BF16),
        scratch_shapes=[pltpu.VMEM((2, slow, fast, width), BF16)],
        compiler_params=_params(2),
        name="fourier_mix",
    )(w1, u4, t2, wc)


def _rope_tables(seq):
    t = np.arange(seq)
    row = (t // GRID_W).astype(np.float64)
    col = (t % GRID_W).astype(np.float64)
    n_pair = HEAD_DIM // 4
    inv = ROPE_THETA ** (-np.arange(n_pair, dtype=np.float64) / n_pair)
    ang = np.concatenate([row[:, None] * inv, col[:, None] * inv], axis=-1)
    cos = np.repeat(np.cos(ang), 2, axis=1)
    sin = np.repeat(np.sin(ang), 2, axis=1)
    sign = np.tile(np.array([-1.0, 1.0]), HEAD_DIM // 2)
    return jnp.asarray(cos, dtype=F32), jnp.asarray(sin * sign, dtype=F32)


def _norm_rope(xh, gain, cos, sin_signed, even_lane):
    xn = _rms_norm(xh, gain)
    partner = jnp.where(even_lane, pltpu.roll(xn, HEAD_DIM - 1, axis=1), pltpu.roll(xn, 1, axis=1))
    return xn * cos + partner * sin_signed


def _mid_kernel(x_ref, cx_ref, cxp_ref, cxn_ref, ga_ref, f_ref, gb_ref, cw_ref, wo_ref,
                g_ref, wi_ref, qg_ref, kg_ref, cos_ref, sin_ref,
                x1_ref, q_ref, k_ref, v_ref, z_ref, wob_ref, wib_ref, *, q_scale):
    first_step = (pl.program_id(0) == 0) & (pl.program_id(1) == 0)
    _cast_once(first_step, wo_ref, wob_ref)
    _cast_once(first_step, wi_ref, wib_ref)
    i = pl.program_id(1)
    n_i = pl.num_programs(1)
    tm = x_ref.shape[0]
    cx = cx_ref[...].astype(F32)
    prev_row = jnp.where(i > 0, cxp_ref[7:8, :].astype(F32), 0.0)
    next_row = jnp.where(i < n_i - 1, cxn_ref[0:1, :].astype(F32), 0.0)
    r = lax.broadcasted_iota(jnp.int32, cx.shape, 0)
    up = jnp.where(r == 0, prev_row, pltpu.roll(cx, 1, axis=0))
    dn = jnp.where(r == tm - 1, next_row, pltpu.roll(cx, tm - 1, axis=0))
    conv = up * cw_ref[0:1, :] + cx * cw_ref[1:2, :] + dn * cw_ref[2:3, :]
    y_a = ga_ref[...].astype(F32) * conv
    y_b = f_ref[...].astype(F32) * gb_ref[...].astype(F32)
    y = jnp.concatenate([y_a, y_b], axis=1).astype(BF16)
    qw = N_HEADS * HEAD_DIM
    kw = N_KV_HEADS * HEAD_DIM
    ts = tm // MID_SUBTILES
    even_lane = (lax.broadcasted_iota(jnp.int32, (ts, HEAD_DIM), 1) & 1) == 0
    ones_pad = (lax.broadcasted_iota(jnp.int32, (V_PAD_ROWS, ts), 0) == 0).astype(v_ref.dtype)
    for sub in range(MID_SUBTILES):
        rows = slice(sub * ts, (sub + 1) * ts)
        x1 = x_ref[rows, :] + jnp.dot(y[rows], wob_ref[...], preferred_element_type=F32)
        x1_ref[rows, :] = x1
        h = _rms_norm(x1, g_ref[...]).astype(BF16)
        p = jnp.dot(h, wib_ref[...], preferred_element_type=F32)
        cos = cos_ref[rows, :]
        sin = sin_ref[rows, :]
        for hh in range(N_HEADS):
            qh = _norm_rope(p[:, hh * HEAD_DIM:(hh + 1) * HEAD_DIM], qg_ref[...], cos, sin, even_lane)
            q_ref[0, hh, :, rows] = (qh * q_scale).T.astype(q_ref.dtype)
        for hh in range(N_KV_HEADS):
            kh = _norm_rope(p[:, qw + hh * HEAD_DIM:qw + (hh + 1) * HEAD_DIM], kg_ref[...], cos, sin, even_lane)
            k_ref[0, hh, rows, :] = kh.astype(k_ref.dtype)
            vt = p[:, qw + kw + hh * HEAD_DIM:qw + kw + (hh + 1) * HEAD_DIM].T.astype(v_ref.dtype)
            v_ref[0, hh, :, rows] = jnp.concatenate([vt, ones_pad], axis=0)
        z_ref[rows, :] = p[:, qw + 2 * kw:].astype(z_ref.dtype)


def _mid(x2, cx, ga, f2, gb, conv_w, w_out, g_odd, w_in, q_gain, k_gain, cos, sin, bsz, seq, tm, q_scale):
    n, d = x2.shape
    assert conv_w.shape[0] == 3, "the halo blocks supply one neighbour row on each side"
    nt = seq // tm
    halo = 8
    tpb = tm // halo
    last_blk = n // halo - 1
    row = lambda b, i: (b * nt + i, 0)
    fixed = lambda b, i: (0, 0)
    prev = lambda b, i: (jnp.maximum((b * nt + i) * tpb - 1, 0), 0)
    nxt = lambda b, i: (jnp.minimum((b * nt + i + 1) * tpb, last_blk), 0)
    pos = lambda b, i: (i, 0)
    head = lambda b, i: (b, 0, i, 0)
    head_t = lambda b, i: (b, 0, 0, i)
    half = pl.BlockSpec((tm, A_WIDTH), row)
    return pl.pallas_call(
        functools.partial(_mid_kernel, q_scale=q_scale),
        grid=(bsz, nt),
        in_specs=[pl.BlockSpec((tm, d), row),
                  half,
                  pl.BlockSpec((halo, A_WIDTH), prev),
                  pl.BlockSpec((halo, A_WIDTH), nxt),
                  half, half, half,
                  pl.BlockSpec(conv_w.shape, fixed),
                  _resident(w_out.shape),
                  pl.BlockSpec((1, d), fixed),
                  _resident(w_in.shape),
                  pl.BlockSpec((1, HEAD_DIM), fixed),
                  pl.BlockSpec((1, HEAD_DIM), fixed),
                  pl.BlockSpec((tm, HEAD_DIM), pos),
                  pl.BlockSpec((tm, HEAD_DIM), pos)],
        out_specs=[pl.BlockSpec((tm, d), row),
                   pl.BlockSpec((1, N_HEADS, HEAD_DIM, tm), head_t),
                   pl.BlockSpec((1, N_KV_HEADS, tm, HEAD_DIM), head),
                   pl.BlockSpec((1, N_KV_HEADS, HEAD_DIM + V_PAD_ROWS, tm), head_t),
                   pl.BlockSpec((tm, N_HEADS * HEAD_DIM), row)],
        out_shape=[jax.ShapeDtypeStruct((n, d), F32),
                   jax.ShapeDtypeStruct((bsz, N_HEADS, HEAD_DIM, seq), BF16),
                   jax.ShapeDtypeStruct((bsz, N_KV_HEADS, seq, HEAD_DIM), BF16),
                   jax.ShapeDtypeStruct((bsz, N_KV_HEADS, HEAD_DIM + V_PAD_ROWS, seq), BF16),
                   jax.ShapeDtypeStruct((n, N_HEADS * HEAD_DIM), BF16)],
        scratch_shapes=[pltpu.VMEM(w_out.shape, BF16), pltpu.VMEM(w_in.shape, BF16)],
        compiler_params=_params(2),
        name="even_out_odd_in",
    )(x2, cx, cx, cx, ga, f2, gb, conv_w, w_out, g_odd, w_in, q_gain, k_gain, cos, sin)


def _attn_kernel(q_ref, k_ref, v_ref, qg_ref, kg_ref, o_ref, acc_ref, *, tk, q_scale):
    _, grp, hd, tq = q_ref.shape
    seq = k_ref.shape[2]
    n_chunks = seq // tk
    v_rows = v_ref.shape[2]
    shift = (hd * q_scale * SHIFT_MARGIN) * jnp.max(jnp.abs(qg_ref[...])) * jnp.max(jnp.abs(kg_ref[...]))

    def chunk(j):
        start = pl.multiple_of(j * tk, tk)
        return k_ref[0, 0, pl.ds(start, tk), :], v_ref[0, 0, :, pl.ds(start, tk)]

    bounded = shift <= SCORE_SPAN_LIMIT

    @pl.when(bounded)
    def _bounded_shift():
        q_all = jnp.concatenate([q_ref[0, g] for g in range(grp)], axis=1)

        def body(j, acc):
            kc, vc = chunk(j)
            s = jnp.dot(kc, q_all, preferred_element_type=F32)
            p = jnp.exp2(s - shift).astype(BF16)
            return acc + jnp.dot(vc, p, preferred_element_type=F32)
        acc = lax.fori_loop(0, n_chunks, body, jnp.zeros((v_rows, grp * tq), F32), unroll=True)
        for g in range(grp):
            acc_ref[g] = acc[:, g * tq:(g + 1) * tq]

    @pl.when(jnp.logical_not(bounded))
    def _online_max():
        for g in range(grp):
            def body(j, carry):
                m, acc = carry
                kc, vc = chunk(j)
                s = jnp.dot(kc, q_ref[0, g], preferred_element_type=F32)
                m_new = jnp.maximum(m, jnp.max(s, axis=0, keepdims=True))
                p = jnp.exp2(s - m_new).astype(BF16)
                acc = jnp.exp2(m - m_new) * acc + jnp.dot(vc, p, preferred_element_type=F32)
                return m_new, acc
            init = (jnp.full((1, tq), -jnp.inf, F32), jnp.zeros((v_rows, tq), F32))
            _, acc = lax.fori_loop(0, n_chunks, body, init)
            acc_ref[g] = acc

    for g in range(grp):
        acc = acc_ref[g]
        o_t = acc[:hd] / acc[hd:hd + 1]
        o_ref[0, :, g * hd:(g + 1) * hd] = o_t.T.astype(o_ref.dtype)


def _attention(qt, k, vt, q_gain, k_gain, tq, tk, q_scale):
    bsz, nh, hd, seq = qt.shape
    nkv = k.shape[1]
    grp = nh // nkv
    v_rows = vt.shape[2]
    return pl.pallas_call(
        functools.partial(_attn_kernel, tk=tk, q_scale=q_scale),
        grid=(bsz, nkv, seq // tq),
        in_specs=[pl.BlockSpec((1, grp, hd, tq), lambda b, h, i: (b, h, 0, i)),
                  pl.BlockSpec((1, 1, seq, hd), lambda b, h, i: (b, h, 0, 0)),
                  pl.BlockSpec((1, 1, v_rows, seq), lambda b, h, i: (b, h, 0, 0)),
                  pl.BlockSpec((1, hd), lambda b, h, i: (0, 0)),
                  pl.BlockSpec((1, hd), lambda b, h, i: (0, 0))],
        out_specs=pl.BlockSpec((1, tq, grp * hd), lambda b, h, i: (b, i, h)),
        out_shape=jax.ShapeDtypeStruct((bsz, seq, nh * hd), BF16),
        scratch_shapes=[pltpu.VMEM((grp, v_rows, tq), F32)],
        compiler_params=_params(3),
        name="gqa_attention",
    )(qt, k, vt, q_gain, k_gain)


def _odd_out_kernel(x1_ref, o_ref, z_ref, w_ref, g_ref, out_ref, wb_ref):
    _cast_once(pl.program_id(0) == 0, w_ref, wb_ref)
    y = (o_ref[...].astype(F32) * _silu(z_ref[...].astype(F32))).astype(BF16)
    x2 = x1_ref[...] + jnp.dot(y, wb_ref[...], preferred_element_type=F32)
    out_ref[...] = _rms_norm(x2, g_ref[...])


def _odd_out(x1, o2, z, w, g, tm):
    n, d = x1.shape
    row = lambda i: (i, 0)
    fixed = lambda i: (0, 0)
    return pl.pallas_call(
        _odd_out_kernel,
        grid=(n // tm,),
        in_specs=[pl.BlockSpec((tm, d), row),
                  pl.BlockSpec((tm, o2.shape[1]), row),
                  pl.BlockSpec((tm, z.shape[1]), row),
                  _resident(w.shape),
                  pl.BlockSpec((1, d), fixed)],
        out_specs=pl.BlockSpec((tm, d), row),
        out_shape=jax.ShapeDtypeStruct((n, d), F32),
        scratch_shapes=[pltpu.VMEM(w.shape, BF16)],
        compiler_params=_params(1),
        name="odd_out",
    )(x1, o2, z, w, g)


class _Tiles(NamedTuple):
    proj_rows: int
    mid_rows: int
    dft_rows: int
    queries: int
    keys: int


def _tiles(seq):
    assert seq % FFT_FAST == 0 and seq % GRID_W == 0, seq
    t = _Tiles(proj_rows=min(1024, seq), mid_rows=min(512, seq), dft_rows=min(DFT_ROWS_PER_STEP, seq // FFT_FAST),
               queries=min(512, seq), keys=min(512, seq))
    assert all(seq % v == 0 for v in (t.proj_rows, t.mid_rows, t.queries, t.keys)), (seq, t)
    assert (seq // FFT_FAST) % t.dft_rows == 0 and FFT_FAST % DFT_ROWS_PER_STEP == 0, (seq, t)
    return t


def kernel(x, norm_even, w_in_even, conv_w, w_out_even, norm_odd, w_in_odd, q_gain, k_gain, w_out_odd, final_norm):
    bsz, seq, d = x.shape
    assert norm_even.shape[0] == 1 and norm_odd.shape[0] == 1, "one even and one odd layer"
    assert conv_w.shape[1:] == (CONV_WIDTH, A_WIDTH) and w_in_even.shape[2] == 4 * A_WIDTH + 2 * B_WIDTH
    assert w_in_odd.shape[2] == 2 * (N_HEADS + N_KV_HEADS) * HEAD_DIM and d == N_HEADS * HEAD_DIM
    n = bsz * seq
    slow = seq // FFT_FAST
    t = _tiles(seq)
    x2 = x.reshape(n, d)

    cx, ga, bu, gb = _even_in(x2, norm_even[0][None, :], w_in_even[0], t.proj_rows)
    w1, t2, wc = _dft_tables(seq)
    scale = 1.0 / math.sqrt(seq * B_GROUP_DIM)
    f = _fourier_mix(w1, t2, wc, bu.reshape(bsz, slow, FFT_FAST, B_WIDTH),
                     nb=DFT_ROWS_PER_STEP, cc=t.dft_rows, scale=scale)
    f2 = f.reshape(n, B_WIDTH)

    cos, sin = _rope_tables(seq)
    q_scale = HEAD_DIM ** -0.5 * math.log2(math.e)
    x1, qt, k, vt, z = _mid(x2, cx, ga, f2, gb, conv_w[0], w_out_even[0],
                                 norm_odd[0][None, :], w_in_odd[0],
                                 q_gain[0][None, :], k_gain[0][None, :], cos, sin, bsz, seq, t.mid_rows, q_scale)

    o = _attention(qt, k, vt, q_gain[0][None, :], k_gain[0][None, :], tq=t.queries, tk=t.keys, q_scale=q_scale)
    out = _odd_out(x1, o.reshape(n, N_HEADS * HEAD_DIM), z, w_out_odd[0], final_norm[None, :], t.proj_rows)
    return out.reshape(bsz, seq, d)
```

```python
import functools
import math
from typing import NamedTuple

import numpy as np
import jax
import jax.numpy as jnp
from jax import lax
from jax.experimental import pallas as pl
from jax.experimental.pallas import tpu as pltpu

F32 = jnp.float32
BF16 = jnp.bfloat16

EPS = 1e-6
GRID_W = 64
ROPE_THETA = 10000.0
CONV_WIDTH = 3
HEAD_DIM = 128
N_HEADS = 8
N_KV_HEADS = 2
A_WIDTH = 512
B_WIDTH = 512
B_GROUPS = 4
B_GROUP_DIM = 128
FFT_FAST = 128
DFT_ROWS_PER_STEP = 16
V_PAD_ROWS = 16
SCORE_SPAN_LIMIT = 50.0
SHIFT_MARGIN = 1.01
MID_SUBTILES = 2

VMEM_LIMIT_BYTES = 56 * 1024 * 1024


def _silu(z):
    return z / (1.0 + jnp.exp(-z))


def _rms_norm(x, g):
    return x * lax.rsqrt(jnp.mean(x * x, axis=-1, keepdims=True) + EPS) * g


def _resident(shape):
    zeros = (0,) * len(shape)
    return pl.BlockSpec(shape, lambda *_: zeros, pipeline_mode=pl.Buffered(1))


def _cast_once(first_step, src_ref, dst_ref):
    @pl.when(first_step)
    def _():
        dst_ref[...] = src_ref[...].astype(dst_ref.dtype)


def _params(n_grid_dims):
    return pltpu.CompilerParams(
        dimension_semantics=("arbitrary",) * n_grid_dims,
        vmem_limit_bytes=VMEM_LIMIT_BYTES)


def _even_in_kernel(x_ref, g_ref, w_ref, cx_ref, ga_ref, bu_ref, gb_ref, wb_ref):
    _cast_once(pl.program_id(0) == 0, w_ref, wb_ref)
    h = _rms_norm(x_ref[...], g_ref[...]).astype(BF16)
    p = jnp.dot(h, wb_ref[...], preferred_element_type=F32)
    a_x = p[:, 0 * A_WIDTH:1 * A_WIDTH]
    a_b = p[:, 1 * A_WIDTH:2 * A_WIDTH]
    a_c = p[:, 2 * A_WIDTH:3 * A_WIDTH]
    a_z = p[:, 3 * A_WIDTH:4 * A_WIDTH]
    b_u = p[:, 4 * A_WIDTH:4 * A_WIDTH + B_WIDTH]
    b_z = p[:, 4 * A_WIDTH + B_WIDTH:]
    cx_ref[...] = (a_c * a_x).astype(cx_ref.dtype)
    ga_ref[...] = (a_b * _silu(a_z)).astype(ga_ref.dtype)
    bu_ref[...] = b_u.astype(bu_ref.dtype)
    gb_ref[...] = _silu(b_z).astype(gb_ref.dtype)


def _even_in(x2, g, w, tm):
    n, d = x2.shape
    e_in = w.shape[1]
    out = jax.ShapeDtypeStruct((n, A_WIDTH), BF16)
    row = lambda i: (i, 0)
    fixed = lambda i: (0, 0)
    return pl.pallas_call(
        _even_in_kernel,
        grid=(n // tm,),
        in_specs=[pl.BlockSpec((tm, d), row),
                  pl.BlockSpec((1, d), fixed),
                  _resident((d, e_in))],
        out_specs=[pl.BlockSpec((tm, A_WIDTH), row)] * 4,
        out_shape=[out] * 4,
        scratch_shapes=[pltpu.VMEM((d, e_in), BF16)],
        compiler_params=_params(1),
        name="even_in",
    )(x2, g, w)


def _dft_tables(seq):
    slow = seq // FFT_FAST
    a = np.arange(slow)
    th = 2.0 * np.pi * np.outer(a, a) / slow
    w1 = np.concatenate([np.cos(th), -np.sin(th)], axis=0)
    b = np.arange(FFT_FAST)
    k = a[:, None, None] + slow * b[None, :, None]
    th2 = 2.0 * np.pi * (k * b[None, None, :]) / seq
    cs, sn = np.cos(th2), np.sin(th2)
    t2 = np.concatenate([np.concatenate([cs, sn], axis=2),
                         np.concatenate([-sn, cs], axis=2)], axis=1)
    ch = np.arange(B_GROUP_DIM)
    thc = 2.0 * np.pi * np.outer(ch, ch) / B_GROUP_DIM
    wc = np.concatenate([np.cos(thc), np.sin(thc)], axis=0)
    return tuple(jnp.asarray(t, dtype=F32) for t in (w1, t2, wc))


def _dft_kernel(w1_ref, u_ref, t_ref, wc_ref, f_ref, y_ref, *, nb, cc, scale):
    _, slow, fast, width = u_ref.shape
    gd = B_GROUP_DIM

    @pl.when(pl.program_id(1) == 0)
    def _stage1():
        w1 = w1_ref[...].astype(BF16)

        def b_block(i, carry):
            cols = pl.ds(pl.multiple_of(i * nb, nb), nb)
            ut = jnp.transpose(u_ref[0, :, cols, :], (1, 0, 2))
            x = jnp.concatenate([ut[b] for b in range(nb)], axis=1)
            yb = jnp.dot(w1, x, preferred_element_type=F32).astype(y_ref.dtype)
            for part in range(2):
                rows = slice(part * slow, (part + 1) * slow)
                by_b = jnp.stack([yb[rows, b * width:(b + 1) * width] for b in range(nb)], axis=0)
                y_ref[part, :, cols, :] = jnp.transpose(by_b, (1, 0, 2))
            return carry
        lax.fori_loop(0, fast // nb, b_block, 0)

    wc = wc_ref[...].astype(BF16)
    c0 = pl.program_id(1) * cc
    by_c = []
    for ci in range(cc):
        ys = jnp.concatenate([y_ref[0, c0 + ci], y_ref[1, c0 + ci]], axis=0)
        p = jnp.dot(t_ref[ci].astype(BF16), ys, preferred_element_type=F32).astype(BF16)
        lhs = jnp.concatenate(
            [jnp.concatenate([p[:fast, g * gd:(g + 1) * gd], p[fast:, g * gd:(g + 1) * gd]], axis=1)
             for g in range(B_GROUPS)], axis=0)
        f = jnp.dot(lhs, wc, preferred_element_type=F32) * scale
        by_c.append(jnp.concatenate([f[g * fast:(g + 1) * fast] for g in range(B_GROUPS)],
                                    axis=1).astype(f_ref.dtype))
    f_ref[0] = jnp.transpose(jnp.stack(by_c, axis=0), (1, 0, 2))


def _fourier_mix(w1, t2, wc, u4, nb, cc, scale):
    bsz, slow, fast, width = u4.shape
    return pl.pallas_call(
        functools.partial(_dft_kernel, nb=nb, cc=cc, scale=scale),
        grid=(bsz, slow // cc),
        in_specs=[pl.BlockSpec(w1.shape, lambda b, j: (0, 0)),
                  pl.BlockSpec((1, slow, fast, width), lambda b, j: (b, 0, 0, 0)),
                  pl.BlockSpec((cc, 2 * fast, 2 * fast), lambda b, j: (j, 0, 0)),
                  pl.BlockSpec(wc.shape, lambda b, j: (0, 0))],
        out_specs=pl.BlockSpec((1, fast, cc, width), lambda b, j: (b, 0, j, 0)),
        out_shape=jax.ShapeDtypeStruct((bsz, fast, slow, width), BF16),
        scratch_shapes=[pltpu.VMEM((2, slow, fast, width), BF16)],
        compiler_params=_params(2),
        name="fourier_mix",
    )(w1, u4, t2, wc)


def _rope_tables(seq):
    t = np.arange(seq)
    row = (t // GRID_W).astype(np.float64)
    col = (t % GRID_W).astype(np.float64)
    n_pair = HEAD_DIM // 4
    inv = ROPE_THETA ** (-np.arange(n_pair, dtype=np.float64) / n_pair)
    ang = np.concatenate([row[:, None] * inv, col[:, None] * inv], axis=-1)
    cos = np.repeat(np.cos(ang), 2, axis=1)
    sin = np.repeat(np.sin(ang), 2, axis=1)
    sign = np.tile(np.array([-1.0, 1.0]), HEAD_DIM // 2)
    return jnp.asarray(cos, dtype=F32), jnp.asarray(sin * sign, dtype=F32)


def _norm_rope(xh, gain, cos, sin_signed, even_lane):
    xn = _rms_norm(xh, gain)
    partner = jnp.where(even_lane, pltpu.roll(xn, HEAD_DIM - 1, axis=1), pltpu.roll(xn, 1, axis=1))
    return xn * cos + partner * sin_signed


def _mid_kernel(x_ref, cx_ref, cxp_ref, cxn_ref, ga_ref, f_ref, gb_ref, cw_ref, wo_ref,
                g_ref, wi_ref, qg_ref, kg_ref, cos_ref, sin_ref,
                x1_ref, q_ref, k_ref, v_ref, z_ref, wob_ref, wib_ref, *, q_scale):
    first_step = (pl.program_id(0) == 0) & (pl.program_id(1) == 0)
    _cast_once(first_step, wo_ref, wob_ref)
    _cast_once(first_step, wi_ref, wib_ref)
    i = pl.program_id(1)
    n_i = pl.num_programs(1)
    tm = x_ref.shape[0]
    cx = cx_ref[...].astype(F32)
    prev_row = jnp.where(i > 0, cxp_ref[7:8, :].astype(F32), 0.0)
    next_row = jnp.where(i < n_i - 1, cxn_ref[0:1, :].astype(F32), 0.0)
    r = lax.broadcasted_iota(jnp.int32, cx.shape, 0)
    up = jnp.where(r == 0, prev_row, pltpu.roll(cx, 1, axis=0))
    dn = jnp.where(r == tm - 1, next_row, pltpu.roll(cx, tm - 1, axis=0))
    conv = up * cw_ref[0:1, :] + cx * cw_ref[1:2, :] + dn * cw_ref[2:3, :]
    y_a = ga_ref[...].astype(F32) * conv
    y_b = f_ref[...].astype(F32) * gb_ref[...].astype(F32)
    y = jnp.concatenate([y_a, y_b], axis=1).astype(BF16)
    qw = N_HEADS * HEAD_DIM
    kw = N_KV_HEADS * HEAD_DIM
    ts = tm // MID_SUBTILES
    even_lane = (lax.broadcasted_iota(jnp.int32, (ts, HEAD_DIM), 1) & 1) == 0
    ones_pad = (lax.broadcasted_iota(jnp.int32, (V_PAD_ROWS, ts), 0) == 0).astype(v_ref.dtype)
    for sub in range(MID_SUBTILES):
        rows = slice(sub * ts, (sub + 1) * ts)
        x1 = x_ref[rows, :] + jnp.dot(y[rows], wob_ref[...], preferred_element_type=F32)
        x1_ref[rows, :] = x1
        h = _rms_norm(x1, g_ref[...]).astype(BF16)
        p = jnp.dot(h, wib_ref[...], preferred_element_type=F32)
        cos = cos_ref[rows, :]
        sin = sin_ref[rows, :]
        for hh in range(N_HEADS):
            qh = _norm_rope(p[:, hh * HEAD_DIM:(hh + 1) * HEAD_DIM], qg_ref[...], cos, sin, even_lane)
            q_ref[0, hh, :, rows] = (qh * q_scale).T.astype(q_ref.dtype)
        for hh in range(N_KV_HEADS):
            kh = _norm_rope(p[:, qw + hh * HEAD_DIM:qw + (hh + 1) * HEAD_DIM], kg_ref[...], cos, sin, even_lane)
            k_ref[0, hh, rows, :] = kh.astype(k_ref.dtype)
            vt = p[:, qw + kw + hh * HEAD_DIM:qw + kw + (hh + 1) * HEAD_DIM].T.astype(v_ref.dtype)
            v_ref[0, hh, :, rows] = jnp.concatenate([vt, ones_pad], axis=0)
        z_ref[rows, :] = p[:, qw + 2 * kw:].astype(z_ref.dtype)


def _mid(x2, cx, ga, f2, gb, conv_w, w_out, g_odd, w_in, q_gain, k_gain, cos, sin, bsz, seq, tm, q_scale):
    n, d = x2.shape
    assert conv_w.shape[0] == 3, "the halo blocks supply one neighbour row on each side"
    nt = seq // tm
    halo = 8
    tpb = tm // halo
    last_blk = n // halo - 1
    row = lambda b, i: (b * nt + i, 0)
    fixed = lambda b, i: (0, 0)
    prev = lambda b, i: (jnp.maximum((b * nt + i) * tpb - 1, 0), 0)
    nxt = lambda b, i: (jnp.minimum((b * nt + i + 1) * tpb, last_blk), 0)
    pos = lambda b, i: (i, 0)
    head = lambda b, i: (b, 0, i, 0)
    head_t = lambda b, i: (b, 0, 0, i)
    half = pl.BlockSpec((tm, A_WIDTH), row)
    return pl.pallas_call(
        functools.partial(_mid_kernel, q_scale=q_scale),
        grid=(bsz, nt),
        in_specs=[pl.BlockSpec((tm, d), row),
                  half,
                  pl.BlockSpec((halo, A_WIDTH), prev),
                  pl.BlockSpec((halo, A_WIDTH), nxt),
                  half, half, half,
                  pl.BlockSpec(conv_w.shape, fixed),
                  _resident(w_out.shape),
                  pl.BlockSpec((1, d), fixed),
                  _resident(w_in.shape),
                  pl.BlockSpec((1, HEAD_DIM), fixed),
                  pl.BlockSpec((1, HEAD_DIM), fixed),
                  pl.BlockSpec((tm, HEAD_DIM), pos),
                  pl.BlockSpec((tm, HEAD_DIM), pos)],
        out_specs=[pl.BlockSpec((tm, d), row),
                   pl.BlockSpec((1, N_HEADS, HEAD_DIM, tm), head_t),
                   pl.BlockSpec((1, N_KV_HEADS, tm, HEAD_DIM), head),
                   pl.BlockSpec((1, N_KV_HEADS, HEAD_DIM + V_PAD_ROWS, tm), head_t),
                   pl.BlockSpec((tm, N_HEADS * HEAD_DIM), row)],
        out_shape=[jax.ShapeDtypeStruct((n, d), F32),
                   jax.ShapeDtypeStruct((bsz, N_HEADS, HEAD_DIM, seq), BF16),
                   jax.ShapeDtypeStruct((bsz, N_KV_HEADS, seq, HEAD_DIM), BF16),
                   jax.ShapeDtypeStruct((bsz, N_KV_HEADS, HEAD_DIM + V_PAD_ROWS, seq), BF16),
                   jax.ShapeDtypeStruct((n, N_HEADS * HEAD_DIM), BF16)],
        scratch_shapes=[pltpu.VMEM(w_out.shape, BF16), pltpu.VMEM(w_in.shape, BF16)],
        compiler_params=_params(2),
        name="even_out_odd_in",
    )(x2, cx, cx, cx, ga, f2, gb, conv_w, w_out, g_odd, w_in, q_gain, k_gain, cos, sin)


def _attn_kernel(q_ref, k_ref, v_ref, qg_ref, kg_ref, o_ref, acc_ref, *, tk, q_scale):
    _, grp, hd, tq = q_ref.shape
    seq = k_ref.shape[2]
    n_chunks = seq // tk
    v_rows = v_ref.shape[2]
    shift = (hd * q_scale * SHIFT_MARGIN) * jnp.max(jnp.abs(qg_ref[...])) * jnp.max(jnp.abs(kg_ref[...]))

    def chunk(j):
        start = pl.multiple_of(j * tk, tk)
        return k_ref[0, 0, pl.ds(start, tk), :], v_ref[0, 0, :, pl.ds(start, tk)]

    bounded = shift <= SCORE_SPAN_LIMIT

    @pl.when(bounded)
    def _bounded_shift():
        q_all = jnp.concatenate([q_ref[0, g] for g in range(grp)], axis=1)

        def body(j, acc):
            kc, vc = chunk(j)
            s = jnp.dot(kc, q_all, preferred_element_type=F32)
            p = jnp.exp2(s - shift).astype(BF16)
            return acc + jnp.dot(vc, p, preferred_element_type=F32)
        acc = lax.fori_loop(0, n_chunks, body, jnp.zeros((v_rows, grp * tq), F32), unroll=True)
        for g in range(grp):
            acc_ref[g] = acc[:, g * tq:(g + 1) * tq]

    @pl.when(jnp.logical_not(bounded))
    def _online_max():
        for g in range(grp):
            def body(j, carry):
                m, acc = carry
                kc, vc = chunk(j)
                s = jnp.dot(kc, q_ref[0, g], preferred_element_type=F32)
                m_new = jnp.maximum(m, jnp.max(s, axis=0, keepdims=True))
                p = jnp.exp2(s - m_new).astype(BF16)
                acc = jnp.exp2(m - m_new) * acc + jnp.dot(vc, p, preferred_element_type=F32)
                return m_new, acc
            init = (jnp.full((1, tq), -jnp.inf, F32), jnp.zeros((v_rows, tq), F32))
            _, acc = lax.fori_loop(0, n_chunks, body, init)
            acc_ref[g] = acc

    for g in range(grp):
        acc = acc_ref[g]
        o_t = acc[:hd] / acc[hd:hd + 1]
        o_ref[0, :, g * hd:(g + 1) * hd] = o_t.T.astype(o_ref.dtype)


def _attention(qt, k, vt, q_gain, k_gain, tq, tk, q_scale):
    bsz, nh, hd, seq = qt.shape
    nkv = k.shape[1]
    grp = nh // nkv
    v_rows = vt.shape[2]
    return pl.pallas_call(
        functools.partial(_attn_kernel, tk=tk, q_scale=q_scale),
        grid=(bsz, nkv, seq // tq),
        in_specs=[pl.BlockSpec((1, grp, hd, tq), lambda b, h, i: (b, h, 0, i)),
                  pl.BlockSpec((1, 1, seq, hd), lambda b, h, i: (b, h, 0, 0)),
                  pl.BlockSpec((1, 1, v_rows, seq), lambda b, h, i: (b, h, 0, 0)),
                  pl.BlockSpec((1, hd), lambda b, h, i: (0, 0)),
                  pl.BlockSpec((1, hd), lambda b, h, i: (0, 0))],
        out_specs=pl.BlockSpec((1, tq, grp * hd), lambda b, h, i: (b, i, h)),
        out_shape=jax.ShapeDtypeStruct((bsz, seq, nh * hd), BF16),
        scratch_shapes=[pltpu.VMEM((grp, v_rows, tq), F32)],
        compiler_params=_params(3),
        name="gqa_attention",
    )(qt, k, vt, q_gain, k_gain)


def _odd_out_kernel(x1_ref, o_ref, z_ref, w_ref, g_ref, out_ref, wb_ref):
    _cast_once(pl.program_id(0) == 0, w_ref, wb_ref)
    y = (o_ref[...].astype(F32) * _silu(z_ref[...].astype(F32))).astype(BF16)
    x2 = x1_ref[...] + jnp.dot(y, wb_ref[...], preferred_element_type=F32)
    out_ref[...] = _rms_norm(x2, g_ref[...])


def _odd_out(x1, o2, z, w, g, tm):
    n, d = x1.shape
    row = lambda i: (i, 0)
    fixed = lambda i: (0, 0)
    return pl.pallas_call(
        _odd_out_kernel,
        grid=(n // tm,),
        in_specs=[pl.BlockSpec((tm, d), row),
                  pl.BlockSpec((tm, o2.shape[1]), row),
                  pl.BlockSpec((tm, z.shape[1]), row),
                  _resident(w.shape),
                  pl.BlockSpec((1, d), fixed)],
        out_specs=pl.BlockSpec((tm, d), row),
        out_shape=jax.ShapeDtypeStruct((n, d), F32),
        scratch_shapes=[pltpu.VMEM(w.shape, BF16)],
        compiler_params=_params(1),
        name="odd_out",
    )(x1, o2, z, w, g)


class _Tiles(NamedTuple):
    proj_rows: int
    mid_rows: int
    dft_rows: int
    queries: int
    keys: int


def _tiles(seq):
    assert seq % FFT_FAST == 0 and seq % GRID_W == 0, seq
    t = _Tiles(proj_rows=min(1024, seq), mid_rows=min(512, seq), dft_rows=min(DFT_ROWS_PER_STEP, seq // FFT_FAST),
               queries=min(512, seq), keys=min(512, seq))
    assert all(seq % v == 0 for v in (t.proj_rows, t.mid_rows, t.queries, t.keys)), (seq, t)
    assert (seq // FFT_FAST) % t.dft_rows == 0 and FFT_FAST % DFT_ROWS_PER_STEP == 0, (seq, t)
    return t


def kernel(x, norm_even, w_in_even, conv_w, w_out_even, norm_odd, w_in_odd, q_gain, k_gain, w_out_odd, final_norm):
    bsz, seq, d = x.shape
    assert norm_even.shape[0] == 1 and norm_odd.shape[0] == 1, "one even and one odd layer"
    assert conv_w.shape[1:] == (CONV_WIDTH, A_WIDTH) and w_in_even.shape[2] == 4 * A_WIDTH + 2 * B_WIDTH
    assert w_in_odd.shape[2] == 2 * (N_HEADS + N_KV_HEADS) * HEAD_DIM and d == N_HEADS * HEAD_DIM
    n = bsz * seq
    slow = seq // FFT_FAST
    t = _tiles(seq)
    x2 = x.reshape(n, d)

    cx, ga, bu, gb = _even_in(x2, norm_even[0][None, :], w_in_even[0], t.proj_rows)
    w1, t2, wc = _dft_tables(seq)
    scale = 1.0 / math.sqrt(seq * B_GROUP_DIM)
    f = _fourier_mix(w1, t2, wc, bu.reshape(bsz, slow, FFT_FAST, B_WIDTH),
                     nb=DFT_ROWS_PER_STEP, cc=t.dft_rows, scale=scale)
    f2 = f.reshape(n, B_WIDTH)

    cos, sin = _rope_tables(seq)
    q_scale = HEAD_DIM ** -0.5 * math.log2(math.e)
    x1, qt, k, vt, z = _mid(x2, cx, ga, f2, gb, conv_w[0], w_out_even[0],
                                 norm_odd[0][None, :], w_in_odd[0],
                                 q_gain[0][None, :], k_gain[0][None, :], cos, sin, bsz, seq, t.mid_rows, q_scale)

    o = _attention(qt, k, vt, q_gain[0][None, :], k_gain[0][None, :], tq=t.queries, tk=t.keys, q_scale=q_scale)
    out = _odd_out(x1, o.reshape(n, N_HEADS * HEAD_DIM), z, w_out_odd[0], final_norm[None, :], t.proj_rows)
    return out.reshape(bsz, seq, d)
```

```python
import functools
import math
from typing import NamedTuple

import numpy as np
import jax
import jax.numpy as jnp
from jax import lax
from jax.experimental import pallas as pl
from jax.experimental.pallas import tpu as pltpu

F32 = jnp.float32
BF16 = jnp.bfloat16

EPS = 1e-6
GRID_W = 64
ROPE_THETA = 10000.0
CONV_WIDTH = 3
HEAD_DIM = 128
N_HEADS = 8
N_KV_HEADS = 2
A_WIDTH = 512
B_WIDTH = 512
B_GROUPS = 4
B_GROUP_DIM = 128
FFT_FAST = 128
DFT_ROWS_PER_STEP = 16
V_PAD_ROWS = 16
SCORE_SPAN_LIMIT = 50.0
SHIFT_MARGIN = 1.01
MID_SUBTILES = 2

VMEM_LIMIT_BYTES = 56 * 1024 * 1024


def _silu(z):
    half = 0.5 * z
    return half + half * jnp.tanh(half)


def _rms_norm(x, g):
    return x * lax.rsqrt(jnp.mean(x * x, axis=-1, keepdims=True) + EPS) * g


def _resident(shape):
    zeros = (0,) * len(shape)
    return pl.BlockSpec(shape, lambda *_: zeros, pipeline_mode=pl.Buffered(1))


def _cast_once(first_step, src_ref, dst_ref):
    @pl.when(first_step)
    def _():
        dst_ref[...] = src_ref[...].astype(dst_ref.dtype)


def _params(n_grid_dims):
    return pltpu.CompilerParams(
        dimension_semantics=("arbitrary",) * n_grid_dims,
        vmem_limit_bytes=VMEM_LIMIT_BYTES)


def _even_in_kernel(x_ref, g_ref, w_ref, cx_ref, ga_ref, bu_ref, gb_ref, wb_ref):
    _cast_once(pl.program_id(0) == 0, w_ref, wb_ref)
    h = _rms_norm(x_ref[...], g_ref[...]).astype(BF16)
    p = jnp.dot(h, wb_ref[...], preferred_element_type=F32)
    a_x = p[:, 0 * A_WIDTH:1 * A_WIDTH]
    a_b = p[:, 1 * A_WIDTH:2 * A_WIDTH]
    a_c = p[:, 2 * A_WIDTH:3 * A_WIDTH]
    a_z = p[:, 3 * A_WIDTH:4 * A_WIDTH]
    b_u = p[:, 4 * A_WIDTH:4 * A_WIDTH + B_WIDTH]
    b_z = p[:, 4 * A_WIDTH + B_WIDTH:]
    cx_ref[...] = (a_c * a_x).astype(cx_ref.dtype)
    ga_ref[...] = (a_b * _silu(a_z)).astype(ga_ref.dtype)
    bu_ref[...] = b_u.astype(bu_ref.dtype)
    gb_ref[...] = _silu(b_z).astype(gb_ref.dtype)


def _even_in(x2, g, w, tm):
    n, d = x2.shape
    e_in = w.shape[1]
    out = jax.ShapeDtypeStruct((n, A_WIDTH), BF16)
    row = lambda i: (i, 0)
    fixed = lambda i: (0, 0)
    return pl.pallas_call(
        _even_in_kernel,
        grid=(n // tm,),
        in_specs=[pl.BlockSpec((tm, d), row),
                  pl.BlockSpec((1, d), fixed),
                  _resident((d, e_in))],
        out_specs=[pl.BlockSpec((tm, A_WIDTH), row)] * 4,
        out_shape=[out] * 4,
        scratch_shapes=[pltpu.VMEM((d, e_in), BF16)],
        compiler_params=_params(1),
        name="even_in",
    )(x2, g, w)


def _dft_tables(seq):
    slow = seq // FFT_FAST
    a = np.arange(slow)
    th = 2.0 * np.pi * np.outer(a, a) / slow
    w1 = np.concatenate([np.cos(th), -np.sin(th)], axis=0)
    b = np.arange(FFT_FAST)
    k = a[:, None, None] + slow * b[None, :, None]
    th2 = 2.0 * np.pi * (k * b[None, None, :]) / seq
    cs, sn = np.cos(th2), np.sin(th2)
    t2 = np.concatenate([np.concatenate([cs, sn], axis=2),
                         np.concatenate([-sn, cs], axis=2)], axis=1)
    ch = np.arange(B_GROUP_DIM)
    thc = 2.0 * np.pi * np.outer(ch, ch) / B_GROUP_DIM
    wc = np.concatenate([np.cos(thc), np.sin(thc)], axis=0)
    return tuple(jnp.asarray(t, dtype=F32) for t in (w1, t2, wc))


def _dft_kernel(w1_ref, u_ref, t_ref, wc_ref, f_ref, y_ref, *, nb, cc, scale):
    _, slow, fast, width = u_ref.shape
    gd = B_GROUP_DIM

    @pl.when(pl.program_id(1) == 0)
    def _stage1():
        w1 = w1_ref[...].astype(BF16)

        def b_block(i, carry):
            cols = pl.ds(pl.multiple_of(i * nb, nb), nb)
            ut = jnp.transpose(u_ref[0, :, cols, :], (1, 0, 2))
            x = jnp.concatenate([ut[b] for b in range(nb)], axis=1)
            yb = jnp.dot(w1, x, preferred_element_type=F32).astype(y_ref.dtype)
            for part in range(2):
                rows = slice(part * slow, (part + 1) * slow)
                by_b = jnp.stack([yb[rows, b * width:(b + 1) * width] for b in range(nb)], axis=0)
                y_ref[part, :, cols, :] = jnp.transpose(by_b, (1, 0, 2))
            return carry
        lax.fori_loop(0, fast // nb, b_block, 0)

    wc = wc_ref[...].astype(BF16)
    c0 = pl.program_id(1) * cc
    by_c = []
    for ci in range(cc):
        ys = jnp.concatenate([y_ref[0, c0 + ci], y_ref[1, c0 + ci]], axis=0)
        p = jnp.dot(t_ref[ci].astype(BF16), ys, preferred_element_type=F32).astype(BF16)
        lhs = jnp.concatenate(
            [jnp.concatenate([p[:fast, g * gd:(g + 1) * gd], p[fast:, g * gd:(g + 1) * gd]], axis=1)
             for g in range(B_GROUPS)], axis=0)
        f = jnp.dot(lhs, wc, preferred_element_type=F32) * scale
        by_c.append(jnp.concatenate([f[g * fast:(g + 1) * fast] for g in range(B_GROUPS)],
                                    axis=1).astype(f_ref.dtype))
    f_ref[0] = jnp.transpose(jnp.stack(by_c, axis=0), (1, 0, 2))


def _fourier_mix(w1, t2, wc, u4, nb, cc, scale):
    bsz, slow, fast, width = u4.shape
    return pl.pallas_call(
        functools.partial(_dft_kernel, nb=nb, cc=cc, scale=scale),
        grid=(bsz, slow // cc),
        in_specs=[pl.BlockSpec(w1.shape, lambda b, j: (0, 0)),
                  pl.BlockSpec((1, slow, fast, width), lambda b, j: (b, 0, 0, 0)),
                  pl.BlockSpec((cc, 2 * fast, 2 * fast), lambda b, j: (j, 0, 0)),
                  pl.BlockSpec(wc.shape, lambda b, j: (0, 0))],
        out_specs=pl.BlockSpec((1, fast, cc, width), lambda b, j: (b, 0, j, 0)),
        out_shape=jax.ShapeDtypeStruct((bsz, fast, slow, width), BF16),
        scratch_shapes=[pltpu.VMEM((2, slow, fast, width), BF16)],
        compiler_params=_params(2),
        name="fourier_mix",
    )(w1, u4, t2, wc)


def _rope_tables(seq):
    t = np.arange(seq)
    row = (t // GRID_W).astype(np.float64)
    col = (t % GRID_W).astype(np.float64)
    n_pair = HEAD_DIM // 4
    inv = ROPE_THETA ** (-np.arange(n_pair, dtype=np.float64) / n_pair)
    ang = np.concatenate([row[:, None] * inv, col[:, None] * inv], axis=-1)
    cos = np.repeat(np.cos(ang), 2, axis=1)
    sin = np.repeat(np.sin(ang), 2, axis=1)
    sign = np.tile(np.array([-1.0, 1.0]), HEAD_DIM // 2)
    return jnp.asarray(cos, dtype=F32), jnp.asarray(sin * sign, dtype=F32)


def _norm_rope(xh, gain, cos, sin_signed, even_lane):
    xn = _rms_norm(xh, gain)
    partner = jnp.where(even_lane, pltpu.roll(xn, HEAD_DIM - 1, axis=1), pltpu.roll(xn, 1, axis=1))
    return xn * cos + partner * sin_signed


def _mid_kernel(x_ref, cx_ref, cxp_ref, cxn_ref, ga_ref, f_ref, gb_ref, cw_ref, wo_ref,
                g_ref, wi_ref, qg_ref, kg_ref, cos_ref, sin_ref,
                x1_ref, q_ref, k_ref, v_ref, z_ref, wob_ref, wib_ref, *, q_scale):
    first_step = (pl.program_id(0) == 0) & (pl.program_id(1) == 0)
    _cast_once(first_step, wo_ref, wob_ref)
    _cast_once(first_step, wi_ref, wib_ref)
    i = pl.program_id(1)
    n_i = pl.num_programs(1)
    tm = x_ref.shape[0]
    cx = cx_ref[...].astype(F32)
    prev_row = jnp.where(i > 0, cxp_ref[7:8, :].astype(F32), 0.0)
    next_row = jnp.where(i < n_i - 1, cxn_ref[0:1, :].astype(F32), 0.0)
    r = lax.broadcasted_iota(jnp.int32, cx.shape, 0)
    up = jnp.where(r == 0, prev_row, pltpu.roll(cx, 1, axis=0))
    dn = jnp.where(r == tm - 1, next_row, pltpu.roll(cx, tm - 1, axis=0))
    conv = up * cw_ref[0:1, :] + cx * cw_ref[1:2, :] + dn * cw_ref[2:3, :]
    y_a = ga_ref[...].astype(F32) * conv
    y_b = f_ref[...].astype(F32) * gb_ref[...].astype(F32)
    y = jnp.concatenate([y_a, y_b], axis=1).astype(BF16)
    qw = N_HEADS * HEAD_DIM
    kw = N_KV_HEADS * HEAD_DIM
    ts = tm // MID_SUBTILES
    even_lane = (lax.broadcasted_iota(jnp.int32, (ts, HEAD_DIM), 1) & 1) == 0
    ones_pad = (lax.broadcasted_iota(jnp.int32, (V_PAD_ROWS, ts), 0) == 0).astype(v_ref.dtype)
    for sub in range(MID_SUBTILES):
        rows = slice(sub * ts, (sub + 1) * ts)
        x1 = x_ref[rows, :] + jnp.dot(y[rows], wob_ref[...], preferred_element_type=F32)
        x1_ref[rows, :] = x1
        h = _rms_norm(x1, g_ref[...]).astype(BF16)
        p = jnp.dot(h, wib_ref[...], preferred_element_type=F32)
        cos = cos_ref[rows, :]
        sin = sin_ref[rows, :]
        for hh in range(N_HEADS):
            qh = _norm_rope(p[:, hh * HEAD_DIM:(hh + 1) * HEAD_DIM], qg_ref[...], cos, sin, even_lane)
            q_ref[0, hh, :, rows] = (qh * q_scale).T.astype(q_ref.dtype)
        for hh in range(N_KV_HEADS):
            kh = _norm_rope(p[:, qw + hh * HEAD_DIM:qw + (hh + 1) * HEAD_DIM], kg_ref[...], cos, sin, even_lane)
            k_ref[0, hh, rows, :] = kh.astype(k_ref.dtype)
            vt = p[:, qw + kw + hh * HEAD_DIM:qw + kw + (hh + 1) * HEAD_DIM].T.astype(v_ref.dtype)
            v_ref[0, hh, :, rows] = jnp.concatenate([vt, ones_pad], axis=0)
        z_ref[rows, :] = p[:, qw + 2 * kw:].astype(z_ref.dtype)


def _mid(x2, cx, ga, f2, gb, conv_w, w_out, g_odd, w_in, q_gain, k_gain, cos, sin, bsz, seq, tm, q_scale):
    n, d = x2.shape
    assert conv_w.shape[0] == 3, "the halo blocks supply one neighbour row on each side"
    nt = seq // tm
    halo = 8
    tpb = tm // halo
    last_blk = n // halo - 1
    row = lambda b, i: (b * nt + i, 0)
    fixed = lambda b, i: (0, 0)
    prev = lambda b, i: (jnp.maximum((b * nt + i) * tpb - 1, 0), 0)
    nxt = lambda b, i: (jnp.minimum((b * nt + i + 1) * tpb, last_blk), 0)
    pos = lambda b, i: (i, 0)
    head = lambda b, i: (b, 0, i, 0)
    head_t = lambda b, i: (b, 0, 0, i)
    half = pl.BlockSpec((tm, A_WIDTH), row)
    return pl.pallas_call(
        functools.partial(_mid_kernel, q_scale=q_scale),
        grid=(bsz, nt),
        in_specs=[pl.BlockSpec((tm, d), row),
                  half,
                  pl.BlockSpec((halo, A_WIDTH), prev),
                  pl.BlockSpec((halo, A_WIDTH), nxt),
                  half, half, half,
                  pl.BlockSpec(conv_w.shape, fixed),
                  _resident(w_out.shape),
                  pl.BlockSpec((1, d), fixed),
                  _resident(w_in.shape),
                  pl.BlockSpec((1, HEAD_DIM), fixed),
                  pl.BlockSpec((1, HEAD_DIM), fixed),
                  pl.BlockSpec((tm, HEAD_DIM), pos),
                  pl.BlockSpec((tm, HEAD_DIM), pos)],
        out_specs=[pl.BlockSpec((tm, d), row),
                   pl.BlockSpec((1, N_HEADS, HEAD_DIM, tm), head_t),
                   pl.BlockSpec((1, N_KV_HEADS, tm, HEAD_DIM), head),
                   pl.BlockSpec((1, N_KV_HEADS, HEAD_DIM + V_PAD_ROWS, tm), head_t),
                   pl.BlockSpec((tm, N_HEADS * HEAD_DIM), row)],
        out_shape=[jax.ShapeDtypeStruct((n, d), F32),
                   jax.ShapeDtypeStruct((bsz, N_HEADS, HEAD_DIM, seq), BF16),
                   jax.ShapeDtypeStruct((bsz, N_KV_HEADS, seq, HEAD_DIM), BF16),
                   jax.ShapeDtypeStruct((bsz, N_KV_HEADS, HEAD_DIM + V_PAD_ROWS, seq), BF16),
                   jax.ShapeDtypeStruct((n, N_HEADS * HEAD_DIM), BF16)],
        scratch_shapes=[pltpu.VMEM(w_out.shape, BF16), pltpu.VMEM(w_in.shape, BF16)],
        compiler_params=_params(2),
        name="even_out_odd_in",
    )(x2, cx, cx, cx, ga, f2, gb, conv_w, w_out, g_odd, w_in, q_gain, k_gain, cos, sin)


def _attn_kernel(q_ref, k_ref, v_ref, qg_ref, kg_ref, o_ref, acc_ref, *, tk, q_scale):
    _, grp, hd, tq = q_ref.shape
    seq = k_ref.shape[2]
    n_chunks = seq // tk
    v_rows = v_ref.shape[2]
    shift = (hd * q_scale * SHIFT_MARGIN) * jnp.max(jnp.abs(qg_ref[...])) * jnp.max(jnp.abs(kg_ref[...]))

    def chunk(j):
        start = pl.multiple_of(j * tk, tk)
        return k_ref[0, 0, pl.ds(start, tk), :], v_ref[0, 0, :, pl.ds(start, tk)]

    bounded = shift <= SCORE_SPAN_LIMIT

    @pl.when(bounded)
    def _bounded_shift():
        q_all = jnp.concatenate([q_ref[0, g] for g in range(grp)], axis=1)

        def body(j, acc):
            kc, vc = chunk(j)
            s = jnp.dot(kc, q_all, preferred_element_type=F32)
            p = jnp.exp2(s - shift).astype(BF16)
            return acc + jnp.dot(vc, p, preferred_element_type=F32)
        acc = lax.fori_loop(0, n_chunks, body, jnp.zeros((v_rows, grp * tq), F32), unroll=True)
        for g in range(grp):
            acc_ref[g] = acc[:, g * tq:(g + 1) * tq]

    @pl.when(jnp.logical_not(bounded))
    def _online_max():
        for g in range(grp):
            def body(j, carry):
                m, acc = carry
                kc, vc = chunk(j)
                s = jnp.dot(kc, q_ref[0, g], preferred_element_type=F32)
                m_new = jnp.maximum(m, jnp.max(s, axis=0, keepdims=True))
                p = jnp.exp2(s - m_new).astype(BF16)
                acc = jnp.exp2(m - m_new) * acc + jnp.dot(vc, p, preferred_element_type=F32)
                return m_new, acc
            init = (jnp.full((1, tq), -jnp.inf, F32), jnp.zeros((v_rows, tq), F32))
            _, acc = lax.fori_loop(0, n_chunks, body, init)
            acc_ref[g] = acc

    for g in range(grp):
        acc = acc_ref[g]
        o_t = acc[:hd] / acc[hd:hd + 1]
        o_ref[0, :, g * hd:(g + 1) * hd] = o_t.T.astype(o_ref.dtype)


def _attention(qt, k, vt, q_gain, k_gain, tq, tk, q_scale):
    bsz, nh, hd, seq = qt.shape
    nkv = k.shape[1]
    grp = nh // nkv
    v_rows = vt.shape[2]
    return pl.pallas_call(
        functools.partial(_attn_kernel, tk=tk, q_scale=q_scale),
        grid=(bsz, nkv, seq // tq),
        in_specs=[pl.BlockSpec((1, grp, hd, tq), lambda b, h, i: (b, h, 0, i)),
                  pl.BlockSpec((1, 1, seq, hd), lambda b, h, i: (b, h, 0, 0)),
                  pl.BlockSpec((1, 1, v_rows, seq), lambda b, h, i: (b, h, 0, 0)),
                  pl.BlockSpec((1, hd), lambda b, h, i: (0, 0)),
                  pl.BlockSpec((1, hd), lambda b, h, i: (0, 0))],
        out_specs=pl.BlockSpec((1, tq, grp * hd), lambda b, h, i: (b, i, h)),
        out_shape=jax.ShapeDtypeStruct((bsz, seq, nh * hd), BF16),
        scratch_shapes=[pltpu.VMEM((grp, v_rows, tq), F32)],
        compiler_params=_params(3),
        name="gqa_attention",
    )(qt, k, vt, q_gain, k_gain)


def _odd_out_kernel(x1_ref, o_ref, z_ref, w_ref, g_ref, out_ref, wb_ref):
    _cast_once(pl.program_id(0) == 0, w_ref, wb_ref)
    y = (o_ref[...].astype(F32) * _silu(z_ref[...].astype(F32))).astype(BF16)
    x2 = x1_ref[...] + jnp.dot(y, wb_ref[...], preferred_element_type=F32)
    out_ref[...] = _rms_norm(x2, g_ref[...])


def _odd_out(x1, o2, z, w, g, tm):
    n, d = x1.shape
    row = lambda i: (i, 0)
    fixed = lambda i: (0, 0)
    return pl.pallas_call(
        _odd_out_kernel,
        grid=(n // tm,),
        in_specs=[pl.BlockSpec((tm, d), row),
                  pl.BlockSpec((tm, o2.shape[1]), row),
                  pl.BlockSpec((tm, z.shape[1]), row),
                  _resident(w.shape),
                  pl.BlockSpec((1, d), fixed)],
        out_specs=pl.BlockSpec((tm, d), row),
        out_shape=jax.ShapeDtypeStruct((n, d), F32),
        scratch_shapes=[pltpu.VMEM(w.shape, BF16)],
        compiler_params=_params(1),
        name="odd_out",
    )(x1, o2, z, w, g)


class _Tiles(NamedTuple):
    proj_rows: int
    mid_rows: int
    dft_rows: int
    queries: int
    keys: int


def _tiles(seq):
    assert seq % FFT_FAST == 0 and seq % GRID_W == 0, seq
    t = _Tiles(proj_rows=min(1024, seq), mid_rows=min(512, seq), dft_rows=min(DFT_ROWS_PER_STEP, seq // FFT_FAST),
               queries=min(512, seq), keys=min(512, seq))
    assert all(seq % v == 0 for v in (t.proj_rows, t.mid_rows, t.queries, t.keys)), (seq, t)
    assert (seq // FFT_FAST) % t.dft_rows == 0 and FFT_FAST % DFT_ROWS_PER_STEP == 0, (seq, t)
    return t


def kernel(x, norm_even, w_in_even, conv_w, w_out_even, norm_odd, w_in_odd, q_gain, k_gain, w_out_odd, final_norm):
    bsz, seq, d = x.shape
    assert norm_even.shape[0] == 1 and norm_odd.shape[0] == 1, "one even and one odd layer"
    assert conv_w.shape[1:] == (CONV_WIDTH, A_WIDTH) and w_in_even.shape[2] == 4 * A_WIDTH + 2 * B_WIDTH
    assert w_in_odd.shape[2] == 2 * (N_HEADS + N_KV_HEADS) * HEAD_DIM and d == N_HEADS * HEAD_DIM
    n = bsz * seq
    slow = seq // FFT_FAST
    t = _tiles(seq)
    x2 = x.reshape(n, d)

    cx, ga, bu, gb = _even_in(x2, norm_even[0][None, :], w_in_even[0], t.proj_rows)
    w1, t2, wc = _dft_tables(seq)
    scale = 1.0 / math.sqrt(seq * B_GROUP_DIM)
    f = _fourier_mix(w1, t2, wc, bu.reshape(bsz, slow, FFT_FAST, B_WIDTH),
                     nb=DFT_ROWS_PER_STEP, cc=t.dft_rows, scale=scale)
    f2 = f.reshape(n, B_WIDTH)

    cos, sin = _rope_tables(seq)
    q_scale = HEAD_DIM ** -0.5 * math.log2(math.e)
    x1, qt, k, vt, z = _mid(x2, cx, ga, f2, gb, conv_w[0], w_out_even[0],
                                 norm_odd[0][None, :], w_in_odd[0],
                                 q_gain[0][None, :], k_gain[0][None, :], cos, sin, bsz, seq, t.mid_rows, q_scale)

    o = _attention(qt, k, vt, q_gain[0][None, :], k_gain[0][None, :], tq=t.queries, tk=t.keys, q_scale=q_scale)
    out = _odd_out(x1, o.reshape(n, N_HEADS * HEAD_DIM), z, w_out_odd[0], final_norm[None, :], t.proj_rows)
    return out.reshape(bsz, seq, d)
```

```python
import functools
import math
from typing import NamedTuple

import numpy as np
import jax
import jax.numpy as jnp
from jax import lax
from jax.experimental import pallas as pl
from jax.experimental.pallas import tpu as pltpu

F32 = jnp.float32
BF16 = jnp.bfloat16

EPS = 1e-6
GRID_W = 64
ROPE_THETA = 10000.0
CONV_WIDTH = 3
HEAD_DIM = 128
N_HEADS = 8
N_KV_HEADS = 2
A_WIDTH = 512
B_WIDTH = 512
B_GROUPS = 4
B_GROUP_DIM = 128
FFT_FAST = 128
DFT_ROWS_PER_STEP = 16
V_PAD_ROWS = 16
SCORE_SPAN_LIMIT = 50.0
SHIFT_MARGIN = 1.01
MID_SUBTILES = 2

VMEM_LIMIT_BYTES = 56 * 1024 * 1024


def _silu(z):
    half = 0.5 * z
    return half + half * jnp.tanh(half)


def _rms_norm(x, g):
    return x * lax.rsqrt(jnp.mean(x * x, axis=-1, keepdims=True) + EPS) * g


def _score_shift(q_gain, k_gain, q_scale):
    return (HEAD_DIM * q_scale * SHIFT_MARGIN) * jnp.max(jnp.abs(q_gain)) * jnp.max(jnp.abs(k_gain))


def _resident(shape):
    zeros = (0,) * len(shape)
    return pl.BlockSpec(shape, lambda *_: zeros, pipeline_mode=pl.Buffered(1))


def _cast_once(first_step, src_ref, dst_ref):
    @pl.when(first_step)
    def _():
        dst_ref[...] = src_ref[...].astype(dst_ref.dtype)


def _params(n_grid_dims):
    return pltpu.CompilerParams(
        dimension_semantics=("arbitrary",) * n_grid_dims,
        vmem_limit_bytes=VMEM_LIMIT_BYTES)


def _even_in_kernel(x_ref, g_ref, w_ref, cx_ref, ga_ref, bu_ref, gb_ref, wb_ref):
    _cast_once(pl.program_id(0) == 0, w_ref, wb_ref)
    h = _rms_norm(x_ref[...], g_ref[...]).astype(BF16)
    p = jnp.dot(h, wb_ref[...], preferred_element_type=F32)
    a_x = p[:, 0 * A_WIDTH:1 * A_WIDTH]
    a_b = p[:, 1 * A_WIDTH:2 * A_WIDTH]
    a_c = p[:, 2 * A_WIDTH:3 * A_WIDTH]
    a_z = p[:, 3 * A_WIDTH:4 * A_WIDTH]
    b_u = p[:, 4 * A_WIDTH:4 * A_WIDTH + B_WIDTH]
    b_z = p[:, 4 * A_WIDTH + B_WIDTH:]
    cx_ref[...] = (a_c * a_x).astype(cx_ref.dtype)
    ga_ref[...] = (a_b * _silu(a_z)).astype(ga_ref.dtype)
    bu_ref[...] = b_u.astype(bu_ref.dtype)
    gb_ref[...] = _silu(b_z).astype(gb_ref.dtype)


def _even_in(x2, g, w, tm):
    n, d = x2.shape
    e_in = w.shape[1]
    out = jax.ShapeDtypeStruct((n, A_WIDTH), BF16)
    row = lambda i: (i, 0)
    fixed = lambda i: (0, 0)
    return pl.pallas_call(
        _even_in_kernel,
        grid=(n // tm,),
        in_specs=[pl.BlockSpec((tm, d), row),
                  pl.BlockSpec((1, d), fixed),
                  _resident((d, e_in))],
        out_specs=[pl.BlockSpec((tm, A_WIDTH), row)] * 4,
        out_shape=[out] * 4,
        scratch_shapes=[pltpu.VMEM((d, e_in), BF16)],
        compiler_params=_params(1),
        name="even_in",
    )(x2, g, w)


def _dft_tables(seq):
    slow = seq // FFT_FAST
    a = np.arange(slow)
    th = 2.0 * np.pi * np.outer(a, a) / slow
    w1 = np.concatenate([np.cos(th), -np.sin(th)], axis=0)
    b = np.arange(FFT_FAST)
    k = a[:, None, None] + slow * b[None, :, None]
    th2 = 2.0 * np.pi * (k * b[None, None, :]) / seq
    cs, sn = np.cos(th2), np.sin(th2)
    t2 = np.concatenate([np.concatenate([cs, sn], axis=2),
                         np.concatenate([-sn, cs], axis=2)], axis=1)
    ch = np.arange(B_GROUP_DIM)
    thc = 2.0 * np.pi * np.outer(ch, ch) / B_GROUP_DIM
    wc = np.concatenate([np.cos(thc), np.sin(thc)], axis=0)
    return tuple(jnp.asarray(t, dtype=F32) for t in (w1, t2, wc))


def _dft_kernel(w1_ref, u_ref, t_ref, wc_ref, f_ref, y_ref, *, nb, cc, scale):
    _, slow, fast, width = u_ref.shape
    gd = B_GROUP_DIM

    @pl.when(pl.program_id(1) == 0)
    def _stage1():
        w1 = w1_ref[...].astype(BF16)

        def b_block(i, carry):
            cols = pl.ds(pl.multiple_of(i * nb, nb), nb)
            ut = jnp.transpose(u_ref[0, :, cols, :], (1, 0, 2))
            x = jnp.concatenate([ut[b] for b in range(nb)], axis=1)
            yb = jnp.dot(w1, x, preferred_element_type=F32).astype(y_ref.dtype)
            for part in range(2):
                rows = slice(part * slow, (part + 1) * slow)
                by_b = jnp.stack([yb[rows, b * width:(b + 1) * width] for b in range(nb)], axis=0)
                y_ref[part, :, cols, :] = jnp.transpose(by_b, (1, 0, 2))
            return carry
        lax.fori_loop(0, fast // nb, b_block, 0)

    wc = wc_ref[...].astype(BF16)
    c0 = pl.program_id(1) * cc
    by_c = []
    for ci in range(cc):
        ys = jnp.concatenate([y_ref[0, c0 + ci], y_ref[1, c0 + ci]], axis=0)
        p = jnp.dot(t_ref[ci].astype(BF16), ys, preferred_element_type=F32).astype(BF16)
        lhs = jnp.concatenate(
            [jnp.concatenate([p[:fast, g * gd:(g + 1) * gd], p[fast:, g * gd:(g + 1) * gd]], axis=1)
             for g in range(B_GROUPS)], axis=0)
        f = jnp.dot(lhs, wc, preferred_element_type=F32) * scale
        by_c.append(jnp.concatenate([f[g * fast:(g + 1) * fast] for g in range(B_GROUPS)],
                                    axis=1).astype(f_ref.dtype))
    f_ref[0] = jnp.transpose(jnp.stack(by_c, axis=0), (1, 0, 2))


def _fourier_mix(w1, t2, wc, u4, nb, cc, scale):
    bsz, slow, fast, width = u4.shape
    return pl.pallas_call(
        functools.partial(_dft_kernel, nb=nb, cc=cc, scale=scale),
        grid=(bsz, slow // cc),
        in_specs=[pl.BlockSpec(w1.shape, lambda b, j: (0, 0)),
                  pl.BlockSpec((1, slow, fast, width), lambda b, j: (b, 0, 0, 0)),
                  pl.BlockSpec((cc, 2 * fast, 2 * fast), lambda b, j: (j, 0, 0)),
                  pl.BlockSpec(wc.shape, lambda b, j: (0, 0))],
        out_specs=pl.BlockSpec((1, fast, cc, width), lambda b, j: (b, 0, j, 0)),
        out_shape=jax.ShapeDtypeStruct((bsz, fast, slow, width), BF16),
        scratch_shapes=[pltpu.VMEM((2, slow, fast, width), BF16)],
        compiler_params=_params(2),
        name="fourier_mix",
    )(w1, u4, t2, wc)


def _rope_tables(seq):
    t = np.arange(seq)
    row = (t // GRID_W).astype(np.float64)
    col = (t % GRID_W).astype(np.float64)
    n_pair = HEAD_DIM // 4
    inv = ROPE_THETA ** (-np.arange(n_pair, dtype=np.float64) / n_pair)
    ang = np.concatenate([row[:, None] * inv, col[:, None] * inv], axis=-1)
    cos = np.repeat(np.cos(ang), 2, axis=1)
    sin = np.repeat(np.sin(ang), 2, axis=1)
    sign = np.tile(np.array([-1.0, 1.0]), HEAD_DIM // 2)
    return jnp.asarray(cos, dtype=F32), jnp.asarray(sin * sign, dtype=F32)


def _norm_rope(xh, gain, cos, sin_signed, even_lane):
    xn = _rms_norm(xh, gain)
    partner = jnp.where(even_lane, pltpu.roll(xn, HEAD_DIM - 1, axis=1), pltpu.roll(xn, 1, axis=1))
    return xn * cos + partner * sin_signed


def _mid_kernel(x_ref, cx_ref, cxp_ref, cxn_ref, ga_ref, f_ref, gb_ref, cw_ref, wo_ref,
                g_ref, wi_ref, qg_ref, kg_ref, cos_ref, sin_ref,
                x1_ref, q_ref, k_ref, v_ref, z_ref, wob_ref, wib_ref, *, q_scale):
    first_step = (pl.program_id(0) == 0) & (pl.program_id(1) == 0)
    _cast_once(first_step, wo_ref, wob_ref)
    _cast_once(first_step, wi_ref, wib_ref)
    i = pl.program_id(1)
    n_i = pl.num_programs(1)
    tm = x_ref.shape[0]
    cx = cx_ref[...].astype(F32)
    prev_row = jnp.where(i > 0, cxp_ref[7:8, :].astype(F32), 0.0)
    next_row = jnp.where(i < n_i - 1, cxn_ref[0:1, :].astype(F32), 0.0)
    r = lax.broadcasted_iota(jnp.int32, cx.shape, 0)
    up = jnp.where(r == 0, prev_row, pltpu.roll(cx, 1, axis=0))
    dn = jnp.where(r == tm - 1, next_row, pltpu.roll(cx, tm - 1, axis=0))
    conv = up * cw_ref[0:1, :] + cx * cw_ref[1:2, :] + dn * cw_ref[2:3, :]
    y_a = ga_ref[...].astype(F32) * conv
    y_b = f_ref[...].astype(F32) * gb_ref[...].astype(F32)
    y = jnp.concatenate([y_a, y_b], axis=1).astype(BF16)
    qw = N_HEADS * HEAD_DIM
    kw = N_KV_HEADS * HEAD_DIM
    ts = tm // MID_SUBTILES
    even_lane = (lax.broadcasted_iota(jnp.int32, (ts, HEAD_DIM), 1) & 1) == 0
    ones_pad = (lax.broadcasted_iota(jnp.int32, (V_PAD_ROWS, ts), 0) == 0).astype(v_ref.dtype)
    shift = _score_shift(qg_ref[...], kg_ref[...], q_scale)
    q_ext = (lax.broadcasted_iota(jnp.int32, (HEAD_DIM, ts), 0) == 0).astype(q_ref.dtype)
    k_ext = jnp.where(lax.broadcasted_iota(jnp.int32, (ts, HEAD_DIM), 1) == 0, -shift, 0.0).astype(k_ref.dtype)
    for sub in range(MID_SUBTILES):
        rows = slice(sub * ts, (sub + 1) * ts)
        x1 = x_ref[rows, :] + jnp.dot(y[rows], wob_ref[...], preferred_element_type=F32)
        x1_ref[rows, :] = x1
        h = _rms_norm(x1, g_ref[...]).astype(BF16)
        p = jnp.dot(h, wib_ref[...], preferred_element_type=F32)
        cos = cos_ref[rows, :]
        sin = sin_ref[rows, :]
        for hh in range(N_HEADS):
            qh = _norm_rope(p[:, hh * HEAD_DIM:(hh + 1) * HEAD_DIM], qg_ref[...], cos, sin, even_lane)
            q_ref[0, hh, :, rows] = jnp.concatenate([(qh * q_scale).T.astype(q_ref.dtype), q_ext], axis=0)
        for hh in range(N_KV_HEADS):
            kh = _norm_rope(p[:, qw + hh * HEAD_DIM:qw + (hh + 1) * HEAD_DIM], kg_ref[...], cos, sin, even_lane)
            k_ref[0, hh, rows, :] = jnp.concatenate([kh.astype(k_ref.dtype), k_ext], axis=1)
            vt = p[:, qw + kw + hh * HEAD_DIM:qw + kw + (hh + 1) * HEAD_DIM].T.astype(v_ref.dtype)
            v_ref[0, hh, :, rows] = jnp.concatenate([vt, ones_pad], axis=0)
        z_ref[rows, :] = p[:, qw + 2 * kw:].astype(z_ref.dtype)


def _mid(x2, cx, ga, f2, gb, conv_w, w_out, g_odd, w_in, q_gain, k_gain, cos, sin, bsz, seq, tm, q_scale):
    n, d = x2.shape
    assert conv_w.shape[0] == 3, "the halo blocks supply one neighbour row on each side"
    nt = seq // tm
    halo = 8
    tpb = tm // halo
    last_blk = n // halo - 1
    row = lambda b, i: (b * nt + i, 0)
    fixed = lambda b, i: (0, 0)
    prev = lambda b, i: (jnp.maximum((b * nt + i) * tpb - 1, 0), 0)
    nxt = lambda b, i: (jnp.minimum((b * nt + i + 1) * tpb, last_blk), 0)
    pos = lambda b, i: (i, 0)
    head = lambda b, i: (b, 0, i, 0)
    head_t = lambda b, i: (b, 0, 0, i)
    half = pl.BlockSpec((tm, A_WIDTH), row)
    return pl.pallas_call(
        functools.partial(_mid_kernel, q_scale=q_scale),
        grid=(bsz, nt),
        in_specs=[pl.BlockSpec((tm, d), row),
                  half,
                  pl.BlockSpec((halo, A_WIDTH), prev),
                  pl.BlockSpec((halo, A_WIDTH), nxt),
                  half, half, half,
                  pl.BlockSpec(conv_w.shape, fixed),
                  _resident(w_out.shape),
                  pl.BlockSpec((1, d), fixed),
                  _resident(w_in.shape),
                  pl.BlockSpec((1, HEAD_DIM), fixed),
                  pl.BlockSpec((1, HEAD_DIM), fixed),
                  pl.BlockSpec((tm, HEAD_DIM), pos),
                  pl.BlockSpec((tm, HEAD_DIM), pos)],
        out_specs=[pl.BlockSpec((tm, d), row),
                   pl.BlockSpec((1, N_HEADS, 2 * HEAD_DIM, tm), head_t),
                   pl.BlockSpec((1, N_KV_HEADS, tm, 2 * HEAD_DIM), head),
                   pl.BlockSpec((1, N_KV_HEADS, HEAD_DIM + V_PAD_ROWS, tm), head_t),
                   pl.BlockSpec((tm, N_HEADS * HEAD_DIM), row)],
        out_shape=[jax.ShapeDtypeStruct((n, d), F32),
                   jax.ShapeDtypeStruct((bsz, N_HEADS, 2 * HEAD_DIM, seq), BF16),
                   jax.ShapeDtypeStruct((bsz, N_KV_HEADS, seq, 2 * HEAD_DIM), BF16),
                   jax.ShapeDtypeStruct((bsz, N_KV_HEADS, HEAD_DIM + V_PAD_ROWS, seq), BF16),
                   jax.ShapeDtypeStruct((n, N_HEADS * HEAD_DIM), BF16)],
        scratch_shapes=[pltpu.VMEM(w_out.shape, BF16), pltpu.VMEM(w_in.shape, BF16)],
        compiler_params=_params(2),
        name="even_out_odd_in",
    )(x2, cx, cx, cx, ga, f2, gb, conv_w, w_out, g_odd, w_in, q_gain, k_gain, cos, sin)


def _attn_kernel(q_ref, k_ref, v_ref, qg_ref, kg_ref, o_ref, acc_ref, *, tk, q_scale):
    _, grp, _, tq = q_ref.shape
    hd = HEAD_DIM
    seq = k_ref.shape[2]
    n_chunks = seq // tk
    v_rows = v_ref.shape[2]
    shift = _score_shift(qg_ref[...], kg_ref[...], q_scale)

    def chunk(j):
        start = pl.multiple_of(j * tk, tk)
        return k_ref[0, 0, pl.ds(start, tk), :], v_ref[0, 0, :, pl.ds(start, tk)]

    bounded = shift <= SCORE_SPAN_LIMIT

    @pl.when(bounded)
    def _bounded_shift():
        q_all = jnp.concatenate([q_ref[0, g] for g in range(grp)], axis=1)

        def body(j, acc):
            kc, vc = chunk(j)
            s = jnp.dot(kc, q_all, preferred_element_type=F32)
            p = jnp.exp2(s).astype(BF16)
            return acc + jnp.dot(vc, p, preferred_element_type=F32)
        acc = lax.fori_loop(0, n_chunks, body, jnp.zeros((v_rows, grp * tq), F32), unroll=True)
        for g in range(grp):
            acc_ref[g] = acc[:, g * tq:(g + 1) * tq]

    @pl.when(jnp.logical_not(bounded))
    def _online_max():
        for g in range(grp):
            def body(j, carry):
                m, acc = carry
                kc, vc = chunk(j)
                s = jnp.dot(kc, q_ref[0, g], preferred_element_type=F32)
                m_new = jnp.maximum(m, jnp.max(s, axis=0, keepdims=True))
                p = jnp.exp2(s - m_new).astype(BF16)
                acc = jnp.exp2(m - m_new) * acc + jnp.dot(vc, p, preferred_element_type=F32)
                return m_new, acc
            init = (jnp.full((1, tq), -jnp.inf, F32), jnp.zeros((v_rows, tq), F32))
            _, acc = lax.fori_loop(0, n_chunks, body, init)
            acc_ref[g] = acc

    for g in range(grp):
        acc = acc_ref[g]
        o_t = acc[:hd] / acc[hd:hd + 1]
        o_ref[0, :, g * hd:(g + 1) * hd] = o_t.T.astype(o_ref.dtype)


def _attention(qt, k, vt, q_gain, k_gain, tq, tk, q_scale):
    bsz, nh, kd, seq = qt.shape
    hd = HEAD_DIM
    nkv = k.shape[1]
    grp = nh // nkv
    v_rows = vt.shape[2]
    return pl.pallas_call(
        functools.partial(_attn_kernel, tk=tk, q_scale=q_scale),
        grid=(bsz, nkv, seq // tq),
        in_specs=[pl.BlockSpec((1, grp, kd, tq), lambda b, h, i: (b, h, 0, i)),
                  pl.BlockSpec((1, 1, seq, kd), lambda b, h, i: (b, h, 0, 0)),
                  pl.BlockSpec((1, 1, v_rows, seq), lambda b, h, i: (b, h, 0, 0)),
                  pl.BlockSpec((1, hd), lambda b, h, i: (0, 0)),
                  pl.BlockSpec((1, hd), lambda b, h, i: (0, 0))],
        out_specs=pl.BlockSpec((1, tq, grp * hd), lambda b, h, i: (b, i, h)),
        out_shape=jax.ShapeDtypeStruct((bsz, seq, nh * hd), BF16),
        scratch_shapes=[pltpu.VMEM((grp, v_rows, tq), F32)],
        compiler_params=_params(3),
        name="gqa_attention",
    )(qt, k, vt, q_gain, k_gain)


def _odd_out_kernel(x1_ref, o_ref, z_ref, w_ref, g_ref, out_ref, wb_ref):
    _cast_once(pl.program_id(0) == 0, w_ref, wb_ref)
    y = (o_ref[...].astype(F32) * _silu(z_ref[...].astype(F32))).astype(BF16)
    x2 = x1_ref[...] + jnp.dot(y, wb_ref[...], preferred_element_type=F32)
    out_ref[...] = _rms_norm(x2, g_ref[...])


def _odd_out(x1, o2, z, w, g, tm):
    n, d = x1.shape
    row = lambda i: (i, 0)
    fixed = lambda i: (0, 0)
    return pl.pallas_call(
        _odd_out_kernel,
        grid=(n // tm,),
        in_specs=[pl.BlockSpec((tm, d), row),
                  pl.BlockSpec((tm, o2.shape[1]), row),
                  pl.BlockSpec((tm, z.shape[1]), row),
                  _resident(w.shape),
                  pl.BlockSpec((1, d), fixed)],
        out_specs=pl.BlockSpec((tm, d), row),
        out_shape=jax.ShapeDtypeStruct((n, d), F32),
        scratch_shapes=[pltpu.VMEM(w.shape, BF16)],
        compiler_params=_params(1),
        name="odd_out",
    )(x1, o2, z, w, g)


class _Tiles(NamedTuple):
    proj_rows: int
    mid_rows: int
    dft_rows: int
    queries: int
    keys: int


def _tiles(seq):
    assert seq % FFT_FAST == 0 and seq % GRID_W == 0, seq
    t = _Tiles(proj_rows=min(1024, seq), mid_rows=min(512, seq), dft_rows=min(DFT_ROWS_PER_STEP, seq // FFT_FAST),
               queries=min(512, seq), keys=min(512, seq))
    assert all(seq % v == 0 for v in (t.proj_rows, t.mid_rows, t.queries, t.keys)), (seq, t)
    assert (seq // FFT_FAST) % t.dft_rows == 0 and FFT_FAST % DFT_ROWS_PER_STEP == 0, (seq, t)
    return t


def kernel(x, norm_even, w_in_even, conv_w, w_out_even, norm_odd, w_in_odd, q_gain, k_gain, w_out_odd, final_norm):
    bsz, seq, d = x.shape
    assert norm_even.shape[0] == 1 and norm_odd.shape[0] == 1, "one even and one odd layer"
    assert conv_w.shape[1:] == (CONV_WIDTH, A_WIDTH) and w_in_even.shape[2] == 4 * A_WIDTH + 2 * B_WIDTH
    assert w_in_odd.shape[2] == 2 * (N_HEADS + N_KV_HEADS) * HEAD_DIM and d == N_HEADS * HEAD_DIM
    n = bsz * seq
    slow = seq // FFT_FAST
    t = _tiles(seq)
    x2 = x.reshape(n, d)

    cx, ga, bu, gb = _even_in(x2, norm_even[0][None, :], w_in_even[0], t.proj_rows)
    w1, t2, wc = _dft_tables(seq)
    scale = 1.0 / math.sqrt(seq * B_GROUP_DIM)
    f = _fourier_mix(w1, t2, wc, bu.reshape(bsz, slow, FFT_FAST, B_WIDTH),
                     nb=DFT_ROWS_PER_STEP, cc=t.dft_rows, scale=scale)
    f2 = f.reshape(n, B_WIDTH)

    cos, sin = _rope_tables(seq)
    q_scale = HEAD_DIM ** -0.5 * math.log2(math.e)
    x1, qt, k, vt, z = _mid(x2, cx, ga, f2, gb, conv_w[0], w_out_even[0],
                                 norm_odd[0][None, :], w_in_odd[0],
                                 q_gain[0][None, :], k_gain[0][None, :], cos, sin, bsz, seq, t.mid_rows, q_scale)

    o = _attention(qt, k, vt, q_gain[0][None, :], k_gain[0][None, :], tq=t.queries, tk=t.keys, q_scale=q_scale)
    out = _odd_out(x1, o.reshape(n, N_HEADS * HEAD_DIM), z, w_out_odd[0], final_norm[None, :], t.proj_rows)
    return out.reshape(bsz, seq, d)
```

```python
import functools
import math
from typing import NamedTuple

import numpy as np
import jax
import jax.numpy as jnp
from jax import lax
from jax.experimental import pallas as pl
from jax.experimental.pallas import tpu as pltpu

F32 = jnp.float32
BF16 = jnp.bfloat16

EPS = 1e-6
GRID_W = 64
ROPE_THETA = 10000.0
CONV_WIDTH = 3
HEAD_DIM = 128
N_HEADS = 8
N_KV_HEADS = 2
A_WIDTH = 512
B_WIDTH = 512
B_GROUPS = 4
B_GROUP_DIM = 128
FFT_FAST = 128
DFT_ROWS_PER_STEP = 16
V_PAD_ROWS = 16
SCORE_SPAN_LIMIT = 50.0
SHIFT_MARGIN = 1.01
MID_SUBTILES = 2

VMEM_LIMIT_BYTES = 56 * 1024 * 1024


def _silu(z):
    half = 0.5 * z
    return half + half * jnp.tanh(half)


def _rms_norm(x, g):
    return x * lax.rsqrt(jnp.mean(x * x, axis=-1, keepdims=True) + EPS) * g


def _score_shift(q_gain, k_gain, q_scale):
    return (HEAD_DIM * q_scale * SHIFT_MARGIN) * jnp.max(jnp.abs(q_gain)) * jnp.max(jnp.abs(k_gain))


def _resident(shape):
    zeros = (0,) * len(shape)
    return pl.BlockSpec(shape, lambda *_: zeros, pipeline_mode=pl.Buffered(1))


def _cast_once(first_step, src_ref, dst_ref, row_gain_ref=None):
    @pl.when(first_step)
    def _():
        w = src_ref[...]
        if row_gain_ref is not None:
            w = w * row_gain_ref[...]
        dst_ref[...] = w.astype(dst_ref.dtype)


def _params(n_grid_dims):
    return pltpu.CompilerParams(
        dimension_semantics=("arbitrary",) * n_grid_dims,
        vmem_limit_bytes=VMEM_LIMIT_BYTES)


def _even_in_kernel(x_ref, g_ref, w_ref, cx_ref, ga_ref, bu_ref, gb_ref, wb_ref):
    _cast_once(pl.program_id(0) == 0, w_ref, wb_ref, g_ref)
    h = _rms_norm(x_ref[...], 1.0).astype(BF16)
    p = jnp.dot(h, wb_ref[...], preferred_element_type=F32)
    a_x = p[:, 0 * A_WIDTH:1 * A_WIDTH]
    a_b = p[:, 1 * A_WIDTH:2 * A_WIDTH]
    a_c = p[:, 2 * A_WIDTH:3 * A_WIDTH]
    a_z = p[:, 3 * A_WIDTH:4 * A_WIDTH]
    b_u = p[:, 4 * A_WIDTH:4 * A_WIDTH + B_WIDTH]
    b_z = p[:, 4 * A_WIDTH + B_WIDTH:]
    cx_ref[...] = (a_c * a_x).astype(cx_ref.dtype)
    ga_ref[...] = (a_b * _silu(a_z)).astype(ga_ref.dtype)
    bu_ref[...] = b_u.astype(bu_ref.dtype)
    gb_ref[...] = _silu(b_z).astype(gb_ref.dtype)


def _even_in(x2, g, w, tm):
    n, d = x2.shape
    e_in = w.shape[1]
    out = jax.ShapeDtypeStruct((n, A_WIDTH), BF16)
    row = lambda i: (i, 0)
    fixed = lambda i: (0, 0)
    return pl.pallas_call(
        _even_in_kernel,
        grid=(n // tm,),
        in_specs=[pl.BlockSpec((tm, d), row),
                  pl.BlockSpec((d, 1), fixed),
                  _resident((d, e_in))],
        out_specs=[pl.BlockSpec((tm, A_WIDTH), row)] * 4,
        out_shape=[out] * 4,
        scratch_shapes=[pltpu.VMEM((d, e_in), BF16)],
        compiler_params=_params(1),
        name="even_in",
    )(x2, g, w)


def _dft_tables(seq):
    slow = seq // FFT_FAST
    a = np.arange(slow)
    th = 2.0 * np.pi * np.outer(a, a) / slow
    w1 = np.concatenate([np.cos(th), -np.sin(th)], axis=0)
    b = np.arange(FFT_FAST)
    k = a[:, None, None] + slow * b[None, :, None]
    th2 = 2.0 * np.pi * (k * b[None, None, :]) / seq
    cs, sn = np.cos(th2), np.sin(th2)
    t2 = np.concatenate([np.concatenate([cs, sn], axis=2),
                         np.concatenate([-sn, cs], axis=2)], axis=1)
    ch = np.arange(B_GROUP_DIM)
    thc = 2.0 * np.pi * np.outer(ch, ch) / B_GROUP_DIM
    wc = np.concatenate([np.cos(thc), np.sin(thc)], axis=0)
    return tuple(jnp.asarray(t, dtype=F32) for t in (w1, t2, wc))


def _dft_kernel(w1_ref, u_ref, t_ref, wc_ref, f_ref, y_ref, *, nb, cc, scale):
    _, slow, fast, width = u_ref.shape
    gd = B_GROUP_DIM

    @pl.when(pl.program_id(1) == 0)
    def _stage1():
        w1 = w1_ref[...].astype(BF16)

        def b_block(i, carry):
            cols = pl.ds(pl.multiple_of(i * nb, nb), nb)
            ut = jnp.transpose(u_ref[0, :, cols, :], (1, 0, 2))
            x = jnp.concatenate([ut[b] for b in range(nb)], axis=1)
            yb = jnp.dot(w1, x, preferred_element_type=F32).astype(y_ref.dtype)
            for part in range(2):
                rows = slice(part * slow, (part + 1) * slow)
                by_b = jnp.stack([yb[rows, b * width:(b + 1) * width] for b in range(nb)], axis=0)
                y_ref[part, :, cols, :] = jnp.transpose(by_b, (1, 0, 2))
            return carry
        lax.fori_loop(0, fast // nb, b_block, 0)

    wc = wc_ref[...].astype(BF16)
    c0 = pl.program_id(1) * cc
    by_c = []
    for ci in range(cc):
        ys = jnp.concatenate([y_ref[0, c0 + ci], y_ref[1, c0 + ci]], axis=0)
        p = jnp.dot(t_ref[ci].astype(BF16), ys, preferred_element_type=F32).astype(BF16)
        lhs = jnp.concatenate(
            [jnp.concatenate([p[:fast, g * gd:(g + 1) * gd], p[fast:, g * gd:(g + 1) * gd]], axis=1)
             for g in range(B_GROUPS)], axis=0)
        f = jnp.dot(lhs, wc, preferred_element_type=F32) * scale
        by_c.append(jnp.concatenate([f[g * fast:(g + 1) * fast] for g in range(B_GROUPS)],
                                    axis=1).astype(f_ref.dtype))
    f_ref[0] = jnp.transpose(jnp.stack(by_c, axis=0), (1, 0, 2))


def _fourier_mix(w1, t2, wc, u4, nb, cc, scale):
    bsz, slow, fast, width = u4.shape
    return pl.pallas_call(
        functools.partial(_dft_kernel, nb=nb, cc=cc, scale=scale),
        grid=(bsz, slow // cc),
        in_specs=[pl.BlockSpec(w1.shape, lambda b, j: (0, 0)),
                  pl.BlockSpec((1, slow, fast, width), lambda b, j: (b, 0, 0, 0)),
                  pl.BlockSpec((cc, 2 * fast, 2 * fast), lambda b, j: (j, 0, 0)),
                  pl.BlockSpec(wc.shape, lambda b, j: (0, 0))],
        out_specs=pl.BlockSpec((1, fast, cc, width), lambda b, j: (b, 0, j, 0)),
        out_shape=jax.ShapeDtypeStruct((bsz, fast, slow, width), BF16),
        scratch_shapes=[pltpu.VMEM((2, slow, fast, width), BF16)],
        compiler_params=_params(2),
        name="fourier_mix",
    )(w1, u4, t2, wc)


def _rope_tables(seq):
    t = np.arange(seq)
    row = (t // GRID_W).astype(np.float64)
    col = (t % GRID_W).astype(np.float64)
    n_pair = HEAD_DIM // 4
    inv = ROPE_THETA ** (-np.arange(n_pair, dtype=np.float64) / n_pair)
    ang = np.concatenate([row[:, None] * inv, col[:, None] * inv], axis=-1)
    cos = np.repeat(np.cos(ang), 2, axis=1)
    sin = np.repeat(np.sin(ang), 2, axis=1)
    sign = np.tile(np.array([-1.0, 1.0]), HEAD_DIM // 2)
    return jnp.asarray(cos, dtype=F32), jnp.asarray(sin * sign, dtype=F32)


def _norm_rope(xh, gain, cos, sin_signed, even_lane):
    xn = _rms_norm(xh, gain)
    partner = jnp.where(even_lane, pltpu.roll(xn, HEAD_DIM - 1, axis=1), pltpu.roll(xn, 1, axis=1))
    return xn * cos + partner * sin_signed


def _mid_kernel(x_ref, cx_ref, cxp_ref, cxn_ref, ga_ref, f_ref, gb_ref, cw_ref, wo_ref,
                g_ref, wi_ref, qg_ref, kg_ref, cos_ref, sin_ref,
                x1_ref, q_ref, k_ref, v_ref, z_ref, wob_ref, wib_ref, *, q_scale):
    first_step = (pl.program_id(0) == 0) & (pl.program_id(1) == 0)
    _cast_once(first_step, wo_ref, wob_ref)
    _cast_once(first_step, wi_ref, wib_ref, g_ref)
    i = pl.program_id(1)
    n_i = pl.num_programs(1)
    tm = x_ref.shape[0]
    cx = cx_ref[...].astype(F32)
    prev_row = jnp.where(i > 0, cxp_ref[7:8, :].astype(F32), 0.0)
    next_row = jnp.where(i < n_i - 1, cxn_ref[0:1, :].astype(F32), 0.0)
    r = lax.broadcasted_iota(jnp.int32, cx.shape, 0)
    up = jnp.where(r == 0, prev_row, pltpu.roll(cx, 1, axis=0))
    dn = jnp.where(r == tm - 1, next_row, pltpu.roll(cx, tm - 1, axis=0))
    conv = up * cw_ref[0:1, :] + cx * cw_ref[1:2, :] + dn * cw_ref[2:3, :]
    y_a = ga_ref[...].astype(F32) * conv
    y_b = f_ref[...].astype(F32) * gb_ref[...].astype(F32)
    y = jnp.concatenate([y_a, y_b], axis=1).astype(BF16)
    qw = N_HEADS * HEAD_DIM
    kw = N_KV_HEADS * HEAD_DIM
    ts = tm // MID_SUBTILES
    even_lane = (lax.broadcasted_iota(jnp.int32, (ts, HEAD_DIM), 1) & 1) == 0
    ones_pad = (lax.broadcasted_iota(jnp.int32, (V_PAD_ROWS, ts), 0) == 0).astype(v_ref.dtype)
    shift = _score_shift(qg_ref[...], kg_ref[...], q_scale)
    q_ext = (lax.broadcasted_iota(jnp.int32, (HEAD_DIM, ts), 0) == 0).astype(q_ref.dtype)
    k_ext = jnp.where(lax.broadcasted_iota(jnp.int32, (ts, HEAD_DIM), 1) == 0, -shift, 0.0).astype(k_ref.dtype)
    for sub in range(MID_SUBTILES):
        rows = slice(sub * ts, (sub + 1) * ts)
        x1 = x_ref[rows, :] + jnp.dot(y[rows], wob_ref[...], preferred_element_type=F32)
        x1_ref[rows, :] = x1
        h = _rms_norm(x1, 1.0).astype(BF16)
        p = jnp.dot(h, wib_ref[...], preferred_element_type=F32)
        cos = cos_ref[rows, :]
        sin = sin_ref[rows, :]
        for hh in range(N_HEADS):
            qh = _norm_rope(p[:, hh * HEAD_DIM:(hh + 1) * HEAD_DIM], qg_ref[...], cos, sin, even_lane)
            q_ref[0, hh, :, rows] = jnp.concatenate([(qh * q_scale).T.astype(q_ref.dtype), q_ext], axis=0)
        for hh in range(N_KV_HEADS):
            kh = _norm_rope(p[:, qw + hh * HEAD_DIM:qw + (hh + 1) * HEAD_DIM], kg_ref[...], cos, sin, even_lane)
            k_ref[0, hh, rows, :] = jnp.concatenate([kh.astype(k_ref.dtype), k_ext], axis=1)
            vt = p[:, qw + kw + hh * HEAD_DIM:qw + kw + (hh + 1) * HEAD_DIM].T.astype(v_ref.dtype)
            v_ref[0, hh, :, rows] = jnp.concatenate([vt, ones_pad], axis=0)
        z_ref[rows, :] = p[:, qw + 2 * kw:].astype(z_ref.dtype)


def _mid(x2, cx, ga, f2, gb, conv_w, w_out, g_odd, w_in, q_gain, k_gain, cos, sin, bsz, seq, tm, q_scale):
    n, d = x2.shape
    assert conv_w.shape[0] == 3, "the halo blocks supply one neighbour row on each side"
    nt = seq // tm
    halo = 8
    tpb = tm // halo
    last_blk = n // halo - 1
    row = lambda b, i: (b * nt + i, 0)
    fixed = lambda b, i: (0, 0)
    prev = lambda b, i: (jnp.maximum((b * nt + i) * tpb - 1, 0), 0)
    nxt = lambda b, i: (jnp.minimum((b * nt + i + 1) * tpb, last_blk), 0)
    pos = lambda b, i: (i, 0)
    head = lambda b, i: (b, 0, i, 0)
    head_t = lambda b, i: (b, 0, 0, i)
    half = pl.BlockSpec((tm, A_WIDTH), row)
    return pl.pallas_call(
        functools.partial(_mid_kernel, q_scale=q_scale),
        grid=(bsz, nt),
        in_specs=[pl.BlockSpec((tm, d), row),
                  half,
                  pl.BlockSpec((halo, A_WIDTH), prev),
                  pl.BlockSpec((halo, A_WIDTH), nxt),
                  half, half, half,
                  pl.BlockSpec(conv_w.shape, fixed),
                  _resident(w_out.shape),
                  pl.BlockSpec((d, 1), fixed),
                  _resident(w_in.shape),
                  pl.BlockSpec((1, HEAD_DIM), fixed),
                  pl.BlockSpec((1, HEAD_DIM), fixed),
                  pl.BlockSpec((tm, HEAD_DIM), pos),
                  pl.BlockSpec((tm, HEAD_DIM), pos)],
        out_specs=[pl.BlockSpec((tm, d), row),
                   pl.BlockSpec((1, N_HEADS, 2 * HEAD_DIM, tm), head_t),
                   pl.BlockSpec((1, N_KV_HEADS, tm, 2 * HEAD_DIM), head),
                   pl.BlockSpec((1, N_KV_HEADS, HEAD_DIM + V_PAD_ROWS, tm), head_t),
                   pl.BlockSpec((tm, N_HEADS * HEAD_DIM), row)],
        out_shape=[jax.ShapeDtypeStruct((n, d), F32),
                   jax.ShapeDtypeStruct((bsz, N_HEADS, 2 * HEAD_DIM, seq), BF16),
                   jax.ShapeDtypeStruct((bsz, N_KV_HEADS, seq, 2 * HEAD_DIM), BF16),
                   jax.ShapeDtypeStruct((bsz, N_KV_HEADS, HEAD_DIM + V_PAD_ROWS, seq), BF16),
                   jax.ShapeDtypeStruct((n, N_HEADS * HEAD_DIM), BF16)],
        scratch_shapes=[pltpu.VMEM(w_out.shape, BF16), pltpu.VMEM(w_in.shape, BF16)],
        compiler_params=_params(2),
        name="even_out_odd_in",
    )(x2, cx, cx, cx, ga, f2, gb, conv_w, w_out, g_odd, w_in, q_gain, k_gain, cos, sin)


def _attn_kernel(q_ref, k_ref, v_ref, qg_ref, kg_ref, o_ref, acc_ref, *, tk, q_scale):
    _, grp, _, tq = q_ref.shape
    hd = HEAD_DIM
    seq = k_ref.shape[2]
    n_chunks = seq // tk
    v_rows = v_ref.shape[2]
    shift = _score_shift(qg_ref[...], kg_ref[...], q_scale)

    def chunk(j):
        start = pl.multiple_of(j * tk, tk)
        return k_ref[0, 0, pl.ds(start, tk), :], v_ref[0, 0, :, pl.ds(start, tk)]

    bounded = shift <= SCORE_SPAN_LIMIT

    @pl.when(bounded)
    def _bounded_shift():
        q_all = jnp.concatenate([q_ref[0, g] for g in range(grp)], axis=1)

        def body(j, acc):
            kc, vc = chunk(j)
            s = jnp.dot(kc, q_all, preferred_element_type=F32)
            p = jnp.exp2(s).astype(BF16)
            return acc + jnp.dot(vc, p, preferred_element_type=F32)
        acc = lax.fori_loop(0, n_chunks, body, jnp.zeros((v_rows, grp * tq), F32), unroll=True)
        for g in range(grp):
            acc_ref[g] = acc[:, g * tq:(g + 1) * tq]

    @pl.when(jnp.logical_not(bounded))
    def _online_max():
        for g in range(grp):
            def body(j, carry):
                m, acc = carry
                kc, vc = chunk(j)
                s = jnp.dot(kc, q_ref[0, g], preferred_element_type=F32)
                m_new = jnp.maximum(m, jnp.max(s, axis=0, keepdims=True))
                p = jnp.exp2(s - m_new).astype(BF16)
                acc = jnp.exp2(m - m_new) * acc + jnp.dot(vc, p, preferred_element_type=F32)
                return m_new, acc
            init = (jnp.full((1, tq), -jnp.inf, F32), jnp.zeros((v_rows, tq), F32))
            _, acc = lax.fori_loop(0, n_chunks, body, init)
            acc_ref[g] = acc

    for g in range(grp):
        acc = acc_ref[g]
        o_t = acc[:hd] / acc[hd:hd + 1]
        o_ref[0, :, g * hd:(g + 1) * hd] = o_t.T.astype(o_ref.dtype)


def _attention(qt, k, vt, q_gain, k_gain, tq, tk, q_scale):
    bsz, nh, kd, seq = qt.shape
    hd = HEAD_DIM
    nkv = k.shape[1]
    grp = nh // nkv
    v_rows = vt.shape[2]
    return pl.pallas_call(
        functools.partial(_attn_kernel, tk=tk, q_scale=q_scale),
        grid=(bsz, nkv, seq // tq),
        in_specs=[pl.BlockSpec((1, grp, kd, tq), lambda b, h, i: (b, h, 0, i)),
                  pl.BlockSpec((1, 1, seq, kd), lambda b, h, i: (b, h, 0, 0)),
                  pl.BlockSpec((1, 1, v_rows, seq), lambda b, h, i: (b, h, 0, 0)),
                  pl.BlockSpec((1, hd), lambda b, h, i: (0, 0)),
                  pl.BlockSpec((1, hd), lambda b, h, i: (0, 0))],
        out_specs=pl.BlockSpec((1, tq, grp * hd), lambda b, h, i: (b, i, h)),
        out_shape=jax.ShapeDtypeStruct((bsz, seq, nh * hd), BF16),
        scratch_shapes=[pltpu.VMEM((grp, v_rows, tq), F32)],
        compiler_params=_params(3),
        name="gqa_attention",
    )(qt, k, vt, q_gain, k_gain)


def _odd_out_kernel(x1_ref, o_ref, z_ref, w_ref, g_ref, out_ref, wb_ref):
    _cast_once(pl.program_id(0) == 0, w_ref, wb_ref)
    y = (o_ref[...].astype(F32) * _silu(z_ref[...].astype(F32))).astype(BF16)
    x2 = x1_ref[...] + jnp.dot(y, wb_ref[...], preferred_element_type=F32)
    out_ref[...] = _rms_norm(x2, g_ref[...])


def _odd_out(x1, o2, z, w, g, tm):
    n, d = x1.shape
    row = lambda i: (i, 0)
    fixed = lambda i: (0, 0)
    return pl.pallas_call(
        _odd_out_kernel,
        grid=(n // tm,),
        in_specs=[pl.BlockSpec((tm, d), row),
                  pl.BlockSpec((tm, o2.shape[1]), row),
                  pl.BlockSpec((tm, z.shape[1]), row),
                  _resident(w.shape),
                  pl.BlockSpec((1, d), fixed)],
        out_specs=pl.BlockSpec((tm, d), row),
        out_shape=jax.ShapeDtypeStruct((n, d), F32),
        scratch_shapes=[pltpu.VMEM(w.shape, BF16)],
        compiler_params=_params(1),
        name="odd_out",
    )(x1, o2, z, w, g)


class _Tiles(NamedTuple):
    proj_rows: int
    mid_rows: int
    dft_rows: int
    queries: int
    keys: int


def _tiles(seq):
    assert seq % FFT_FAST == 0 and seq % GRID_W == 0, seq
    t = _Tiles(proj_rows=min(1024, seq), mid_rows=min(512, seq), dft_rows=min(DFT_ROWS_PER_STEP, seq // FFT_FAST),
               queries=min(512, seq), keys=min(512, seq))
    assert all(seq % v == 0 for v in (t.proj_rows, t.mid_rows, t.queries, t.keys)), (seq, t)
    assert (seq // FFT_FAST) % t.dft_rows == 0 and FFT_FAST % DFT_ROWS_PER_STEP == 0, (seq, t)
    return t


def kernel(x, norm_even, w_in_even, conv_w, w_out_even, norm_odd, w_in_odd, q_gain, k_gain, w_out_odd, final_norm):
    bsz, seq, d = x.shape
    assert norm_even.shape[0] == 1 and norm_odd.shape[0] == 1, "one even and one odd layer"
    assert conv_w.shape[1:] == (CONV_WIDTH, A_WIDTH) and w_in_even.shape[2] == 4 * A_WIDTH + 2 * B_WIDTH
    assert w_in_odd.shape[2] == 2 * (N_HEADS + N_KV_HEADS) * HEAD_DIM and d == N_HEADS * HEAD_DIM
    n = bsz * seq
    slow = seq // FFT_FAST
    t = _tiles(seq)
    x2 = x.reshape(n, d)

    cx, ga, bu, gb = _even_in(x2, norm_even[0][:, None], w_in_even[0], t.proj_rows)
    w1, t2, wc = _dft_tables(seq)
    scale = 1.0 / math.sqrt(seq * B_GROUP_DIM)
    f = _fourier_mix(w1, t2, wc, bu.reshape(bsz, slow, FFT_FAST, B_WIDTH),
                     nb=DFT_ROWS_PER_STEP, cc=t.dft_rows, scale=scale)
    f2 = f.reshape(n, B_WIDTH)

    cos, sin = _rope_tables(seq)
    q_scale = HEAD_DIM ** -0.5 * math.log2(math.e)
    x1, qt, k, vt, z = _mid(x2, cx, ga, f2, gb, conv_w[0], w_out_even[0],
                                 norm_odd[0][:, None], w_in_odd[0],
                                 q_gain[0][None, :], k_gain[0][None, :], cos, sin, bsz, seq, t.mid_rows, q_scale)

    o = _attention(qt, k, vt, q_gain[0][None, :], k_gain[0][None, :], tq=t.queries, tk=t.keys, q_scale=q_scale)
    out = _odd_out(x1, o.reshape(n, N_HEADS * HEAD_DIM), z, w_out_odd[0], final_norm[None, :], t.proj_rows)
    return out.reshape(bsz, seq, d)
```
